```python
import math
import jax, jax.numpy as jnp
from jax import lax
import numpy as np

D_MODEL = 1024
BATCH = 8
SEQ = 4096
DEPTH = 1

CTX_LEN = 256
GRID_W = 64
D_CONV = D_MODEL
CONV_WIDTH = 3
GLA_HEADS = 4
GLA_DK = D_MODEL // (2 * GLA_HEADS)
GLA_DV = D_MODEL // GLA_HEADS
GLA_RANK = 16
GLA_TAU = 16.0
GLA_CHUNK = 64
QK_DIM = GLA_HEADS * GLA_DK
V_DIM = GLA_HEADS * GLA_DV
D_FF = 4 * D_MODEL
N_MOD = 6
RMS_EPS = 1e-6
IN_SPLITS = (D_CONV, D_CONV, D_CONV, QK_DIM, QK_DIM, V_DIM, V_DIM, GLA_RANK, GLA_RANK, D_MODEL, D_MODEL)
N_IN = sum(IN_SPLITS)

kernel_name = "hybrid_shortconv_gla_prefix_dit_block"


def rmsnorm(x, g):
    xf = x.astype(jnp.float32)
    y = xf * lax.rsqrt(jnp.mean(xf * xf, axis=-1, keepdims=True) + RMS_EPS)
    return (y * g.astype(jnp.float32)).astype(x.dtype)


def modulate(x, g, shift, scale):
    return rmsnorm(x, g) * (1.0 + scale) + shift


def in_proj(h, w, b):
    z = h @ w + b
    idx = []
    acc = 0
    for s in IN_SPLITS[:-1]:
        acc += s
        idx.append(acc)
    return jnp.split(z, idx, axis=-1)


def split_heads(a, d):
    b, t, _ = a.shape
    return a.reshape(b, t, GLA_HEADS, d).transpose(0, 2, 1, 3)


def conv3_grid(u, w):
    b, l, ch = u.shape
    rows = l // GRID_W
    g = jnp.pad(u.reshape(b, rows, GRID_W, ch), ((0, 0), (0, 0), (1, 1), (0, 0)))
    y = g[:, :, :-2] * w[0] + g[:, :, 1:-1] * w[1] + g[:, :, 2:] * w[2]
    return y.reshape(b, l, ch)


def conv3_seq(u, w):
    g = jnp.pad(u, ((0, 0), (1, 1), (0, 0)))
    return g[:, :-2] * w[0] + g[:, 1:-1] * w[1] + g[:, 2:] * w[2]


def gla_kvg(k, v, lr_f, lr_b, w_a2_f, b_a_f, w_a2_b, b_a_b):
    k = split_heads(k, GLA_DK)
    v = split_heads(v, GLA_DV)
    g_f = split_heads(jax.nn.log_sigmoid((lr_f @ w_a2_f + b_a_f).astype(jnp.float32)) / GLA_TAU, GLA_DK)
    g_b = split_heads(jax.nn.log_sigmoid((lr_b @ w_a2_b + b_a_b).astype(jnp.float32)) / GLA_TAU, GLA_DK)
    return k, v, g_f, g_b


def gla_final_state(k, v, g):
    gcum = jnp.cumsum(g, axis=2)
    wdec = jnp.exp(gcum[:, :, -1:, :] - gcum)
    return jnp.einsum('bhtd,bhtv->bhdv', k.astype(jnp.float32) * wdec, v.astype(jnp.float32))


def gla_chunk_scan(q, k, v, g, s0):
    b, h, t, dk = q.shape
    dv = v.shape[-1]
    n = t // GLA_CHUNK

    def to_chunks(a):
        return jnp.moveaxis(a.reshape(b, h, n, GLA_CHUNK, a.shape[-1]), 2, 0)

    mask = jnp.tril(jnp.ones((GLA_CHUNK, GLA_CHUNK), dtype=bool))[:, :, None]

    def step(s, inp):
        qc, kc, vc, gc = inp
        qc = qc.astype(jnp.float32)
        kc = kc.astype(jnp.float32)
        vc = vc.astype(jnp.float32)
        gcum = jnp.cumsum(gc, axis=2)
        diff = gcum[:, :, :, None, :] - gcum[:, :, None, :, :]
        decay = jnp.exp(jnp.where(mask, diff, -jnp.inf))
        scores = jnp.einsum('bhtd,bhtsd,bhsd->bhts', qc, decay, kc)
        o = (jnp.einsum('bhts,bhsv->bhtv', scores, vc)
             + jnp.einsum('bhtd,bhdv->bhtv', qc * jnp.exp(gcum), s))
        g_last = gcum[:, :, -1, :]
        s_new = (jnp.exp(g_last)[..., None] * s
                 + jnp.einsum('bhsd,bhsv->bhdv', kc * jnp.exp(g_last[:, :, None, :] - gcum), vc))
        return s_new, o

    _, o = lax.scan(step, s0.astype(jnp.float32), (to_chunks(q), to_chunks(k), to_chunks(v), to_chunks(g)))
    return jnp.moveaxis(o, 0, 2).reshape(b, h, t, dv)


def gla_bidir(q, k, v, g_f, g_b, s_f, s_b):
    def flip(a):
        return jnp.flip(a, axis=2)
    o_f = gla_chunk_scan(q, k, v, g_f, s_f)
    o_b = gla_chunk_scan(flip(q), flip(k), flip(v), flip(g_b), s_b)
    return o_f + flip(o_b)


def mixer_merge(a_x, a_b, a_c, r, gate_a, gate_b, o_gla, conv_w, w_conv_out, g_gla_norm, w_gla_out, conv_fn):
    y_a = (a_b * conv_fn(a_c * a_x, conv_w)) @ w_conv_out
    b, h, t, dv = o_gla.shape
    o = rmsnorm(o_gla.transpose(0, 2, 1, 3), g_gla_norm.reshape(GLA_HEADS, GLA_DV)).reshape(b, t, h * dv)
    y_b = (o.astype(r.dtype) * jax.nn.silu(r)) @ w_gla_out
    return jax.nn.sigmoid(gate_a) * y_a + jax.nn.sigmoid(gate_b) * y_b


def sq_relu_mlp(h, w_up, w_down):
    return jnp.square(jax.nn.relu(h @ w_up)) @ w_down


def setup_inputs(seed: int = 0) -> dict:
    key = jax.random.key(seed)
    ks = jax.random.split(key, 24)
    nrm = jax.random.normal
    f32 = jnp.float32
    d = D_MODEL
    return {
        "x": nrm(ks[0], (BATCH, SEQ, d), f32),
        "c": nrm(ks[1], (BATCH, d), f32),
        "ctx": nrm(ks[2], (BATCH, CTX_LEN, d), f32),
        "c_ctx": nrm(ks[3], (d,), f32),
        "w_ada": nrm(ks[4], (DEPTH, d, N_MOD * d), f32) * (0.5 * d ** -0.5),
        "b_ada": nrm(ks[5], (DEPTH, N_MOD * d), f32) * 0.01,
        "g_norm1": 1.0 + 0.01 * nrm(ks[6], (DEPTH, d), f32),
        "w_in": nrm(ks[7], (DEPTH, d, N_IN), f32) * d ** -0.5,
        "b_in": nrm(ks[8], (DEPTH, N_IN), f32) * 0.01,
        "conv_w": nrm(ks[9], (DEPTH, CONV_WIDTH, D_CONV), f32) * CONV_WIDTH ** -0.5,
        "w_conv_out": nrm(ks[10], (DEPTH, D_CONV, d), f32) * D_CONV ** -0.5,
        "w_a2_f": nrm(ks[11], (DEPTH, GLA_RANK, QK_DIM), f32) * GLA_RANK ** -0.5,
        "b_a_f": 1.0 + 0.1 * nrm(ks[12], (DEPTH, QK_DIM), f32),
        "w_a2_b": nrm(ks[13], (DEPTH, GLA_RANK, QK_DIM), f32) * GLA_RANK ** -0.5,
        "b_a_b": 1.0 + 0.1 * nrm(ks[14], (DEPTH, QK_DIM), f32),
        "g_gla_norm": 1.0 + 0.01 * nrm(ks[15], (DEPTH, V_DIM), f32),
        "w_gla_out": nrm(ks[16], (DEPTH, V_DIM, d), f32) * V_DIM ** -0.5,
        "w_o": nrm(ks[17], (DEPTH, d, d), f32) * d ** -0.5,
        "g_norm2": 1.0 + 0.01 * nrm(ks[18], (DEPTH, d), f32),
        "w_up": nrm(ks[19], (DEPTH, d, D_FF), f32) * d ** -0.5,
        "w_down": nrm(ks[20], (DEPTH, D_FF, d), f32) * D_FF ** -0.5,
        "g_final": 1.0 + 0.01 * nrm(ks[21], (d,), f32),
    }


def reference(x, c, ctx, c_ctx, w_ada, b_ada, g_norm1, w_in, b_in, conv_w, w_conv_out, w_a2_f, b_a_f,
              w_a2_b, b_a_b, g_gla_norm, w_gla_out, w_o, g_norm2, w_up, w_down, g_final):
    ctx_s = ctx
    for i in range(DEPTH):
        mod_lat = (jax.nn.silu(c) @ w_ada[i] + b_ada[i])[:, None, :]
        mod_ctx = jax.nn.silu(c_ctx) @ w_ada[i] + b_ada[i]
        sh1, sc1, ga1, sh2, sc2, ga2 = jnp.split(mod_lat, N_MOD, axis=-1)
        cmod = jnp.split(mod_ctx, N_MOD, axis=-1)

        (cx_a, cb_a, cc_a, cq, ck, cv, cr, clr_f, clr_b, cgate_a, cgate_b) = in_proj(
            modulate(ctx_s, g_norm1[i], cmod[0], cmod[1]), w_in[i], b_in[i])
        kc, vc, gfc, gbc = gla_kvg(ck, cv, clr_f, clr_b, w_a2_f[i], b_a_f[i], w_a2_b[i], b_a_b[i])
        s_f = gla_final_state(kc, vc, gfc)
        s_b = gla_final_state(jnp.flip(kc, 2), jnp.flip(vc, 2), jnp.flip(gbc, 2))

        (x_a, b_a, c_a, q, k, v, r, lr_f, lr_b, gate_a, gate_b) = in_proj(
            modulate(x, g_norm1[i], sh1, sc1), w_in[i], b_in[i])
        kl, vl, gfl, gbl = gla_kvg(k, v, lr_f, lr_b, w_a2_f[i], b_a_f[i], w_a2_b[i], b_a_b[i])
        ql = split_heads(q, GLA_DK) * GLA_DK ** -0.5
        o_lat = gla_bidir(ql, kl, vl, gfl, gbl, s_f, s_b)
        y_lat = mixer_merge(x_a, b_a, c_a, r, gate_a, gate_b, o_lat, conv_w[i], w_conv_out[i],
                            g_gla_norm[i], w_gla_out[i], conv3_grid)
        x = x + ga1 * (y_lat @ w_o[i])
        x = x + ga2 * sq_relu_mlp(modulate(x, g_norm2[i], sh2, sc2), w_up[i], w_down[i])

        if i < DEPTH - 1:
            qc = split_heads(cq, GLA_DK) * GLA_DK ** -0.5
            zeros = jnp.zeros_like(s_f)
            o_ctx = gla_bidir(qc, kc, vc, gfc, gbc, zeros, zeros)
            y_ctx = mixer_merge(cx_a, cb_a, cc_a, cr, cgate_a, cgate_b, o_ctx, conv_w[i], w_conv_out[i],
                                g_gla_norm[i], w_gla_out[i], conv3_seq)
            ctx_s = ctx_s + cmod[2] * (y_ctx @ w_o[i])
            ctx_s = ctx_s + cmod[5] * sq_relu_mlp(modulate(ctx_s, g_norm2[i], cmod[3], cmod[4]),
                                                  w_up[i], w_down[i])
    return rmsnorm(x, g_final)
```

```python
import functools

import jax
import jax.numpy as jnp
from jax import lax
from jax.experimental import pallas as pl
from jax.experimental.pallas import tpu as pltpu

F32 = jnp.float32
BF16 = jnp.bfloat16
HIGHEST = lax.Precision.HIGHEST

GLA_HEADS = 4
GLA_TAU = 16.0
GRID_W = 64
N_MOD = 6
RMS_EPS = 1e-6
GLA_RANK = 16

LR_PAD = 128
GLA_KERNEL_CHUNK = 64
INPROJ_TM = 256
OUT_TM = 256
COL_BLOCK = 512
FF_BLOCK = 1024
MOD_ROWS = 16
V7X_VMEM_LIMIT_BYTES = 60000 * 1024


def _vmem_limit(pipelined_bytes, resident_bytes, temp_bytes):
    need = 2 * pipelined_bytes + resident_bytes + temp_bytes
    return int(min(V7X_VMEM_LIMIT_BYTES, need))


def _nbytes(shape, dtype):
    n = 1
    for s in shape:
        n *= s
    return n * jnp.dtype(dtype).itemsize


def _dot(a, b):
    return jnp.dot(a, b, preferred_element_type=F32)


def _sigmoid(x):
    return 1.0 / (1.0 + jnp.exp(-x))


def _log_sigmoid(x):
    return jnp.minimum(x, 0.0) - jnp.log(1.0 + jnp.exp(-jnp.abs(x)))


def _rmsnorm(x, g):
    return x * lax.rsqrt(jnp.mean(x * x, axis=-1, keepdims=True) + RMS_EPS) * g


def _modulate(x, g, shift, scale):
    return _rmsnorm(x, g) * (1.0 + scale) + shift


def _ada_kernel(c_ref, w_ref, b_ref, o_ref):
    c = c_ref[...]
    s = c * _sigmoid(c)
    o_ref[...] = jnp.dot(s, w_ref[...], precision=HIGHEST, preferred_element_type=F32) + b_ref[...]


def _ada_call(cc, w_ada, b_ada):
    d = cc.shape[1]
    n_out = w_ada.shape[1]
    return pl.pallas_call(
        _ada_kernel,
        out_shape=jax.ShapeDtypeStruct((MOD_ROWS, n_out), F32),
        grid=(n_out // d,),
        in_specs=[
            pl.BlockSpec((MOD_ROWS, d), lambda j: (0, 0)),
            pl.BlockSpec((d, d), lambda j: (0, j)),
            pl.BlockSpec((1, d), lambda j: (0, j)),
        ],
        out_specs=pl.BlockSpec((MOD_ROWS, d), lambda j: (0, j)),
        compiler_params=pltpu.CompilerParams(
            dimension_semantics=("arbitrary",),
            vmem_limit_bytes=_vmem_limit(_nbytes((d, d), F32) + _nbytes((MOD_ROWS, 2 * d), F32),
                                         0, 4 * _nbytes((d, d), F32)),
        ),
        name="ada",
    )(cc, w_ada, b_ada)


def _log_decays(lr, wa2_ref, ba2_ref):
    xg = _dot(lr.astype(BF16), wa2_ref[...]) + ba2_ref[...]
    return _log_sigmoid(xg) * (1.0 / GLA_TAU)


def _inproj_kernel(x_ref, sh_ref, sc_ref, g1_ref, w1_ref, b1_ref, cw_ref, w2_ref, b2_ref, wa2_ref, ba2_ref,
                   ya_ref, q_ref, k_ref, v_ref, sr_ref, gf_ref, gb_ref, h_scr, *, q_scale):
    tm, d = x_ref.shape
    qk = q_ref.shape[1]
    h_scr[...] = _modulate(x_ref[...], g1_ref[...], sh_ref[...], sc_ref[...]).astype(BF16)

    col_in_row = lax.broadcasted_iota(jnp.int32, (tm, 1), 0) % GRID_W
    has_left = col_in_row != 0
    has_right = col_in_row != GRID_W - 1
    for j in range(0, d, COL_BLOCK):
        cs = slice(j, j + COL_BLOCK)
        h = h_scr[...]
        xa = _dot(h, w1_ref[:, j:j + COL_BLOCK]) + b1_ref[:, j:j + COL_BLOCK]
        ba = _dot(h, w1_ref[:, d + j:d + j + COL_BLOCK]) + b1_ref[:, d + j:d + j + COL_BLOCK]
        ca = _dot(h, w1_ref[:, 2 * d + j:2 * d + j + COL_BLOCK]) + b1_ref[:, 2 * d + j:2 * d + j + COL_BLOCK]
        u = ca * xa
        left = jnp.where(has_left, pltpu.roll(u, 1, 0), 0.0)
        right = jnp.where(has_right, pltpu.roll(u, tm - 1, 0), 0.0)
        y = left * cw_ref[0:1, cs] + u * cw_ref[1:2, cs] + right * cw_ref[2:3, cs]
        ya_ref[:, cs] = (ba * y).astype(BF16)

    h = h_scr[...]
    q_ref[...] = ((_dot(h, w2_ref[:, 0:qk]) + b2_ref[:, 0:qk]) * q_scale).astype(BF16)
    k_ref[...] = (_dot(h, w2_ref[:, qk:2 * qk]) + b2_ref[:, qk:2 * qk]).astype(BF16)
    for j in range(0, d, COL_BLOCK):
        c0 = 2 * qk + j
        v_ref[:, j:j + COL_BLOCK] = (_dot(h, w2_ref[:, c0:c0 + COL_BLOCK]) + b2_ref[:, c0:c0 + COL_BLOCK]).astype(BF16)
    for j in range(0, d, COL_BLOCK):
        c0 = 2 * qk + d + j
        r = _dot(h, w2_ref[:, c0:c0 + COL_BLOCK]) + b2_ref[:, c0:c0 + COL_BLOCK]
        sr_ref[:, j:j + COL_BLOCK] = (r * _sigmoid(r)).astype(BF16)
    c0 = 2 * qk + 2 * d
    lr = _dot(h, w2_ref[:, c0:c0 + LR_PAD]) + b2_ref[:, c0:c0 + LR_PAD]
    g = _log_decays(lr, wa2_ref, ba2_ref)
    gf_ref[...] = g[:, 0:qk]
    gb_ref[...] = g[:, qk:2 * qk]


def _inproj_ctx_kernel(x_ref, sh_ref, sc_ref, g1_ref, w_ref, b_ref, wa2_ref, ba2_ref,
                       k_ref, v_ref, gf_ref, gb_ref):
    d = x_ref.shape[1]
    qk = k_ref.shape[1]
    h = _modulate(x_ref[...], g1_ref[...], sh_ref[...], sc_ref[...]).astype(BF16)
    k_ref[...] = (_dot(h, w_ref[:, 0:qk]) + b_ref[:, 0:qk]).astype(BF16)
    for j in range(0, d, COL_BLOCK):
        c0 = qk + j
        v_ref[:, j:j + COL_BLOCK] = (_dot(h, w_ref[:, c0:c0 + COL_BLOCK]) + b_ref[:, c0:c0 + COL_BLOCK]).astype(BF16)
    c0 = qk + d
    lr = _dot(h, w_ref[:, c0:c0 + LR_PAD]) + b_ref[:, c0:c0 + LR_PAD]
    g = _log_decays(lr, wa2_ref, ba2_ref)
    gf_ref[...] = g[:, 0:qk]
    gb_ref[...] = g[:, qk:2 * qk]


def _resident():
    return pl.BlockSpec(memory_space=pltpu.VMEM)


def _inproj_call(x, mod3, g1, w1, b1, cw, w2, b2, wa2, ba2, qk, dk):
    bsz, t, d = x.shape
    tm = INPROJ_TM
    tok = lambda n: pl.BlockSpec((None, tm, n), lambda b, i: (b, i, 0))
    modrow = lambda col: pl.BlockSpec((None, 1, d), lambda b, i: (b, 0, col))
    out_shapes = (
        jax.ShapeDtypeStruct((bsz, t, d), BF16),
        jax.ShapeDtypeStruct((bsz, t, qk), BF16),
        jax.ShapeDtypeStruct((bsz, t, qk), BF16),
        jax.ShapeDtypeStruct((bsz, t, d), BF16),
        jax.ShapeDtypeStruct((bsz, t, d), BF16),
        jax.ShapeDtypeStruct((bsz, t, qk), F32),
        jax.ShapeDtypeStruct((bsz, t, qk), F32),
    )
    pipelined = (_nbytes((tm, d), F32) + 3 * _nbytes((tm, d), BF16) + 2 * _nbytes((tm, qk), BF16)
                 + 2 * _nbytes((tm, qk), F32) + 2 * _nbytes((1, d), F32))
    resident = sum(_nbytes(a.shape, a.dtype) for a in (g1, w1, b1, cw, w2, b2, wa2, ba2))
    temps = _nbytes((tm, d), BF16) + 12 * _nbytes((tm, COL_BLOCK), F32) + 3 * _nbytes((tm, 2 * qk), F32)
    return pl.pallas_call(
        functools.partial(_inproj_kernel, q_scale=dk ** -0.5),
        out_shape=out_shapes,
        grid=(bsz, t // tm),
        in_specs=[tok(d), modrow(0), modrow(1)] + [_resident()] * 8,
        out_specs=(tok(d), tok(qk), tok(qk), tok(d), tok(d), tok(qk), tok(qk)),
        scratch_shapes=[pltpu.VMEM((tm, d), BF16)],
        compiler_params=pltpu.CompilerParams(
            dimension_semantics=("parallel", "parallel"),
            vmem_limit_bytes=_vmem_limit(pipelined, resident, temps),
        ),
        name="inproj",
    )(x, mod3, mod3, g1, w1, b1, cw, w2, b2, wa2, ba2)


def _inproj_ctx_call(ctx, mod3, ctx_row, g1, w, b, wa2, ba2, qk):
    bsz, tc, d = ctx.shape
    tok = lambda n: pl.BlockSpec((None, tc, n), lambda b: (b, 0, 0))
    modrow = lambda col: pl.BlockSpec((None, 1, d), lambda b: (ctx_row, 0, col))
    out_shapes = (
        jax.ShapeDtypeStruct((bsz, tc, qk), BF16),
        jax.ShapeDtypeStruct((bsz, tc, d), BF16),
        jax.ShapeDtypeStruct((bsz, tc, qk), F32),
        jax.ShapeDtypeStruct((bsz, tc, qk), F32),
    )
    pipelined = (_nbytes((tc, d), F32) + _nbytes((tc, d), BF16) + _nbytes((tc, qk), BF16)
                 + 2 * _nbytes((tc, qk), F32) + 2 * _nbytes((1, d), F32))
    resident = sum(_nbytes(a.shape, a.dtype) for a in (g1, w, b, wa2, ba2))
    temps = _nbytes((tc, d), BF16) + 6 * _nbytes((tc, COL_BLOCK), F32) + 3 * _nbytes((tc, 2 * qk), F32)
    return pl.pallas_call(
        _inproj_ctx_kernel,
        out_shape=out_shapes,
        grid=(bsz,),
        in_specs=[tok(d), modrow(0), modrow(1)] + [_resident()] * 5,
        out_specs=(tok(qk), tok(d), tok(qk), tok(qk)),
        compiler_params=pltpu.CompilerParams(
            dimension_semantics=("parallel",),
            vmem_limit_bytes=_vmem_limit(pipelined, resident, temps),
        ),
        name="inproj_ctx",
    )(ctx, mod3, mod3, g1, w, b, wa2, ba2)


def _gla_chunk(q, k, v, g, st, tri, tri_mask, fwd, want_out):
    c = g.shape[0]
    b = jnp.dot(tri, g, precision=HIGHEST, preferred_element_type=F32)
    tot = b[c - 1:c, :] if fwd else b[0:1, :]
    kf = k.astype(F32)
    kt = (kf * jnp.exp(tot - b)).astype(BF16)
    st_new = st * jnp.exp(tot) + lax.dot_general(v, kt, (((0,), (0,)), ((), ())), preferred_element_type=F32)
    if not want_out:
        return st_new, None
    bm = b[c // 2 - 1:c // 2, :] if fwd else b[c // 2:c // 2 + 1, :]
    qf = q.astype(F32)
    qh = (qf * jnp.exp(b - bm)).astype(BF16)
    kh = (kf * jnp.exp(bm - b)).astype(BF16)
    a = lax.dot_general(qh, kh, (((1,), (1,)), ((), ())), preferred_element_type=F32)
    a = jnp.where(tri_mask, a, 0.0).astype(BF16)
    qb = (qf * jnp.exp(b)).astype(BF16)
    o = _dot(a, v) + lax.dot_general(qb, st.astype(BF16), (((1,), (1,)), ((), ())), preferred_element_type=F32)
    return st_new, o


def _gla_kernel(q_ref, k_ref, v_ref, sr_ref, gf_ref, gb_ref, kc_ref, vc_ref, gfc_ref, gbc_ref, gn_ref,
                out_ref, o_scr):
    c = GLA_KERNEL_CHUNK
    t, dk = q_ref.shape
    dv = v_ref.shape[1]
    n = t // c
    nc = kc_ref.shape[0] // c
    row = lax.broadcasted_iota(jnp.int32, (c, c), 0)
    col = lax.broadcasted_iota(jnp.int32, (c, c), 1)
    lower = row >= col
    upper = row <= col
    tril = lower.astype(F32)
    triu = upper.astype(F32)

    s_f = jnp.zeros((dv, dk), F32)
    for i in range(nc):
        sl = slice(i * c, (i + 1) * c)
        s_f, _ = _gla_chunk(None, kc_ref[sl, :], vc_ref[sl, :], gfc_ref[sl, :], s_f, tril, lower, True, False)
    s_b = jnp.zeros((dv, dk), F32)
    for i in reversed(range(nc)):
        sl = slice(i * c, (i + 1) * c)
        s_b, _ = _gla_chunk(None, kc_ref[sl, :], vc_ref[sl, :], gbc_ref[sl, :], s_b, triu, upper, False, False)

    def fwd_body(i, st):
        sl = pl.ds(pl.multiple_of(i * c, c), c)
        st, o = _gla_chunk(q_ref[sl, :], k_ref[sl, :], v_ref[sl, :], gf_ref[sl, :], st, tril, lower, True, True)
        o_scr[sl, :] = o
        return st

    lax.fori_loop(0, n, fwd_body, s_f)

    def bwd_body(j, st):
        sl = pl.ds(pl.multiple_of((n - 1 - j) * c, c), c)
        st, o = _gla_chunk(q_ref[sl, :], k_ref[sl, :], v_ref[sl, :], gb_ref[sl, :], st, triu, upper, False, True)
        o = o + o_scr[sl, :]
        y = _rmsnorm(o, gn_ref[...])
        out_ref[sl, :] = (y * sr_ref[sl, :].astype(F32)).astype(BF16)
        return st

    lax.fori_loop(0, n, bwd_body, s_b)


def _gla_call(q, k, v, sr, gf, gb, kc, vc, gfc, gbc, gn):
    bsz, t, qk = q.shape
    d = v.shape[2]
    tc = kc.shape[1]
    dk = qk // GLA_HEADS
    dv = d // GLA_HEADS
    seq = lambda rows, n: pl.BlockSpec((None, rows, n), lambda b, h: (b, 0, h))
    pipelined = (2 * _nbytes((t, dk), BF16) + 3 * _nbytes((t, dv), BF16) + 2 * _nbytes((t, dk), F32)
                 + _nbytes((tc, dk), BF16) + _nbytes((tc, dv), BF16) + 2 * _nbytes((tc, dk), F32)
                 + _nbytes((1, dv), F32))
    scratch = _nbytes((t, dv), F32)
    temps = 64 * _nbytes((GLA_KERNEL_CHUNK, dv), F32)
    return pl.pallas_call(
        _gla_kernel,
        out_shape=jax.ShapeDtypeStruct((bsz, t, d), BF16),
        grid=(bsz, GLA_HEADS),
        in_specs=[seq(t, dk), seq(t, dk), seq(t, dv), seq(t, dv), seq(t, dk), seq(t, dk),
                  seq(tc, dk), seq(tc, dv), seq(tc, dk), seq(tc, dk),
                  pl.BlockSpec((1, dv), lambda b, h: (0, h))],
        out_specs=seq(t, dv),
        scratch_shapes=[pltpu.VMEM((t, dv), F32)],
        compiler_params=pltpu.CompilerParams(
            dimension_semantics=("parallel", "parallel"),
            vmem_limit_bytes=_vmem_limit(pipelined, scratch, temps),
        ),
        name="gla",
    )(q, k, v, sr, gf, gb, kc, vc, gfc, gbc, gn)


def _out_kernel(x_ref, ya_ref, yb_ref, sh1_ref, sc1_ref, ga1_ref, sh2_ref, sc2_ref, ga2_ref,
                g1_ref, g2_ref, gfin_ref, wg_ref, bg_ref, wco_ref, wgo_ref, wo_ref, wup_ref, wdn_ref,
                o_ref):
    tm, d = x_ref.shape
    d_ff = wup_ref.shape[1]
    x = x_ref[...]
    h1 = _modulate(x, g1_ref[...], sh1_ref[...], sc1_ref[...]).astype(BF16)
    gate_a = _sigmoid(_dot(h1, wg_ref[:, 0:d]) + bg_ref[:, 0:d])
    y = gate_a * _dot(ya_ref[...], wco_ref[...])
    gate_b = _sigmoid(_dot(h1, wg_ref[:, d:2 * d]) + bg_ref[:, d:2 * d])
    y = y + gate_b * _dot(yb_ref[...], wgo_ref[...])
    x1 = x + ga1_ref[...] * _dot(y.astype(BF16), wo_ref[...])
    h2 = _modulate(x1, g2_ref[...], sh2_ref[...], sc2_ref[...]).astype(BF16)
    acc = jnp.zeros((tm, d), F32)
    for j in range(0, d_ff, FF_BLOCK):
        u = jnp.maximum(_dot(h2, wup_ref[:, j:j + FF_BLOCK]), 0.0)
        acc = acc + _dot((u * u).astype(BF16), wdn_ref[j:j + FF_BLOCK, :])
    x2 = x1 + ga2_ref[...] * acc
    o_ref[...] = _rmsnorm(x2, gfin_ref[...])


def _out_call(x, ya, yb, mod3, g1, g2, gfin, wg, bg, wco, wgo, wo, wup, wdn):
    bsz, t, d = x.shape
    tm = OUT_TM
    tok = lambda: pl.BlockSpec((None, tm, d), lambda b, i: (b, i, 0))
    modrow = lambda col: pl.BlockSpec((None, 1, d), lambda b, i: (b, 0, col))
    pipelined = 2 * _nbytes((tm, d), F32) + 2 * _nbytes((tm, d), BF16) + N_MOD * _nbytes((1, d), F32)
    resident = sum(_nbytes(a.shape, a.dtype) for a in (g1, g2, gfin, wg, bg, wco, wgo, wo, wup, wdn))
    temps = 10 * _nbytes((tm, d), F32)
    return pl.pallas_call(
        _out_kernel,
        out_shape=jax.ShapeDtypeStruct((bsz, t, d), F32),
        grid=(bsz, t // tm),
        in_specs=[tok(), tok(), tok()] + [modrow(cidx) for cidx in range(N_MOD)] + [_resident()] * 10,
        out_specs=tok(),
        compiler_params=pltpu.CompilerParams(
            dimension_semantics=("parallel", "parallel"),
            vmem_limit_bytes=_vmem_limit(pipelined, resident, temps),
        ),
        name="out",
    )(x, ya, yb, mod3, mod3, mod3, mod3, mod3, mod3, g1, g2, gfin, wg, bg, wco, wgo, wo, wup, wdn)


def kernel(x, c, ctx, c_ctx, w_ada, b_ada, g_norm1, w_in, b_in, conv_w, w_conv_out, w_a2_f, b_a_f,
           w_a2_b, b_a_b, g_gla_norm, w_gla_out, w_o, g_norm2, w_up, w_down, g_final):
    depth = w_ada.shape[0]
    assert depth == 1, "only the single-layer block is implemented"
    bsz, t, d = x.shape
    qk = w_a2_f.shape[2]
    dk = qk // GLA_HEADS
    rank = w_a2_f.shape[1]
    assert t % INPROJ_TM == 0 and t % OUT_TM == 0 and INPROJ_TM % GRID_W == 0
    assert t % GLA_KERNEL_CHUNK == 0 and ctx.shape[1] % GLA_KERNEL_CHUNK == 0
    assert bsz + 1 <= MOD_ROWS and 2 * rank <= LR_PAD

    cc = jnp.zeros((MOD_ROWS, d), F32).at[:bsz].set(c).at[bsz].set(c_ctx)
    mod = _ada_call(cc, w_ada[0], b_ada[0][None, :])
    mod3 = mod.reshape(MOD_ROWS, 1, N_MOD * d)

    w = w_in[0]
    bias = b_in[0][None, :]
    o_q = 3 * d
    o_k = o_q + qk
    o_v = o_k + qk
    o_r = o_v + d
    o_lr = o_r + d
    o_g = o_lr + 2 * rank
    lr_pad = LR_PAD - 2 * rank
    w1 = w[:, :o_q].astype(BF16)
    b1 = bias[:, :o_q]
    w_lr = jnp.pad(w[:, o_lr:o_g], ((0, 0), (0, lr_pad)))
    b_lr = jnp.pad(bias[:, o_lr:o_g], ((0, 0), (0, lr_pad)))
    w2 = jnp.concatenate([w[:, o_q:o_lr], w_lr], axis=1).astype(BF16)
    b2 = jnp.concatenate([bias[:, o_q:o_lr], b_lr], axis=1)
    w2c = jnp.concatenate([w[:, o_k:o_r], w_lr], axis=1).astype(BF16)
    b2c = jnp.concatenate([bias[:, o_k:o_r], b_lr], axis=1)
    wg = w[:, o_g:].astype(BF16)
    bg = bias[:, o_g:]
    wa2 = jnp.zeros((LR_PAD, 2 * qk), F32)
    wa2 = wa2.at[:rank, :qk].set(w_a2_f[0]).at[rank:2 * rank, qk:].set(w_a2_b[0]).astype(BF16)
    ba2 = jnp.concatenate([b_a_f[0], b_a_b[0]])[None, :]
    g1 = g_norm1[0][None, :]

    ya, q, k, v, sr, gf, gb = _inproj_call(x, mod3, g1, w1, b1, conv_w[0], w2, b2, wa2, ba2, qk, dk)
    kc, vc, gfc, gbc = _inproj_ctx_call(ctx, mod3, bsz, g1, w2c, b2c, wa2, ba2, qk)
    yb = _gla_call(q, k, v, sr, gf, gb, kc, vc, gfc, gbc, g_gla_norm[0][None, :])
    return _out_call(x, ya, yb, mod3, g1, g_norm2[0][None, :], g_final[None, :], wg, bg,
                     w_conv_out[0].astype(BF16), w_gla_out[0].astype(BF16), w_o[0].astype(BF16),
                     w_up[0].astype(BF16), w_down[0].astype(BF16))
```

```python
import functools

import jax
import jax.numpy as jnp
from jax import lax
from jax.experimental import pallas as pl
from jax.experimental.pallas import tpu as pltpu

F32 = jnp.float32
BF16 = jnp.bfloat16
HIGHEST = lax.Precision.HIGHEST

GLA_HEADS = 4
GLA_TAU = 16.0
GRID_W = 64
N_MOD = 6
RMS_EPS = 1e-6
GLA_RANK = 16

LR_PAD = 128
GLA_KERNEL_CHUNK = 64
GLA_GROUP = 4
GLA_SCAN_UNROLL = 4
INPROJ_TM = 256
OUT_TM = 256
COL_BLOCK = 512
FF_BLOCK = 1024
MOD_ROWS = 16
V7X_VMEM_LIMIT_BYTES = 60000 * 1024


def _vmem_limit(pipelined_bytes, resident_bytes, temp_bytes):
    need = 2 * pipelined_bytes + resident_bytes + temp_bytes
    return int(min(V7X_VMEM_LIMIT_BYTES, need))


def _nbytes(shape, dtype):
    n = 1
    for s in shape:
        n *= s
    return n * jnp.dtype(dtype).itemsize


def _dot(a, b):
    return jnp.dot(a, b, preferred_element_type=F32)


def _sigmoid(x):
    return 1.0 / (1.0 + jnp.exp(-x))


def _log_sigmoid(x):
    return jnp.minimum(x, 0.0) - jnp.log(1.0 + jnp.exp(-jnp.abs(x)))


def _rmsnorm(x, g):
    return x * lax.rsqrt(jnp.mean(x * x, axis=-1, keepdims=True) + RMS_EPS) * g


def _modulate(x, g, shift, scale):
    return _rmsnorm(x, g) * (1.0 + scale) + shift


def _ada_kernel(c_ref, w_ref, b_ref, o_ref):
    c = c_ref[...]
    s = c * _sigmoid(c)
    o_ref[...] = jnp.dot(s, w_ref[...], precision=HIGHEST, preferred_element_type=F32) + b_ref[...]


def _ada_call(cc, w_ada, b_ada):
    d = cc.shape[1]
    n_out = w_ada.shape[1]
    return pl.pallas_call(
        _ada_kernel,
        out_shape=jax.ShapeDtypeStruct((MOD_ROWS, n_out), F32),
        grid=(n_out // d,),
        in_specs=[
            pl.BlockSpec((MOD_ROWS, d), lambda j: (0, 0)),
            pl.BlockSpec((d, d), lambda j: (0, j)),
            pl.BlockSpec((1, d), lambda j: (0, j)),
        ],
        out_specs=pl.BlockSpec((MOD_ROWS, d), lambda j: (0, j)),
        compiler_params=pltpu.CompilerParams(
            dimension_semantics=("arbitrary",),
            vmem_limit_bytes=_vmem_limit(_nbytes((d, d), F32) + _nbytes((MOD_ROWS, 2 * d), F32),
                                         0, 4 * _nbytes((d, d), F32)),
        ),
        name="ada",
    )(cc, w_ada, b_ada)


def _log_decays(lr, wa2_ref, ba2_ref):
    xg = _dot(lr.astype(BF16), wa2_ref[...]) + ba2_ref[...]
    return _log_sigmoid(xg) * (1.0 / GLA_TAU)


def _inproj_kernel(x_ref, sh_ref, sc_ref, g1_ref, w1_ref, b1_ref, cw_ref, w2_ref, b2_ref, wa2_ref, ba2_ref,
                   ya_ref, q_ref, k_ref, v_ref, sr_ref, gf_ref, gb_ref, h_scr, *, q_scale):
    tm, d = x_ref.shape
    qk = q_ref.shape[1]
    h_scr[...] = _modulate(x_ref[...], g1_ref[...], sh_ref[...], sc_ref[...]).astype(BF16)

    col_in_row = lax.broadcasted_iota(jnp.int32, (tm, 1), 0) % GRID_W
    has_left = col_in_row != 0
    has_right = col_in_row != GRID_W - 1
    for j in range(0, d, COL_BLOCK):
        cs = slice(j, j + COL_BLOCK)
        h = h_scr[...]
        xa = _dot(h, w1_ref[:, j:j + COL_BLOCK]) + b1_ref[:, j:j + COL_BLOCK]
        ba = _dot(h, w1_ref[:, d + j:d + j + COL_BLOCK]) + b1_ref[:, d + j:d + j + COL_BLOCK]
        ca = _dot(h, w1_ref[:, 2 * d + j:2 * d + j + COL_BLOCK]) + b1_ref[:, 2 * d + j:2 * d + j + COL_BLOCK]
        u = ca * xa
        left = jnp.where(has_left, pltpu.roll(u, 1, 0), 0.0)
        right = jnp.where(has_right, pltpu.roll(u, tm - 1, 0), 0.0)
        y = left * cw_ref[0:1, cs] + u * cw_ref[1:2, cs] + right * cw_ref[2:3, cs]
        ya_ref[:, cs] = (ba * y).astype(BF16)

    h = h_scr[...]
    q_ref[...] = ((_dot(h, w2_ref[:, 0:qk]) + b2_ref[:, 0:qk]) * q_scale).astype(BF16)
    k_ref[...] = (_dot(h, w2_ref[:, qk:2 * qk]) + b2_ref[:, qk:2 * qk]).astype(BF16)
    for j in range(0, d, COL_BLOCK):
        c0 = 2 * qk + j
        v_ref[:, j:j + COL_BLOCK] = (_dot(h, w2_ref[:, c0:c0 + COL_BLOCK]) + b2_ref[:, c0:c0 + COL_BLOCK]).astype(BF16)
    for j in range(0, d, COL_BLOCK):
        c0 = 2 * qk + d + j
        r = _dot(h, w2_ref[:, c0:c0 + COL_BLOCK]) + b2_ref[:, c0:c0 + COL_BLOCK]
        sr_ref[:, j:j + COL_BLOCK] = (r * _sigmoid(r)).astype(BF16)
    c0 = 2 * qk + 2 * d
    lr = _dot(h, w2_ref[:, c0:c0 + LR_PAD]) + b2_ref[:, c0:c0 + LR_PAD]
    g = _log_decays(lr, wa2_ref, ba2_ref)
    gf_ref[...] = g[:, 0:qk]
    gb_ref[...] = g[:, qk:2 * qk]


def _inproj_ctx_kernel(x_ref, sh_ref, sc_ref, g1_ref, w_ref, b_ref, wa2_ref, ba2_ref,
                       k_ref, v_ref, gf_ref, gb_ref):
    d = x_ref.shape[1]
    qk = k_ref.shape[1]
    h = _modulate(x_ref[...], g1_ref[...], sh_ref[...], sc_ref[...]).astype(BF16)
    k_ref[...] = (_dot(h, w_ref[:, 0:qk]) + b_ref[:, 0:qk]).astype(BF16)
    for j in range(0, d, COL_BLOCK):
        c0 = qk + j
        v_ref[:, j:j + COL_BLOCK] = (_dot(h, w_ref[:, c0:c0 + COL_BLOCK]) + b_ref[:, c0:c0 + COL_BLOCK]).astype(BF16)
    c0 = qk + d
    lr = _dot(h, w_ref[:, c0:c0 + LR_PAD]) + b_ref[:, c0:c0 + LR_PAD]
    g = _log_decays(lr, wa2_ref, ba2_ref)
    gf_ref[...] = g[:, 0:qk]
    gb_ref[...] = g[:, qk:2 * qk]


def _resident():
    return pl.BlockSpec(memory_space=pltpu.VMEM)


def _inproj_call(x, mod3, g1, w1, b1, cw, w2, b2, wa2, ba2, qk, dk):
    bsz, t, d = x.shape
    tm = INPROJ_TM
    tok = lambda n: pl.BlockSpec((None, tm, n), lambda b, i: (b, i, 0))
    modrow = lambda col: pl.BlockSpec((None, 1, d), lambda b, i: (b, 0, col))
    out_shapes = (
        jax.ShapeDtypeStruct((bsz, t, d), BF16),
        jax.ShapeDtypeStruct((bsz, t, qk), BF16),
        jax.ShapeDtypeStruct((bsz, t, qk), BF16),
        jax.ShapeDtypeStruct((bsz, t, d), BF16),
        jax.ShapeDtypeStruct((bsz, t, d), BF16),
        jax.ShapeDtypeStruct((bsz, t, qk), F32),
        jax.ShapeDtypeStruct((bsz, t, qk), F32),
    )
    pipelined = (_nbytes((tm, d), F32) + 3 * _nbytes((tm, d), BF16) + 2 * _nbytes((tm, qk), BF16)
                 + 2 * _nbytes((tm, qk), F32) + 2 * _nbytes((1, d), F32))
    resident = sum(_nbytes(a.shape, a.dtype) for a in (g1, w1, b1, cw, w2, b2, wa2, ba2))
    temps = _nbytes((tm, d), BF16) + 12 * _nbytes((tm, COL_BLOCK), F32) + 3 * _nbytes((tm, 2 * qk), F32)
    return pl.pallas_call(
        functools.partial(_inproj_kernel, q_scale=dk ** -0.5),
        out_shape=out_shapes,
        grid=(bsz, t // tm),
        in_specs=[tok(d), modrow(0), modrow(1)] + [_resident()] * 8,
        out_specs=(tok(d), tok(qk), tok(qk), tok(d), tok(d), tok(qk), tok(qk)),
        scratch_shapes=[pltpu.VMEM((tm, d), BF16)],
        compiler_params=pltpu.CompilerParams(
            dimension_semantics=("parallel", "parallel"),
            vmem_limit_bytes=_vmem_limit(pipelined, resident, temps),
        ),
        name="inproj",
    )(x, mod3, mod3, g1, w1, b1, cw, w2, b2, wa2, ba2)


def _inproj_ctx_call(ctx, mod3, ctx_row, g1, w, b, wa2, ba2, qk):
    bsz, tc, d = ctx.shape
    tok = lambda n: pl.BlockSpec((None, tc, n), lambda b: (b, 0, 0))
    modrow = lambda col: pl.BlockSpec((None, 1, d), lambda b: (ctx_row, 0, col))
    out_shapes = (
        jax.ShapeDtypeStruct((bsz, tc, qk), BF16),
        jax.ShapeDtypeStruct((bsz, tc, d), BF16),
        jax.ShapeDtypeStruct((bsz, tc, qk), F32),
        jax.ShapeDtypeStruct((bsz, tc, qk), F32),
    )
    pipelined = (_nbytes((tc, d), F32) + _nbytes((tc, d), BF16) + _nbytes((tc, qk), BF16)
                 + 2 * _nbytes((tc, qk), F32) + 2 * _nbytes((1, d), F32))
    resident = sum(_nbytes(a.shape, a.dtype) for a in (g1, w, b, wa2, ba2))
    temps = _nbytes((tc, d), BF16) + 6 * _nbytes((tc, COL_BLOCK), F32) + 3 * _nbytes((tc, 2 * qk), F32)
    return pl.pallas_call(
        _inproj_ctx_kernel,
        out_shape=out_shapes,
        grid=(bsz,),
        in_specs=[tok(d), modrow(0), modrow(1)] + [_resident()] * 5,
        out_specs=(tok(qk), tok(d), tok(qk), tok(qk)),
        compiler_params=pltpu.CompilerParams(
            dimension_semantics=("parallel",),
            vmem_limit_bytes=_vmem_limit(pipelined, resident, temps),
        ),
        name="inproj_ctx",
    )(ctx, mod3, mod3, g1, w, b, wa2, ba2)


def _nt_dot(a, b):
    return lax.dot_general(a, b, (((1,), (1,)), ((), ())), preferred_element_type=F32)


def _tn_dot(a, b):
    return lax.dot_general(a, b, (((0,), (0,)), ((), ())), preferred_element_type=F32)


def _tile_decays(g, tri, fwd, n_chunks):
    rows, dk = g.shape
    c = rows // n_chunks
    g_hi = g.astype(BF16)
    g_lo = (g - g_hi.astype(F32)).astype(BF16)
    bb = _dot(tri, jnp.concatenate([g_hi, g_lo], axis=1))
    b = (bb[:, :dk] + bb[:, dk:]).reshape(n_chunks, c, dk)
    if fwd:
        return b, b[:, c - 1:c, :], b[:, c // 2 - 1:c // 2, :]
    return b, b[:, 0:1, :], b[:, c // 2:c // 2 + 1, :]


def _lane_broadcast_column(row, width):
    n = row.shape[1]
    col = jnp.broadcast_to(row, (n, n)).T
    return jnp.concatenate([col] * (width // n), axis=1)


def _gla_kernel(q_ref, k_ref, v_ref, sr_ref, gf_ref, gb_ref, kc_ref, vc_ref, gfc_ref, gbc_ref, gn_ref,
                out_ref, o_scr, ktf_scr, ktb_scr, qbf_scr, qbb_scr, df_scr, db_scr, sf_scr, sb_scr):
    c = GLA_KERNEL_CHUNK
    grp = GLA_GROUP
    tile = c * grp
    t, dk = q_ref.shape
    dv = v_ref.shape[1]
    n = t // c
    row = lax.broadcasted_iota(jnp.int32, (tile, tile), 0)
    col = lax.broadcasted_iota(jnp.int32, (tile, tile), 1)
    same_chunk = (row // c) == (col // c)
    lower = same_chunk & (row >= col)
    upper = same_chunk & (row <= col)
    tril = lower.astype(BF16)
    triu = upper.astype(BF16)
    dirs = ((True, gf_ref, gfc_ref, tril, lower, ktf_scr, qbf_scr, df_scr, sf_scr),
            (False, gb_ref, gbc_ref, triu, upper, ktb_scr, qbb_scr, db_scr, sb_scr))

    kc3 = kc_ref[...].astype(F32).reshape(grp, c, dk)
    for fwd, _, gc_ref, tri, _, _, _, _, s_scr in dirs:
        b, tot, _ = _tile_decays(gc_ref[...], tri, fwd, grp)
        kt = (kc3 * jnp.exp(tot - b)).astype(BF16)
        decay = jnp.exp(tot)
        s = jnp.zeros((dk, dv), F32)
        for j in (range(grp) if fwd else reversed(range(grp))):
            s = s * _lane_broadcast_column(decay[j], dv) + _tn_dot(kt[j], vc_ref[j * c:(j + 1) * c, :])
        s_scr[...] = s

    def local_body(i, carry):
        sl = pl.ds(pl.multiple_of(i * tile, tile), tile)
        q3 = q_ref[sl, :].astype(F32).reshape(grp, c, dk)
        k3 = k_ref[sl, :].astype(F32).reshape(grp, c, dk)
        scores = None
        for fwd, g_ref, _, tri, mask, kt_scr, qb_scr, d_scr, _ in dirs:
            b, tot, bm = _tile_decays(g_ref[sl, :], tri, fwd, grp)
            qh = (q3 * jnp.exp(b - bm)).reshape(tile, dk).astype(BF16)
            kh = (k3 * jnp.exp(bm - b)).reshape(tile, dk).astype(BF16)
            a = jnp.where(mask, _nt_dot(qh, kh), 0.0)
            scores = a if scores is None else scores + a
            kt_scr[sl, :] = (k3 * jnp.exp(tot - b)).reshape(tile, dk).astype(BF16)
            qb_scr[sl, :] = (q3 * jnp.exp(b)).reshape(tile, dk).astype(BF16)
            decay = jnp.exp(tot)
            for j in range(grp):
                d_scr[i * grp + j] = decay[j]
        o_scr[sl, :] = _dot(scores.astype(BF16), v_ref[sl, :])
        return carry

    lax.fori_loop(0, t // tile, local_body, 0)

    def scan_body(i, carry, finish):
        for fwd, _, _, _, _, kt_scr, qb_scr, d_scr, s_scr in dirs:
            j = i if fwd else n - 1 - i
            sl = pl.ds(pl.multiple_of(j * c, c), c)
            s = s_scr[...]
            o = _dot(qb_scr[sl, :], s.astype(BF16))
            s_scr[...] = s * _lane_broadcast_column(d_scr[j], dv) + _tn_dot(kt_scr[sl, :], v_ref[sl, :])
            if finish:
                y = _rmsnorm(o + o_scr[sl, :], gn_ref[...])
                out_ref[sl, :] = (y * sr_ref[sl, :].astype(F32)).astype(BF16)
            else:
                o_scr[sl, :] += o
        return carry

    lax.fori_loop(0, n // 2, functools.partial(scan_body, finish=False), 0, unroll=GLA_SCAN_UNROLL)
    lax.fori_loop(n // 2, n, functools.partial(scan_body, finish=True), 0, unroll=GLA_SCAN_UNROLL)


def _gla_call(q, k, v, sr, gf, gb, kc, vc, gfc, gbc, gn):
    bsz, t, qk = q.shape
    d = v.shape[2]
    tc = kc.shape[1]
    dk = qk // GLA_HEADS
    dv = d // GLA_HEADS
    n = t // GLA_KERNEL_CHUNK
    tile = GLA_KERNEL_CHUNK * GLA_GROUP
    seq = lambda rows, width: pl.BlockSpec((None, rows, width), lambda b, h: (b, 0, h))
    scratch_shapes = [
        pltpu.VMEM((t, dv), F32),
        pltpu.VMEM((t, dk), BF16), pltpu.VMEM((t, dk), BF16),
        pltpu.VMEM((t, dk), BF16), pltpu.VMEM((t, dk), BF16),
        pltpu.VMEM((n, 1, dk), F32), pltpu.VMEM((n, 1, dk), F32),
        pltpu.VMEM((dk, dv), F32), pltpu.VMEM((dk, dv), F32),
    ]
    pipelined = (2 * _nbytes((t, dk), BF16) + 3 * _nbytes((t, dv), BF16) + 2 * _nbytes((t, dk), F32)
                 + _nbytes((tc, dk), BF16) + _nbytes((tc, dv), BF16) + 2 * _nbytes((tc, dk), F32)
                 + _nbytes((1, dv), F32))
    scratch = (_nbytes((t, dv), F32) + 4 * _nbytes((t, dk), BF16) + 2 * _nbytes((n, 8, dk), F32)
               + 2 * _nbytes((dk, dv), F32))
    temps = 24 * _nbytes((tile, dv), F32)
    return pl.pallas_call(
        _gla_kernel,
        out_shape=jax.ShapeDtypeStruct((bsz, t, d), BF16),
        grid=(bsz, GLA_HEADS),
        in_specs=[seq(t, dk), seq(t, dk), seq(t, dv), seq(t, dv), seq(t, dk), seq(t, dk),
                  seq(tc, dk), seq(tc, dv), seq(tc, dk), seq(tc, dk),
                  pl.BlockSpec((1, dv), lambda b, h: (0, h))],
        out_specs=seq(t, dv),
        scratch_shapes=scratch_shapes,
        compiler_params=pltpu.CompilerParams(
            dimension_semantics=("parallel", "parallel"),
            vmem_limit_bytes=_vmem_limit(pipelined, scratch, temps),
        ),
        name="gla",
    )(q, k, v, sr, gf, gb, kc, vc, gfc, gbc, gn)


def _out_kernel(x_ref, ya_ref, yb_ref, sh1_ref, sc1_ref, ga1_ref, sh2_ref, sc2_ref, ga2_ref,
                g1_ref, g2_ref, gfin_ref, wg_ref, bg_ref, wco_ref, wgo_ref, wo_ref, wup_ref, wdn_ref,
                o_ref):
    tm, d = x_ref.shape
    d_ff = wup_ref.shape[1]
    x = x_ref[...]
    h1 = _modulate(x, g1_ref[...], sh1_ref[...], sc1_ref[...]).astype(BF16)
    gate_a = _sigmoid(_dot(h1, wg_ref[:, 0:d]) + bg_ref[:, 0:d])
    y = gate_a * _dot(ya_ref[...], wco_ref[...])
    gate_b = _sigmoid(_dot(h1, wg_ref[:, d:2 * d]) + bg_ref[:, d:2 * d])
    y = y + gate_b * _dot(yb_ref[...], wgo_ref[...])
    x1 = x + ga1_ref[...] * _dot(y.astype(BF16), wo_ref[...])
    h2 = _modulate(x1, g2_ref[...], sh2_ref[...], sc2_ref[...]).astype(BF16)
    acc = jnp.zeros((tm, d), F32)
    for j in range(0, d_ff, FF_BLOCK):
        u = jnp.maximum(_dot(h2, wup_ref[:, j:j + FF_BLOCK]), 0.0)
        acc = acc + _dot((u * u).astype(BF16), wdn_ref[j:j + FF_BLOCK, :])
    x2 = x1 + ga2_ref[...] * acc
    o_ref[...] = _rmsnorm(x2, gfin_ref[...])


def _out_call(x, ya, yb, mod3, g1, g2, gfin, wg, bg, wco, wgo, wo, wup, wdn):
    bsz, t, d = x.shape
    tm = OUT_TM
    tok = lambda: pl.BlockSpec((None, tm, d), lambda b, i: (b, i, 0))
    modrow = lambda col: pl.BlockSpec((None, 1, d), lambda b, i: (b, 0, col))
    pipelined = 2 * _nbytes((tm, d), F32) + 2 * _nbytes((tm, d), BF16) + N_MOD * _nbytes((1, d), F32)
    resident = sum(_nbytes(a.shape, a.dtype) for a in (g1, g2, gfin, wg, bg, wco, wgo, wo, wup, wdn))
    temps = 10 * _nbytes((tm, d), F32)
    return pl.pallas_call(
        _out_kernel,
        out_shape=jax.ShapeDtypeStruct((bsz, t, d), F32),
        grid=(bsz, t // tm),
        in_specs=[tok(), tok(), tok()] + [modrow(cidx) for cidx in range(N_MOD)] + [_resident()] * 10,
        out_specs=tok(),
        compiler_params=pltpu.CompilerParams(
            dimension_semantics=("parallel", "parallel"),
            vmem_limit_bytes=_vmem_limit(pipelined, resident, temps),
        ),
        name="out",
    )(x, ya, yb, mod3, mod3, mod3, mod3, mod3, mod3, g1, g2, gfin, wg, bg, wco, wgo, wo, wup, wdn)


def kernel(x, c, ctx, c_ctx, w_ada, b_ada, g_norm1, w_in, b_in, conv_w, w_conv_out, w_a2_f, b_a_f,
           w_a2_b, b_a_b, g_gla_norm, w_gla_out, w_o, g_norm2, w_up, w_down, g_final):
    depth = w_ada.shape[0]
    assert depth == 1, "only the single-layer block is implemented"
    bsz, t, d = x.shape
    qk = w_a2_f.shape[2]
    dk = qk // GLA_HEADS
    rank = w_a2_f.shape[1]
    assert t % INPROJ_TM == 0 and t % OUT_TM == 0 and INPROJ_TM % GRID_W == 0
    assert t % (2 * GLA_GROUP * GLA_KERNEL_CHUNK) == 0 and ctx.shape[1] == GLA_GROUP * GLA_KERNEL_CHUNK
    assert bsz + 1 <= MOD_ROWS and 2 * rank <= LR_PAD

    cc = jnp.zeros((MOD_ROWS, d), F32).at[:bsz].set(c).at[bsz].set(c_ctx)
    mod = _ada_call(cc, w_ada[0], b_ada[0][None, :])
    mod3 = mod.reshape(MOD_ROWS, 1, N_MOD * d)

    w = w_in[0]
    bias = b_in[0][None, :]
    o_q = 3 * d
    o_k = o_q + qk
    o_v = o_k + qk
    o_r = o_v + d
    o_lr = o_r + d
    o_g = o_lr + 2 * rank
    lr_pad = LR_PAD - 2 * rank
    w1 = w[:, :o_q].astype(BF16)
    b1 = bias[:, :o_q]
    w_lr = jnp.pad(w[:, o_lr:o_g], ((0, 0), (0, lr_pad)))
    b_lr = jnp.pad(bias[:, o_lr:o_g], ((0, 0), (0, lr_pad)))
    w2 = jnp.concatenate([w[:, o_q:o_lr], w_lr], axis=1).astype(BF16)
    b2 = jnp.concatenate([bias[:, o_q:o_lr], b_lr], axis=1)
    w2c = jnp.concatenate([w[:, o_k:o_r], w_lr], axis=1).astype(BF16)
    b2c = jnp.concatenate([bias[:, o_k:o_r], b_lr], axis=1)
    wg = w[:, o_g:].astype(BF16)
    bg = bias[:, o_g:]
    wa2 = jnp.zeros((LR_PAD, 2 * qk), F32)
    wa2 = wa2.at[:rank, :qk].set(w_a2_f[0]).at[rank:2 * rank, qk:].set(w_a2_b[0]).astype(BF16)
    ba2 = jnp.concatenate([b_a_f[0], b_a_b[0]])[None, :]
    g1 = g_norm1[0][None, :]

    ya, q, k, v, sr, gf, gb = _inproj_call(x, mod3, g1, w1, b1, conv_w[0], w2, b2, wa2, ba2, qk, dk)
    kc, vc, gfc, gbc = _inproj_ctx_call(ctx, mod3, bsz, g1, w2c, b2c, wa2, ba2, qk)
    yb = _gla_call(q, k, v, sr, gf, gb, kc, vc, gfc, gbc, g_gla_norm[0][None, :])
    return _out_call(x, ya, yb, mod3, g1, g_norm2[0][None, :], g_final[None, :], wg, bg,
                     w_conv_out[0].astype(BF16), w_gla_out[0].astype(BF16), w_o[0].astype(BF16),
                     w_up[0].astype(BF16), w_down[0].astype(BF16))
```

```python
import functools

import jax
import jax.numpy as jnp
from jax import lax
from jax.experimental import pallas as pl
from jax.experimental.pallas import tpu as pltpu

F32 = jnp.float32
BF16 = jnp.bfloat16
HIGHEST = lax.Precision.HIGHEST

GLA_HEADS = 4
GLA_TAU = 16.0
GRID_W = 64
N_MOD = 6
RMS_EPS = 1e-6
GLA_RANK = 16
LOG2_E = 1.4426950408889634

LR_PAD = 128
GLA_KERNEL_CHUNK = 64
GLA_PAIR = 2
GLA_GROUP = 4
GLA_LOCAL_UNROLL = 2
GLA_SCAN_UNROLL = 2
INPROJ_TM = 256
OUT_TM = 256
COL_BLOCK = 512
FF_BLOCK = 1024
MOD_ROWS = 16
V7X_VMEM_LIMIT_BYTES = 60000 * 1024


def _vmem_limit(pipelined_bytes, resident_bytes, temp_bytes):
    need = 2 * pipelined_bytes + resident_bytes + temp_bytes
    return int(min(V7X_VMEM_LIMIT_BYTES, need))


def _nbytes(shape, dtype):
    n = 1
    for s in shape:
        n *= s
    return n * jnp.dtype(dtype).itemsize


def _dot(a, b):
    return jnp.dot(a, b, preferred_element_type=F32)


def _nt_dot(a, b):
    return lax.dot_general(a, b, (((1,), (1,)), ((), ())), preferred_element_type=F32)


def _sigmoid(x):
    return 1.0 / (1.0 + jnp.exp(-x))


def _log_sigmoid(x):
    return jnp.minimum(x, 0.0) - jnp.log(1.0 + jnp.exp(-jnp.abs(x)))


def _rmsnorm(x, g):
    return x * lax.rsqrt(jnp.mean(x * x, axis=-1, keepdims=True) + RMS_EPS) * g


def _modulate(x, g, shift, scale):
    return _rmsnorm(x, g) * (1.0 + scale) + shift


def _ada_kernel(c_ref, w_ref, b_ref, o_ref):
    c = c_ref[...]
    s = c * _sigmoid(c)
    o_ref[...] = jnp.dot(s, w_ref[...], precision=HIGHEST, preferred_element_type=F32) + b_ref[...]


def _ada_call(cc, w_ada, b_ada):
    d = cc.shape[1]
    n_out = w_ada.shape[1]
    return pl.pallas_call(
        _ada_kernel,
        out_shape=jax.ShapeDtypeStruct((MOD_ROWS, n_out), F32),
        grid=(n_out // d,),
        in_specs=[
            pl.BlockSpec((MOD_ROWS, d), lambda j: (0, 0)),
            pl.BlockSpec((d, d), lambda j: (0, j)),
            pl.BlockSpec((1, d), lambda j: (0, j)),
        ],
        out_specs=pl.BlockSpec((MOD_ROWS, d), lambda j: (0, j)),
        compiler_params=pltpu.CompilerParams(
            dimension_semantics=("arbitrary",),
            vmem_limit_bytes=_vmem_limit(_nbytes((d, d), F32) + _nbytes((MOD_ROWS, 2 * d), F32),
                                         0, 4 * _nbytes((d, d), F32)),
        ),
        name="ada",
    )(cc, w_ada, b_ada)


def _log_decays(lr, wa2_ref, ba2_ref):
    xg = _dot(lr.astype(BF16), wa2_ref[...]) + ba2_ref[...]
    return _log_sigmoid(xg) * (LOG2_E / GLA_TAU)


def _inproj_kernel(x_ref, sh_ref, sc_ref, g1_ref, w1_ref, b1_ref, cw_ref, w2_ref, b2_ref, wa2_ref, ba2_ref,
                   ya_ref, q_ref, k_ref, v_ref, sr_ref, gf_ref, gb_ref, h_scr, *, q_scale):
    tm, d = x_ref.shape
    qk = q_ref.shape[1]
    h_scr[...] = _modulate(x_ref[...], g1_ref[...], sh_ref[...], sc_ref[...]).astype(BF16)

    col_in_row = lax.broadcasted_iota(jnp.int32, (tm, 1), 0) % GRID_W
    has_left = col_in_row != 0
    has_right = col_in_row != GRID_W - 1
    for j in range(0, d, COL_BLOCK):
        cs = slice(j, j + COL_BLOCK)
        h = h_scr[...]
        xa = _dot(h, w1_ref[:, j:j + COL_BLOCK]) + b1_ref[:, j:j + COL_BLOCK]
        ba = _dot(h, w1_ref[:, d + j:d + j + COL_BLOCK]) + b1_ref[:, d + j:d + j + COL_BLOCK]
        ca = _dot(h, w1_ref[:, 2 * d + j:2 * d + j + COL_BLOCK]) + b1_ref[:, 2 * d + j:2 * d + j + COL_BLOCK]
        u = ca * xa
        left = jnp.where(has_left, pltpu.roll(u, 1, 0), 0.0)
        right = jnp.where(has_right, pltpu.roll(u, tm - 1, 0), 0.0)
        y = left * cw_ref[0:1, cs] + u * cw_ref[1:2, cs] + right * cw_ref[2:3, cs]
        ya_ref[:, cs] = (ba * y).astype(BF16)

    h = h_scr[...]
    q_ref[...] = ((_dot(h, w2_ref[:, 0:qk]) + b2_ref[:, 0:qk]) * q_scale).astype(BF16)
    k_ref[...] = (_dot(h, w2_ref[:, qk:2 * qk]) + b2_ref[:, qk:2 * qk]).astype(BF16)
    for j in range(0, d, COL_BLOCK):
        c0 = 2 * qk + j
        v_ref[:, j:j + COL_BLOCK] = (_dot(h, w2_ref[:, c0:c0 + COL_BLOCK]) + b2_ref[:, c0:c0 + COL_BLOCK]).astype(BF16)
    for j in range(0, d, COL_BLOCK):
        c0 = 2 * qk + d + j
        r = _dot(h, w2_ref[:, c0:c0 + COL_BLOCK]) + b2_ref[:, c0:c0 + COL_BLOCK]
        sr_ref[:, j:j + COL_BLOCK] = (r * _sigmoid(r)).astype(BF16)
    c0 = 2 * qk + 2 * d
    lr = _dot(h, w2_ref[:, c0:c0 + LR_PAD]) + b2_ref[:, c0:c0 + LR_PAD]
    g = _log_decays(lr, wa2_ref, ba2_ref)
    gf_ref[...] = g[:, 0:qk]
    gb_ref[...] = g[:, qk:2 * qk]


def _inproj_ctx_kernel(x_ref, sh_ref, sc_ref, g1_ref, w_ref, b_ref, wa2_ref, ba2_ref,
                       k_ref, v_ref, gf_ref, gb_ref):
    d = x_ref.shape[1]
    qk = k_ref.shape[1]
    h = _modulate(x_ref[...], g1_ref[...], sh_ref[...], sc_ref[...]).astype(BF16)
    k_ref[...] = (_dot(h, w_ref[:, 0:qk]) + b_ref[:, 0:qk]).astype(BF16)
    for j in range(0, d, COL_BLOCK):
        c0 = qk + j
        v_ref[:, j:j + COL_BLOCK] = (_dot(h, w_ref[:, c0:c0 + COL_BLOCK]) + b_ref[:, c0:c0 + COL_BLOCK]).astype(BF16)
    c0 = qk + d
    lr = _dot(h, w_ref[:, c0:c0 + LR_PAD]) + b_ref[:, c0:c0 + LR_PAD]
    g = _log_decays(lr, wa2_ref, ba2_ref)
    gf_ref[...] = g[:, 0:qk]
    gb_ref[...] = g[:, qk:2 * qk]


def _resident():
    return pl.BlockSpec(memory_space=pltpu.VMEM)


def _inproj_call(x, mod3, g1, w1, b1, cw, w2, b2, wa2, ba2, qk, dk):
    bsz, t, d = x.shape
    tm = INPROJ_TM
    tok = lambda n: pl.BlockSpec((None, tm, n), lambda b, i: (b, i, 0))
    modrow = lambda col: pl.BlockSpec((None, 1, d), lambda b, i: (b, 0, col))
    out_shapes = (
        jax.ShapeDtypeStruct((bsz, t, d), BF16),
        jax.ShapeDtypeStruct((bsz, t, qk), BF16),
        jax.ShapeDtypeStruct((bsz, t, qk), BF16),
        jax.ShapeDtypeStruct((bsz, t, d), BF16),
        jax.ShapeDtypeStruct((bsz, t, d), BF16),
        jax.ShapeDtypeStruct((bsz, t, qk), F32),
        jax.ShapeDtypeStruct((bsz, t, qk), F32),
    )
    pipelined = (_nbytes((tm, d), F32) + 3 * _nbytes((tm, d), BF16) + 2 * _nbytes((tm, qk), BF16)
                 + 2 * _nbytes((tm, qk), F32) + 2 * _nbytes((1, d), F32))
    resident = sum(_nbytes(a.shape, a.dtype) for a in (g1, w1, b1, cw, w2, b2, wa2, ba2))
    temps = _nbytes((tm, d), BF16) + 12 * _nbytes((tm, COL_BLOCK), F32) + 3 * _nbytes((tm, 2 * qk), F32)
    return pl.pallas_call(
        functools.partial(_inproj_kernel, q_scale=dk ** -0.5),
        out_shape=out_shapes,
        grid=(bsz, t // tm),
        in_specs=[tok(d), modrow(0), modrow(1)] + [_resident()] * 8,
        out_specs=(tok(d), tok(qk), tok(qk), tok(d), tok(d), tok(qk), tok(qk)),
        scratch_shapes=[pltpu.VMEM((tm, d), BF16)],
        compiler_params=pltpu.CompilerParams(
            dimension_semantics=("parallel", "parallel"),
            vmem_limit_bytes=_vmem_limit(pipelined, resident, temps),
        ),
        name="inproj",
    )(x, mod3, mod3, g1, w1, b1, cw, w2, b2, wa2, ba2)


def _inproj_ctx_call(ctx, mod3, ctx_row, g1, w, b, wa2, ba2, qk):
    bsz, tc, d = ctx.shape
    tok = lambda n: pl.BlockSpec((None, tc, n), lambda b: (b, 0, 0))
    modrow = lambda col: pl.BlockSpec((None, 1, d), lambda b: (ctx_row, 0, col))
    out_shapes = (
        jax.ShapeDtypeStruct((bsz, tc, qk), BF16),
        jax.ShapeDtypeStruct((bsz, tc, d), BF16),
        jax.ShapeDtypeStruct((bsz, tc, qk), F32),
        jax.ShapeDtypeStruct((bsz, tc, qk), F32),
    )
    pipelined = (_nbytes((tc, d), F32) + _nbytes((tc, d), BF16) + _nbytes((tc, qk), BF16)
                 + 2 * _nbytes((tc, qk), F32) + 2 * _nbytes((1, d), F32))
    resident = sum(_nbytes(a.shape, a.dtype) for a in (g1, w, b, wa2, ba2))
    temps = _nbytes((tc, d), BF16) + 6 * _nbytes((tc, COL_BLOCK), F32) + 3 * _nbytes((tc, 2 * qk), F32)
    return pl.pallas_call(
        _inproj_ctx_kernel,
        out_shape=out_shapes,
        grid=(bsz,),
        in_specs=[tok(d), modrow(0), modrow(1)] + [_resident()] * 5,
        out_specs=(tok(qk), tok(d), tok(qk), tok(qk)),
        compiler_params=pltpu.CompilerParams(
            dimension_semantics=("parallel",),
            vmem_limit_bytes=_vmem_limit(pipelined, resident, temps),
        ),
        name="inproj_ctx",
    )(ctx, mod3, mod3, g1, w, b, wa2, ba2)


def _tile_decays(g, tri, fwd, n_chunks):
    rows, dk = g.shape
    c = rows // n_chunks
    g_hi = g.astype(BF16)
    g_lo = (g - g_hi.astype(F32)).astype(BF16)
    bb = _dot(tri, jnp.concatenate([g_hi, g_lo], axis=1))
    b = (bb[:, :dk] + bb[:, dk:]).reshape(n_chunks, c, dk)
    if fwd:
        return b, b[:, c - 1:c, :], b[:, c // 2 - 1:c // 2, :]
    return b, b[:, 0:1, :], b[:, c // 2:c // 2 + 1, :]


def _pair_offsets(tot, fwd):
    n_chunks = tot.shape[0]
    zero = jnp.zeros_like(tot[0:1])
    on_odd = jnp.concatenate([t for p in range(0, n_chunks, GLA_PAIR) for t in (zero, tot[p:p + 1])], axis=0)
    on_even = jnp.concatenate([t for p in range(0, n_chunks, GLA_PAIR) for t in (tot[p + 1:p + 2], zero)], axis=0)
    pair_tot = jnp.concatenate([tot[p:p + 1] + tot[p + 1:p + 2] for p in range(0, n_chunks, GLA_PAIR)], axis=0)
    return (on_odd, on_even, pair_tot) if fwd else (on_even, on_odd, pair_tot)


def _lane_broadcast_column(row):
    n = row.shape[1]
    return jnp.broadcast_to(row, (n, n)).T


def _gla_kernel(q_ref, k_ref, v_ref, sr_ref, gf_ref, gb_ref, kc_ref, vc_ref, gfc_ref, gbc_ref, gn_ref,
                out_ref, a_scr, o_scr, ktf_scr, ktb_scr, qbf_scr, qbb_scr, dmf_scr, dmb_scr, sf_scr, sb_scr):
    c = GLA_KERNEL_CHUNK
    grp = GLA_GROUP
    tile = c * grp
    pair = c * GLA_PAIR
    pairs_per_tile = grp // GLA_PAIR
    t, dk = q_ref.shape
    dv = v_ref.shape[1]
    n_pairs = t // pair
    row = lax.broadcasted_iota(jnp.int32, (tile, tile), 0)
    col = lax.broadcasted_iota(jnp.int32, (tile, tile), 1)
    row_chunk = row // c
    col_chunk = col // c
    same_pair = (row // pair) == (col // pair)
    inside_f = (row_chunk == col_chunk) & (row >= col)
    inside_b = (row_chunk == col_chunk) & (row <= col)
    across_f = same_pair & (row_chunk == col_chunk + 1)
    across_b = same_pair & (row_chunk == col_chunk - 1)
    dirs = ((True, gf_ref, gfc_ref, inside_f, across_f, ktf_scr, qbf_scr, dmf_scr, sf_scr),
            (False, gb_ref, gbc_ref, inside_b, across_b, ktb_scr, qbb_scr, dmb_scr, sb_scr))

    def key_side(k3, b, tot, k_off):
        kt = k3 * jnp.exp2(tot - b)
        return kt, kt * jnp.exp2(k_off)

    ctx_tiles = kc_ref.shape[0] // tile
    for fwd, _, gc_ref, inside, _, _, _, _, s_scr in dirs:
        s = jnp.zeros((dk, dv), F32)
        for ti in (range(ctx_tiles) if fwd else reversed(range(ctx_tiles))):
            rows = slice(ti * tile, (ti + 1) * tile)
            b, tot, _ = _tile_decays(gc_ref[rows, :], inside.astype(BF16), fwd, grp)
            _, k_off, pair_tot = _pair_offsets(tot, fwd)
            _, kt_pair = key_side(kc_ref[rows, :].astype(F32).reshape(grp, c, dk), b, tot, k_off)
            kt_t = kt_pair.reshape(tile, dk).T.astype(BF16)
            decay = jnp.exp2(pair_tot)
            for p in (range(pairs_per_tile) if fwd else reversed(range(pairs_per_tile))):
                dm = _lane_broadcast_column(decay[p])
                s = (s * jnp.concatenate([dm] * (dv // dk), axis=1)
                     + _dot(kt_t[:, p * pair:(p + 1) * pair],
                            vc_ref[ti * tile + p * pair:ti * tile + (p + 1) * pair, :]))
        s_scr[...] = s

    def local_body(i, carry):
        tiles = [i * GLA_LOCAL_UNROLL + u for u in range(GLA_LOCAL_UNROLL)]
        rows = [pl.ds(pl.multiple_of(ti * tile, tile), tile) for ti in tiles]
        items = [(u, d) for u in range(GLA_LOCAL_UNROLL) for d in dirs]
        cums = [_tile_decays(d[1][rows[u], :], d[3].astype(BF16), d[0], grp) for u, d in items]
        q3 = [q_ref[sl, :].astype(F32).reshape(grp, c, dk) for sl in rows]
        k3 = [k_ref[sl, :].astype(F32).reshape(grp, c, dk) for sl in rows]
        products = []
        for (u, d), (b, tot, bm) in zip(items, cums):
            fwd, _, _, _, _, kt_scr, qb_scr, dm_scr, _ = d
            q_off, k_off, pair_tot = _pair_offsets(tot, fwd)
            qh = (q3[u] * jnp.exp2(b - bm)).reshape(tile, dk).astype(BF16)
            kh = (k3[u] * jnp.exp2(bm - b)).reshape(tile, dk).astype(BF16)
            qb = q3[u] * jnp.exp2(b)
            kt, kt_pair = key_side(k3[u], b, tot, k_off)
            products.append((_nt_dot(qh, kh),
                             _nt_dot(qb.reshape(tile, dk).astype(BF16), kt.reshape(tile, dk).astype(BF16))))
            qb_scr[rows[u], :] = (qb * jnp.exp2(q_off)).reshape(tile, dk).astype(BF16)
            kt_t = kt_pair.reshape(tile, dk).T.astype(BF16)
            decay = jnp.exp2(pair_tot)
            for p in range(pairs_per_tile):
                kt_scr[tiles[u] * pairs_per_tile + p] = kt_t[:, p * pair:(p + 1) * pair]
                dm_scr[tiles[u] * pairs_per_tile + p] = _lane_broadcast_column(decay[p])
        for u in range(GLA_LOCAL_UNROLL):
            scores = jnp.zeros((tile, tile), F32)
            for (iu, d), (inner, outer) in zip(items, products):
                if iu == u:
                    scores = scores + jnp.where(d[3], inner, 0.0) + jnp.where(d[4], outer, 0.0)
            scores = scores.astype(BF16)
            for p in range(pairs_per_tile):
                a_scr[pl.ds(pl.multiple_of(tiles[u] * tile + p * pair, pair), pair), :] = (
                    scores[p * pair:(p + 1) * pair, p * pair:(p + 1) * pair])
        return carry

    lax.fori_loop(0, t // (tile * GLA_LOCAL_UNROLL), local_body, 0)

    def scan_body(i, carry, finish):
        for fwd, _, _, _, _, kt_scr, qb_scr, dm_scr, s_scr in dirs:
            j = i if fwd else n_pairs - 1 - i
            sl = pl.ds(pl.multiple_of(j * pair, pair), pair)
            s = s_scr[...]
            v = v_ref[sl, :]
            o = _dot(qb_scr[sl, :], s.astype(BF16))
            s_scr[...] = s * jnp.concatenate([dm_scr[j]] * (dv // dk), axis=1) + _dot(kt_scr[j], v)
            if finish:
                y = _rmsnorm(o + o_scr[sl, :], gn_ref[...])
                out_ref[sl, :] = (y * sr_ref[sl, :].astype(F32)).astype(BF16)
            else:
                o_scr[sl, :] = o + _dot(a_scr[sl, :], v)
        return carry

    lax.fori_loop(0, n_pairs // 2, functools.partial(scan_body, finish=False), 0, unroll=GLA_SCAN_UNROLL)
    lax.fori_loop(n_pairs // 2, n_pairs, functools.partial(scan_body, finish=True), 0, unroll=GLA_SCAN_UNROLL)


def _gla_call(q, k, v, sr, gf, gb, kc, vc, gfc, gbc, gn):
    bsz, t, qk = q.shape
    d = v.shape[2]
    tc = kc.shape[1]
    dk = qk // GLA_HEADS
    dv = d // GLA_HEADS
    pair = GLA_KERNEL_CHUNK * GLA_PAIR
    n_pairs = t // pair
    tile = GLA_KERNEL_CHUNK * GLA_GROUP
    seq = lambda rows, width: pl.BlockSpec((None, rows, width), lambda b, h: (b, 0, h))
    scratch_shapes = [
        pltpu.VMEM((t, pair), BF16),
        pltpu.VMEM((t, dv), F32),
        pltpu.VMEM((n_pairs, dk, pair), BF16), pltpu.VMEM((n_pairs, dk, pair), BF16),
        pltpu.VMEM((t, dk), BF16), pltpu.VMEM((t, dk), BF16),
        pltpu.VMEM((n_pairs, dk, dk), F32), pltpu.VMEM((n_pairs, dk, dk), F32),
        pltpu.VMEM((dk, dv), F32), pltpu.VMEM((dk, dv), F32),
    ]
    pipelined = (2 * _nbytes((t, dk), BF16) + 3 * _nbytes((t, dv), BF16) + 2 * _nbytes((t, dk), F32)
                 + _nbytes((tc, dk), BF16) + _nbytes((tc, dv), BF16) + 2 * _nbytes((tc, dk), F32)
                 + _nbytes((1, dv), F32))
    scratch = (_nbytes((t, pair), BF16) + _nbytes((t, dv), F32) + 2 * _nbytes((n_pairs, dk, pair), BF16)
               + 2 * _nbytes((t, dk), BF16) + 2 * _nbytes((n_pairs, dk, dk), F32) + 2 * _nbytes((dk, dv), F32))
    temps = 24 * _nbytes((tile, dv), F32)
    return pl.pallas_call(
        _gla_kernel,
        out_shape=jax.ShapeDtypeStruct((bsz, t, d), BF16),
        grid=(bsz, GLA_HEADS),
        in_specs=[seq(t, dk), seq(t, dk), seq(t, dv), seq(t, dv), seq(t, dk), seq(t, dk),
                  seq(tc, dk), seq(tc, dv), seq(tc, dk), seq(tc, dk),
                  pl.BlockSpec((1, dv), lambda b, h: (0, h))],
        out_specs=seq(t, dv),
        scratch_shapes=scratch_shapes,
        compiler_params=pltpu.CompilerParams(
            dimension_semantics=("parallel", "parallel"),
            vmem_limit_bytes=_vmem_limit(pipelined, scratch, temps),
        ),
        name="gla",
    )(q, k, v, sr, gf, gb, kc, vc, gfc, gbc, gn)


def _out_kernel(x_ref, ya_ref, yb_ref, sh1_ref, sc1_ref, ga1_ref, sh2_ref, sc2_ref, ga2_ref,
                g1_ref, g2_ref, gfin_ref, wg_ref, bg_ref, wco_ref, wgo_ref, wo_ref, wup_ref, wdn_ref,
                o_ref):
    tm, d = x_ref.shape
    d_ff = wup_ref.shape[1]
    x = x_ref[...]
    h1 = _modulate(x, g1_ref[...], sh1_ref[...], sc1_ref[...]).astype(BF16)
    gate_a = _sigmoid(_dot(h1, wg_ref[:, 0:d]) + bg_ref[:, 0:d])
    y = gate_a * _dot(ya_ref[...], wco_ref[...])
    gate_b = _sigmoid(_dot(h1, wg_ref[:, d:2 * d]) + bg_ref[:, d:2 * d])
    y = y + gate_b * _dot(yb_ref[...], wgo_ref[...])
    x1 = x + ga1_ref[...] * _dot(y.astype(BF16), wo_ref[...])
    h2 = _modulate(x1, g2_ref[...], sh2_ref[...], sc2_ref[...]).astype(BF16)
    acc = jnp.zeros((tm, d), F32)
    for j in range(0, d_ff, FF_BLOCK):
        u = jnp.maximum(_dot(h2, wup_ref[:, j:j + FF_BLOCK]), 0.0)
        acc = acc + _dot((u * u).astype(BF16), wdn_ref[j:j + FF_BLOCK, :])
    x2 = x1 + ga2_ref[...] * acc
    o_ref[...] = _rmsnorm(x2, gfin_ref[...])


def _out_call(x, ya, yb, mod3, g1, g2, gfin, wg, bg, wco, wgo, wo, wup, wdn):
    bsz, t, d = x.shape
    tm = OUT_TM
    tok = lambda: pl.BlockSpec((None, tm, d), lambda b, i: (b, i, 0))
    modrow = lambda col: pl.BlockSpec((None, 1, d), lambda b, i: (b, 0, col))
    pipelined = 2 * _nbytes((tm, d), F32) + 2 * _nbytes((tm, d), BF16) + N_MOD * _nbytes((1, d), F32)
    resident = sum(_nbytes(a.shape, a.dtype) for a in (g1, g2, gfin, wg, bg, wco, wgo, wo, wup, wdn))
    temps = 10 * _nbytes((tm, d), F32)
    return pl.pallas_call(
        _out_kernel,
        out_shape=jax.ShapeDtypeStruct((bsz, t, d), F32),
        grid=(bsz, t // tm),
        in_specs=[tok(), tok(), tok()] + [modrow(cidx) for cidx in range(N_MOD)] + [_resident()] * 10,
        out_specs=tok(),
        compiler_params=pltpu.CompilerParams(
            dimension_semantics=("parallel", "parallel"),
            vmem_limit_bytes=_vmem_limit(pipelined, resident, temps),
        ),
        name="out",
    )(x, ya, yb, mod3, mod3, mod3, mod3, mod3, mod3, g1, g2, gfin, wg, bg, wco, wgo, wo, wup, wdn)


def kernel(x, c, ctx, c_ctx, w_ada, b_ada, g_norm1, w_in, b_in, conv_w, w_conv_out, w_a2_f, b_a_f,
           w_a2_b, b_a_b, g_gla_norm, w_gla_out, w_o, g_norm2, w_up, w_down, g_final):
    depth = w_ada.shape[0]
    assert depth == 1, "only the single-layer block is implemented"
    bsz, t, d = x.shape
    qk = w_a2_f.shape[2]
    dk = qk // GLA_HEADS
    rank = w_a2_f.shape[1]
    gla_tile = GLA_GROUP * GLA_KERNEL_CHUNK
    assert t % INPROJ_TM == 0 and t % OUT_TM == 0 and INPROJ_TM % GRID_W == 0
    assert GLA_PAIR == 2 and GLA_GROUP % GLA_PAIR == 0 and GLA_PAIR * GLA_KERNEL_CHUNK == dk
    gla_pair = GLA_PAIR * GLA_KERNEL_CHUNK
    assert t % (GLA_LOCAL_UNROLL * gla_tile) == 0 and ctx.shape[1] % gla_tile == 0
    assert t % (2 * GLA_SCAN_UNROLL * gla_pair) == 0
    assert bsz + 1 <= MOD_ROWS and 2 * rank <= LR_PAD

    cc = jnp.zeros((MOD_ROWS, d), F32).at[:bsz].set(c).at[bsz].set(c_ctx)
    mod = _ada_call(cc, w_ada[0], b_ada[0][None, :])
    mod3 = mod.reshape(MOD_ROWS, 1, N_MOD * d)

    w = w_in[0]
    bias = b_in[0][None, :]
    o_q = 3 * d
    o_k = o_q + qk
    o_v = o_k + qk
    o_r = o_v + d
    o_lr = o_r + d
    o_g = o_lr + 2 * rank
    lr_pad = LR_PAD - 2 * rank
    w1 = w[:, :o_q].astype(BF16)
    b1 = bias[:, :o_q]
    w_lr = jnp.pad(w[:, o_lr:o_g], ((0, 0), (0, lr_pad)))
    b_lr = jnp.pad(bias[:, o_lr:o_g], ((0, 0), (0, lr_pad)))
    w2 = jnp.concatenate([w[:, o_q:o_lr], w_lr], axis=1).astype(BF16)
    b2 = jnp.concatenate([bias[:, o_q:o_lr], b_lr], axis=1)
    w2c = jnp.concatenate([w[:, o_k:o_r], w_lr], axis=1).astype(BF16)
    b2c = jnp.concatenate([bias[:, o_k:o_r], b_lr], axis=1)
    wg = w[:, o_g:].astype(BF16)
    bg = bias[:, o_g:]
    wa2 = jnp.zeros((LR_PAD, 2 * qk), F32)
    wa2 = wa2.at[:rank, :qk].set(w_a2_f[0]).at[rank:2 * rank, qk:].set(w_a2_b[0]).astype(BF16)
    ba2 = jnp.concatenate([b_a_f[0], b_a_b[0]])[None, :]
    g1 = g_norm1[0][None, :]

    ya, q, k, v, sr, gf, gb = _inproj_call(x, mod3, g1, w1, b1, conv_w[0], w2, b2, wa2, ba2, qk, dk)
    kc, vc, gfc, gbc = _inproj_ctx_call(ctx, mod3, bsz, g1, w2c, b2c, wa2, ba2, qk)
    yb = _gla_call(q, k, v, sr, gf, gb, kc, vc, gfc, gbc, g_gla_norm[0][None, :])
    return _out_call(x, ya, yb, mod3, g1, g_norm2[0][None, :], g_final[None, :], wg, bg,
                     w_conv_out[0].astype(BF16), w_gla_out[0].astype(BF16), w_o[0].astype(BF16),
                     w_up[0].astype(BF16), w_down[0].astype(BF16))
```

```python
import functools

import jax
import jax.numpy as jnp
from jax import lax
from jax.experimental import pallas as pl
from jax.experimental.pallas import tpu as pltpu

F32 = jnp.float32
BF16 = jnp.bfloat16
HIGHEST = lax.Precision.HIGHEST

GLA_HEADS = 4
GLA_TAU = 16.0
GRID_W = 64
N_MOD = 6
RMS_EPS = 1e-6
GLA_RANK = 16
LOG2_E = 1.4426950408889634

LR_PAD = 128
GLA_KERNEL_CHUNK = 64
GLA_PAIR = 2
GLA_GROUP = 4
GLA_LOCAL_UNROLL = 2
GLA_SCAN_UNROLL = 2
INPROJ_TM = 256
OUT_TM = 256
COL_BLOCK = 256
FF_BLOCK = 1024
MOD_ROWS = 16
V7X_VMEM_LIMIT_BYTES = 60000 * 1024


def _vmem_limit(pipelined_bytes, resident_bytes, temp_bytes):
    need = 2 * pipelined_bytes + resident_bytes + temp_bytes
    return int(min(V7X_VMEM_LIMIT_BYTES, need))


def _nbytes(shape, dtype):
    n = 1
    for s in shape:
        n *= s
    return n * jnp.dtype(dtype).itemsize


def _dot(a, b):
    return jnp.dot(a, b, preferred_element_type=F32)


def _nt_dot(a, b):
    return lax.dot_general(a, b, (((1,), (1,)), ((), ())), preferred_element_type=F32)


def _run_skewed(stages):
    pending = None
    for stage in stages:
        epilogue = stage()
        if pending is not None:
            pending()
        pending = epilogue
    if pending is not None:
        pending()


def _sigmoid(x):
    return 1.0 / (1.0 + jnp.exp(-x))


def _log_sigmoid(x):
    return jnp.minimum(x, 0.0) - jnp.log(1.0 + jnp.exp(-jnp.abs(x)))


def _rmsnorm(x, g):
    return x * lax.rsqrt(jnp.mean(x * x, axis=-1, keepdims=True) + RMS_EPS) * g


def _modulate(x, g, shift, scale):
    return _rmsnorm(x, g) * (1.0 + scale) + shift


def _ada_kernel(c_ref, w_ref, b_ref, o_ref):
    c = c_ref[...]
    s = c * _sigmoid(c)
    o_ref[...] = jnp.dot(s, w_ref[...], precision=HIGHEST, preferred_element_type=F32) + b_ref[...]


def _ada_call(cc, w_ada, b_ada):
    d = cc.shape[1]
    n_out = w_ada.shape[1]
    return pl.pallas_call(
        _ada_kernel,
        out_shape=jax.ShapeDtypeStruct((MOD_ROWS, n_out), F32),
        grid=(n_out // d,),
        in_specs=[
            pl.BlockSpec((MOD_ROWS, d), lambda j: (0, 0)),
            pl.BlockSpec((d, d), lambda j: (0, j)),
            pl.BlockSpec((1, d), lambda j: (0, j)),
        ],
        out_specs=pl.BlockSpec((MOD_ROWS, d), lambda j: (0, j)),
        compiler_params=pltpu.CompilerParams(
            dimension_semantics=("arbitrary",),
            vmem_limit_bytes=_vmem_limit(_nbytes((d, d), F32) + _nbytes((MOD_ROWS, 2 * d), F32),
                                         0, 4 * _nbytes((d, d), F32)),
        ),
        name="ada",
    )(cc, w_ada, b_ada)


def _log_decays(lr, wa2_ref, ba2_ref):
    xg = _dot(lr.astype(BF16), wa2_ref[...]) + ba2_ref[...]
    return _log_sigmoid(xg) * (LOG2_E / GLA_TAU)


def _inproj_kernel(x_ref, sh_ref, sc_ref, g1_ref, w1_ref, b1_ref, cw_ref, w2_ref, b2_ref, wa2_ref, ba2_ref,
                   h_ref, ya_ref, q_ref, k_ref, v_ref, sr_ref, gf_ref, gb_ref, *, q_scale):
    tm, d = x_ref.shape
    qk = q_ref.shape[1]
    h_ref[...] = _modulate(x_ref[...], g1_ref[...], sh_ref[...], sc_ref[...]).astype(BF16)

    col_in_row = lax.broadcasted_iota(jnp.int32, (tm, 1), 0) % GRID_W
    has_left = col_in_row != 0
    has_right = col_in_row != GRID_W - 1

    def proj(w_ref, b_ref, c0, width):
        return _dot(h_ref[...], w_ref[:, c0:c0 + width]) + b_ref[:, c0:c0 + width]

    def decay_stage():
        lr = proj(w2_ref, b2_ref, 2 * qk + 2 * d, LR_PAD)

        def epilogue():
            g = _log_decays(lr, wa2_ref, ba2_ref)
            gf_ref[...] = g[:, 0:qk]
            gb_ref[...] = g[:, qk:2 * qk]
        return epilogue

    def conv_stage(j):
        cs = slice(j, j + COL_BLOCK)
        xa = proj(w1_ref, b1_ref, j, COL_BLOCK)
        ba = proj(w1_ref, b1_ref, d + j, COL_BLOCK)
        ca = proj(w1_ref, b1_ref, 2 * d + j, COL_BLOCK)

        def epilogue():
            u = ca * xa
            left = jnp.where(has_left, pltpu.roll(u, 1, 0), 0.0)
            right = jnp.where(has_right, pltpu.roll(u, tm - 1, 0), 0.0)
            y = left * cw_ref[0:1, cs] + u * cw_ref[1:2, cs] + right * cw_ref[2:3, cs]
            ya_ref[:, cs] = (ba * y).astype(BF16)
        return epilogue

    def swish_stage(j):
        r = proj(w2_ref, b2_ref, 2 * qk + d + j, COL_BLOCK)

        def epilogue():
            sr_ref[:, j:j + COL_BLOCK] = (r * _sigmoid(r)).astype(BF16)
        return epilogue

    def qk_stage():
        q = proj(w2_ref, b2_ref, 0, qk)
        k = proj(w2_ref, b2_ref, qk, qk)

        def epilogue():
            q_ref[...] = (q * q_scale).astype(BF16)
            k_ref[...] = k.astype(BF16)
        return epilogue

    def value_stage(j):
        v = proj(w2_ref, b2_ref, 2 * qk + j, COL_BLOCK)

        def epilogue():
            v_ref[:, j:j + COL_BLOCK] = v.astype(BF16)
        return epilogue

    _run_skewed([decay_stage] + [functools.partial(conv_stage, j) for j in range(0, d, COL_BLOCK)]
                + [functools.partial(swish_stage, j) for j in range(0, d, COL_BLOCK)] + [qk_stage]
                + [functools.partial(value_stage, j) for j in range(0, d, COL_BLOCK)])


def _inproj_ctx_kernel(x_ref, sh_ref, sc_ref, g1_ref, w_ref, b_ref, wa2_ref, ba2_ref,
                       k_ref, v_ref, gf_ref, gb_ref):
    d = x_ref.shape[1]
    qk = k_ref.shape[1]
    h = _modulate(x_ref[...], g1_ref[...], sh_ref[...], sc_ref[...]).astype(BF16)
    k_ref[...] = (_dot(h, w_ref[:, 0:qk]) + b_ref[:, 0:qk]).astype(BF16)
    for j in range(0, d, COL_BLOCK):
        c0 = qk + j
        v_ref[:, j:j + COL_BLOCK] = (_dot(h, w_ref[:, c0:c0 + COL_BLOCK]) + b_ref[:, c0:c0 + COL_BLOCK]).astype(BF16)
    c0 = qk + d
    lr = _dot(h, w_ref[:, c0:c0 + LR_PAD]) + b_ref[:, c0:c0 + LR_PAD]
    g = _log_decays(lr, wa2_ref, ba2_ref)
    gf_ref[...] = g[:, 0:qk]
    gb_ref[...] = g[:, qk:2 * qk]


def _resident():
    return pl.BlockSpec(memory_space=pltpu.VMEM)


def _inproj_call(x, mod3, g1, w1, b1, cw, w2, b2, wa2, ba2, qk, dk):
    bsz, t, d = x.shape
    tm = INPROJ_TM
    tok = lambda n: pl.BlockSpec((None, tm, n), lambda b, i: (b, i, 0))
    modrow = lambda col: pl.BlockSpec((None, 1, d), lambda b, i: (b, 0, col))
    out_shapes = (
        jax.ShapeDtypeStruct((bsz, t, d), BF16),
        jax.ShapeDtypeStruct((bsz, t, d), BF16),
        jax.ShapeDtypeStruct((bsz, t, qk), BF16),
        jax.ShapeDtypeStruct((bsz, t, qk), BF16),
        jax.ShapeDtypeStruct((bsz, t, d), BF16),
        jax.ShapeDtypeStruct((bsz, t, d), BF16),
        jax.ShapeDtypeStruct((bsz, t, qk), F32),
        jax.ShapeDtypeStruct((bsz, t, qk), F32),
    )
    pipelined = (_nbytes((tm, d), F32) + 4 * _nbytes((tm, d), BF16) + 2 * _nbytes((tm, qk), BF16)
                 + 2 * _nbytes((tm, qk), F32) + 2 * _nbytes((1, d), F32))
    resident = sum(_nbytes(a.shape, a.dtype) for a in (g1, w1, b1, cw, w2, b2, wa2, ba2))
    temps = 24 * _nbytes((tm, COL_BLOCK), F32) + 3 * _nbytes((tm, 2 * qk), F32)
    return pl.pallas_call(
        functools.partial(_inproj_kernel, q_scale=dk ** -0.5),
        out_shape=out_shapes,
        grid=(bsz, t // tm),
        in_specs=[tok(d), modrow(0), modrow(1)] + [_resident()] * 8,
        out_specs=(tok(d), tok(d), tok(qk), tok(qk), tok(d), tok(d), tok(qk), tok(qk)),
        compiler_params=pltpu.CompilerParams(
            dimension_semantics=("parallel", "parallel"),
            vmem_limit_bytes=_vmem_limit(pipelined, resident, temps),
        ),
        name="inproj",
    )(x, mod3, mod3, g1, w1, b1, cw, w2, b2, wa2, ba2)


def _inproj_ctx_call(ctx, mod3, ctx_row, g1, w, b, wa2, ba2, qk):
    bsz, tc, d = ctx.shape
    tok = lambda n: pl.BlockSpec((None, tc, n), lambda b: (b, 0, 0))
    modrow = lambda col: pl.BlockSpec((None, 1, d), lambda b: (ctx_row, 0, col))
    out_shapes = (
        jax.ShapeDtypeStruct((bsz, tc, qk), BF16),
        jax.ShapeDtypeStruct((bsz, tc, d), BF16),
        jax.ShapeDtypeStruct((bsz, tc, qk), F32),
        jax.ShapeDtypeStruct((bsz, tc, qk), F32),
    )
    pipelined = (_nbytes((tc, d), F32) + _nbytes((tc, d), BF16) + _nbytes((tc, qk), BF16)
                 + 2 * _nbytes((tc, qk), F32) + 2 * _nbytes((1, d), F32))
    resident = sum(_nbytes(a.shape, a.dtype) for a in (g1, w, b, wa2, ba2))
    temps = _nbytes((tc, d), BF16) + 6 * _nbytes((tc, COL_BLOCK), F32) + 3 * _nbytes((tc, 2 * qk), F32)
    return pl.pallas_call(
        _inproj_ctx_kernel,
        out_shape=out_shapes,
        grid=(bsz,),
        in_specs=[tok(d), modrow(0), modrow(1)] + [_resident()] * 5,
        out_specs=(tok(qk), tok(d), tok(qk), tok(qk)),
        compiler_params=pltpu.CompilerParams(
            dimension_semantics=("parallel",),
            vmem_limit_bytes=_vmem_limit(pipelined, resident, temps),
        ),
        name="inproj_ctx",
    )(ctx, mod3, mod3, g1, w, b, wa2, ba2)


def _tile_decays(g, tri, fwd, n_chunks):
    rows, dk = g.shape
    c = rows // n_chunks
    g_hi = g.astype(BF16)
    g_lo = (g - g_hi.astype(F32)).astype(BF16)
    bb = _dot(tri, jnp.concatenate([g_hi, g_lo], axis=1))
    b = (bb[:, :dk] + bb[:, dk:]).reshape(n_chunks, c, dk)
    if fwd:
        return b, b[:, c - 1:c, :], b[:, c // 2 - 1:c // 2, :]
    return b, b[:, 0:1, :], b[:, c // 2:c // 2 + 1, :]


def _pair_offsets(tot, fwd):
    n_chunks = tot.shape[0]
    zero = jnp.zeros_like(tot[0:1])
    on_odd = jnp.concatenate([t for p in range(0, n_chunks, GLA_PAIR) for t in (zero, tot[p:p + 1])], axis=0)
    on_even = jnp.concatenate([t for p in range(0, n_chunks, GLA_PAIR) for t in (tot[p + 1:p + 2], zero)], axis=0)
    pair_tot = jnp.concatenate([tot[p:p + 1] + tot[p + 1:p + 2] for p in range(0, n_chunks, GLA_PAIR)], axis=0)
    return (on_odd, on_even, pair_tot) if fwd else (on_even, on_odd, pair_tot)


def _lane_broadcast_column(row):
    n = row.shape[1]
    return jnp.broadcast_to(row, (n, n)).T


def _gla_kernel(q_ref, k_ref, v_ref, sr_ref, gf_ref, gb_ref, kc_ref, vc_ref, gfc_ref, gbc_ref, gn_ref,
                out_ref, a_scr, o_scr, ktf_scr, ktb_scr, qbf_scr, qbb_scr, dmf_scr, dmb_scr, sf_scr, sb_scr):
    c = GLA_KERNEL_CHUNK
    grp = GLA_GROUP
    tile = c * grp
    pair = c * GLA_PAIR
    pairs_per_tile = grp // GLA_PAIR
    t, dk = q_ref.shape
    dv = v_ref.shape[1]
    n_pairs = t // pair
    row = lax.broadcasted_iota(jnp.int32, (tile, tile), 0)
    col = lax.broadcasted_iota(jnp.int32, (tile, tile), 1)
    row_chunk = row // c
    col_chunk = col // c
    same_pair = (row // pair) == (col // pair)
    inside_f = (row_chunk == col_chunk) & (row >= col)
    inside_b = (row_chunk == col_chunk) & (row <= col)
    across_f = same_pair & (row_chunk == col_chunk + 1)
    across_b = same_pair & (row_chunk == col_chunk - 1)
    dirs = ((True, gf_ref, gfc_ref, inside_f, across_f, ktf_scr, qbf_scr, dmf_scr, sf_scr),
            (False, gb_ref, gbc_ref, inside_b, across_b, ktb_scr, qbb_scr, dmb_scr, sb_scr))

    def key_side(k3, b, tot, k_off):
        kt = k3 * jnp.exp2(tot - b)
        return kt, kt * jnp.exp2(k_off)

    ctx_tiles = kc_ref.shape[0] // tile
    for fwd, _, gc_ref, inside, _, _, _, _, s_scr in dirs:
        s = jnp.zeros((dk, dv), F32)
        for ti in (range(ctx_tiles) if fwd else reversed(range(ctx_tiles))):
            rows = slice(ti * tile, (ti + 1) * tile)
            b, tot, _ = _tile_decays(gc_ref[rows, :], inside.astype(BF16), fwd, grp)
            _, k_off, pair_tot = _pair_offsets(tot, fwd)
            _, kt_pair = key_side(kc_ref[rows, :].astype(F32).reshape(grp, c, dk), b, tot, k_off)
            kt_t = kt_pair.reshape(tile, dk).T.astype(BF16)
            decay = jnp.exp2(pair_tot)
            for p in (range(pairs_per_tile) if fwd else reversed(range(pairs_per_tile))):
                dm = _lane_broadcast_column(decay[p])
                s = (s * jnp.concatenate([dm] * (dv // dk), axis=1)
                     + _dot(kt_t[:, p * pair:(p + 1) * pair],
                            vc_ref[ti * tile + p * pair:ti * tile + (p + 1) * pair, :]))
        s_scr[...] = s

    def local_body(i, carry):
        tiles = [i * GLA_LOCAL_UNROLL + u for u in range(GLA_LOCAL_UNROLL)]
        rows = [pl.ds(pl.multiple_of(ti * tile, tile), tile) for ti in tiles]
        items = [(u, d) for u in range(GLA_LOCAL_UNROLL) for d in dirs]
        cums = [_tile_decays(d[1][rows[u], :], d[3].astype(BF16), d[0], grp) for u, d in items]
        q3 = [q_ref[sl, :].astype(F32).reshape(grp, c, dk) for sl in rows]
        k3 = [k_ref[sl, :].astype(F32).reshape(grp, c, dk) for sl in rows]
        products = []
        for (u, d), (b, tot, bm) in zip(items, cums):
            fwd, _, _, _, _, kt_scr, qb_scr, dm_scr, _ = d
            q_off, k_off, pair_tot = _pair_offsets(tot, fwd)
            qh = (q3[u] * jnp.exp2(b - bm)).reshape(tile, dk).astype(BF16)
            kh = (k3[u] * jnp.exp2(bm - b)).reshape(tile, dk).astype(BF16)
            qb = q3[u] * jnp.exp2(b)
            kt, kt_pair = key_side(k3[u], b, tot, k_off)
            products.append((_nt_dot(qh, kh),
                             _nt_dot(qb.reshape(tile, dk).astype(BF16), kt.reshape(tile, dk).astype(BF16))))
            qb_scr[rows[u], :] = (qb * jnp.exp2(q_off)).reshape(tile, dk).astype(BF16)
            kt_t = kt_pair.reshape(tile, dk).T.astype(BF16)
            decay = jnp.exp2(pair_tot)
            for p in range(pairs_per_tile):
                kt_scr[tiles[u] * pairs_per_tile + p] = kt_t[:, p * pair:(p + 1) * pair]
                dm_scr[tiles[u] * pairs_per_tile + p] = _lane_broadcast_column(decay[p])
        for u in range(GLA_LOCAL_UNROLL):
            scores = jnp.zeros((tile, tile), F32)
            for (iu, d), (inner, outer) in zip(items, products):
                if iu == u:
                    scores = scores + jnp.where(d[3], inner, 0.0) + jnp.where(d[4], outer, 0.0)
            scores = scores.astype(BF16)
            for p in range(pairs_per_tile):
                a_scr[pl.ds(pl.multiple_of(tiles[u] * tile + p * pair, pair), pair), :] = (
                    scores[p * pair:(p + 1) * pair, p * pair:(p + 1) * pair])
        return carry

    lax.fori_loop(0, t // (tile * GLA_LOCAL_UNROLL), local_body, 0)

    def scan_body(i, carry, finish):
        for fwd, _, _, _, _, kt_scr, qb_scr, dm_scr, s_scr in dirs:
            j = i if fwd else n_pairs - 1 - i
            sl = pl.ds(pl.multiple_of(j * pair, pair), pair)
            s = s_scr[...]
            v = v_ref[sl, :]
            o = _dot(qb_scr[sl, :], s.astype(BF16))
            s_scr[...] = s * jnp.concatenate([dm_scr[j]] * (dv // dk), axis=1) + _dot(kt_scr[j], v)
            if finish:
                y = _rmsnorm(o + o_scr[sl, :], gn_ref[...])
                out_ref[sl, :] = (y * sr_ref[sl, :].astype(F32)).astype(BF16)
            else:
                o_scr[sl, :] = o + _dot(a_scr[sl, :], v)
        return carry

    lax.fori_loop(0, n_pairs // 2, functools.partial(scan_body, finish=False), 0, unroll=GLA_SCAN_UNROLL)
    lax.fori_loop(n_pairs // 2, n_pairs, functools.partial(scan_body, finish=True), 0, unroll=GLA_SCAN_UNROLL)


def _gla_call(q, k, v, sr, gf, gb, kc, vc, gfc, gbc, gn):
    bsz, t, qk = q.shape
    d = v.shape[2]
    tc = kc.shape[1]
    dk = qk // GLA_HEADS
    dv = d // GLA_HEADS
    pair = GLA_KERNEL_CHUNK * GLA_PAIR
    n_pairs = t // pair
    tile = GLA_KERNEL_CHUNK * GLA_GROUP
    seq = lambda rows, width: pl.BlockSpec((None, rows, width), lambda b, h: (b, 0, h))
    scratch_shapes = [
        pltpu.VMEM((t, pair), BF16),
        pltpu.VMEM((t, dv), F32),
        pltpu.VMEM((n_pairs, dk, pair), BF16), pltpu.VMEM((n_pairs, dk, pair), BF16),
        pltpu.VMEM((t, dk), BF16), pltpu.VMEM((t, dk), BF16),
        pltpu.VMEM((n_pairs, dk, dk), F32), pltpu.VMEM((n_pairs, dk, dk), F32),
        pltpu.VMEM((dk, dv), F32), pltpu.VMEM((dk, dv), F32),
    ]
    pipelined = (2 * _nbytes((t, dk), BF16) + 3 * _nbytes((t, dv), BF16) + 2 * _nbytes((t, dk), F32)
                 + _nbytes((tc, dk), BF16) + _nbytes((tc, dv), BF16) + 2 * _nbytes((tc, dk), F32)
                 + _nbytes((1, dv), F32))
    scratch = (_nbytes((t, pair), BF16) + _nbytes((t, dv), F32) + 2 * _nbytes((n_pairs, dk, pair), BF16)
               + 2 * _nbytes((t, dk), BF16) + 2 * _nbytes((n_pairs, dk, dk), F32) + 2 * _nbytes((dk, dv), F32))
    temps = 24 * _nbytes((tile, dv), F32)
    return pl.pallas_call(
        _gla_kernel,
        out_shape=jax.ShapeDtypeStruct((bsz, t, d), BF16),
        grid=(bsz, GLA_HEADS),
        in_specs=[seq(t, dk), seq(t, dk), seq(t, dv), seq(t, dv), seq(t, dk), seq(t, dk),
                  seq(tc, dk), seq(tc, dv), seq(tc, dk), seq(tc, dk),
                  pl.BlockSpec((1, dv), lambda b, h: (0, h))],
        out_specs=seq(t, dv),
        scratch_shapes=scratch_shapes,
        compiler_params=pltpu.CompilerParams(
            dimension_semantics=("parallel", "parallel"),
            vmem_limit_bytes=_vmem_limit(pipelined, scratch, temps),
        ),
        name="gla",
    )(q, k, v, sr, gf, gb, kc, vc, gfc, gbc, gn)


def _out_kernel(x_ref, h1_ref, ya_ref, yb_ref, ga1_ref, sh2_ref, sc2_ref, ga2_ref, g2_ref, gfin_ref,
                wg_ref, bg_ref, wco_ref, wgo_ref, wo_ref, wup_ref, wdn_ref, o_ref, y_scr, h2_scr):
    tm, d = x_ref.shape
    d_ff = wup_ref.shape[1]

    def merge_stage(j):
        cs = slice(j, j + COL_BLOCK)
        h1 = h1_ref[...]
        za = _dot(h1, wg_ref[:, j:j + COL_BLOCK]) + bg_ref[:, j:j + COL_BLOCK]
        zb = _dot(h1, wg_ref[:, d + j:d + j + COL_BLOCK]) + bg_ref[:, d + j:d + j + COL_BLOCK]
        pa = _dot(ya_ref[...], wco_ref[:, cs])
        pb = _dot(yb_ref[...], wgo_ref[:, cs])

        def epilogue():
            y_scr[:, cs] = (_sigmoid(za) * pa + _sigmoid(zb) * pb).astype(BF16)
        return epilogue

    _run_skewed([functools.partial(merge_stage, j) for j in range(0, d, COL_BLOCK)])
    o_ref[...] = x_ref[...] + ga1_ref[...] * _dot(y_scr[...], wo_ref[...])
    h2_scr[...] = _modulate(o_ref[...], g2_ref[...], sh2_ref[...], sc2_ref[...]).astype(BF16)

    parts = []

    def mlp_stage(j):
        u = _dot(h2_scr[...], wup_ref[:, j:j + FF_BLOCK])

        def epilogue():
            a = jnp.maximum(u, 0.0)
            parts.append(_dot((a * a).astype(BF16), wdn_ref[j:j + FF_BLOCK, :]))
        return epilogue

    _run_skewed([functools.partial(mlp_stage, j) for j in range(0, d_ff, FF_BLOCK)])
    x2 = o_ref[...] + ga2_ref[...] * functools.reduce(lambda a, b: a + b, parts)
    o_ref[...] = _rmsnorm(x2, gfin_ref[...])


def _out_call(x, h1, ya, yb, mod3, g2, gfin, wg, bg, wco, wgo, wo, wup, wdn):
    bsz, t, d = x.shape
    tm = OUT_TM
    tok = lambda: pl.BlockSpec((None, tm, d), lambda b, i: (b, i, 0))
    modrow = lambda col: pl.BlockSpec((None, 1, d), lambda b, i: (b, 0, col))
    pipelined = 2 * _nbytes((tm, d), F32) + 3 * _nbytes((tm, d), BF16) + 4 * _nbytes((1, d), F32)
    resident = sum(_nbytes(a.shape, a.dtype) for a in (g2, gfin, wg, bg, wco, wgo, wo, wup, wdn))
    scratch = 2 * _nbytes((tm, d), BF16)
    temps = 10 * _nbytes((tm, d), F32)
    return pl.pallas_call(
        _out_kernel,
        out_shape=jax.ShapeDtypeStruct((bsz, t, d), F32),
        grid=(bsz, t // tm),
        in_specs=[tok(), tok(), tok(), tok()] + [modrow(cidx) for cidx in (2, 3, 4, 5)] + [_resident()] * 9,
        out_specs=tok(),
        scratch_shapes=[pltpu.VMEM((tm, d), BF16), pltpu.VMEM((tm, d), BF16)],
        compiler_params=pltpu.CompilerParams(
            dimension_semantics=("parallel", "parallel"),
            vmem_limit_bytes=_vmem_limit(pipelined, resident + scratch, temps),
        ),
        name="out",
    )(x, h1, ya, yb, mod3, mod3, mod3, mod3, g2, gfin, wg, bg, wco, wgo, wo, wup, wdn)


def kernel(x, c, ctx, c_ctx, w_ada, b_ada, g_norm1, w_in, b_in, conv_w, w_conv_out, w_a2_f, b_a_f,
           w_a2_b, b_a_b, g_gla_norm, w_gla_out, w_o, g_norm2, w_up, w_down, g_final):
    depth = w_ada.shape[0]
    assert depth == 1, "only the single-layer block is implemented"
    bsz, t, d = x.shape
    qk = w_a2_f.shape[2]
    dk = qk // GLA_HEADS
    rank = w_a2_f.shape[1]
    gla_tile = GLA_GROUP * GLA_KERNEL_CHUNK
    assert t % INPROJ_TM == 0 and t % OUT_TM == 0 and INPROJ_TM % GRID_W == 0
    assert GLA_PAIR == 2 and GLA_GROUP % GLA_PAIR == 0 and GLA_PAIR * GLA_KERNEL_CHUNK == dk
    gla_pair = GLA_PAIR * GLA_KERNEL_CHUNK
    assert t % (GLA_LOCAL_UNROLL * gla_tile) == 0 and ctx.shape[1] % gla_tile == 0
    assert t % (2 * GLA_SCAN_UNROLL * gla_pair) == 0
    assert bsz + 1 <= MOD_ROWS and 2 * rank <= LR_PAD

    cc = jnp.zeros((MOD_ROWS, d), F32).at[:bsz].set(c).at[bsz].set(c_ctx)
    mod = _ada_call(cc, w_ada[0], b_ada[0][None, :])
    mod3 = mod.reshape(MOD_ROWS, 1, N_MOD * d)

    w = w_in[0]
    bias = b_in[0][None, :]
    o_q = 3 * d
    o_k = o_q + qk
    o_v = o_k + qk
    o_r = o_v + d
    o_lr = o_r + d
    o_g = o_lr + 2 * rank
    lr_pad = LR_PAD - 2 * rank
    w1 = w[:, :o_q].astype(BF16)
    b1 = bias[:, :o_q]
    w_lr = jnp.pad(w[:, o_lr:o_g], ((0, 0), (0, lr_pad)))
    b_lr = jnp.pad(bias[:, o_lr:o_g], ((0, 0), (0, lr_pad)))
    w2 = jnp.concatenate([w[:, o_q:o_lr], w_lr], axis=1).astype(BF16)
    b2 = jnp.concatenate([bias[:, o_q:o_lr], b_lr], axis=1)
    w2c = jnp.concatenate([w[:, o_k:o_r], w_lr], axis=1).astype(BF16)
    b2c = jnp.concatenate([bias[:, o_k:o_r], b_lr], axis=1)
    wg = w[:, o_g:].astype(BF16)
    bg = bias[:, o_g:]
    wa2 = jnp.zeros((LR_PAD, 2 * qk), F32)
    wa2 = wa2.at[:rank, :qk].set(w_a2_f[0]).at[rank:2 * rank, qk:].set(w_a2_b[0]).astype(BF16)
    ba2 = jnp.concatenate([b_a_f[0], b_a_b[0]])[None, :]
    g1 = g_norm1[0][None, :]

    h1, ya, q, k, v, sr, gf, gb = _inproj_call(x, mod3, g1, w1, b1, conv_w[0], w2, b2, wa2, ba2, qk, dk)
    kc, vc, gfc, gbc = _inproj_ctx_call(ctx, mod3, bsz, g1, w2c, b2c, wa2, ba2, qk)
    yb = _gla_call(q, k, v, sr, gf, gb, kc, vc, gfc, gbc, g_gla_norm[0][None, :])
    return _out_call(x, h1, ya, yb, mod3, g_norm2[0][None, :], g_final[None, :], wg, bg,
                     w_conv_out[0].astype(BF16), w_gla_out[0].astype(BF16), w_o[0].astype(BF16),
                     w_up[0].astype(BF16), w_down[0].astype(BF16))
```

```python
import functools

import jax
import jax.numpy as jnp
from jax import lax
from jax.experimental import pallas as pl
from jax.experimental.pallas import tpu as pltpu

F32 = jnp.float32
BF16 = jnp.bfloat16
HIGHEST = lax.Precision.HIGHEST

GLA_HEADS = 4
GLA_TAU = 16.0
GRID_W = 64
N_MOD = 6
RMS_EPS = 1e-6
GLA_RANK = 16
LOG2_E = 1.4426950408889634

LR_PAD = 128
GLA_KERNEL_CHUNK = 64
GLA_PAIR = 2
GLA_GROUP = 4
GLA_LOCAL_UNROLL = 2
GLA_SCAN_UNROLL = 2
INPROJ_TM = 512
OUT_TM = 512
COL_BLOCK = 256
FF_BLOCK = 1024
MOD_ROWS = 16
V7X_VMEM_LIMIT_BYTES = 60000 * 1024


def _vmem_limit(pipelined_bytes, resident_bytes, temp_bytes):
    need = 2 * pipelined_bytes + resident_bytes + temp_bytes
    return int(min(V7X_VMEM_LIMIT_BYTES, need))


def _nbytes(shape, dtype):
    n = 1
    for s in shape:
        n *= s
    return n * jnp.dtype(dtype).itemsize


def _dot(a, b):
    return jnp.dot(a, b, preferred_element_type=F32)


def _nt_dot(a, b):
    return lax.dot_general(a, b, (((1,), (1,)), ((), ())), preferred_element_type=F32)


def _run_skewed(stages):
    pending = None
    for stage in stages:
        epilogue = stage()
        if pending is not None:
            pending()
        pending = epilogue
    if pending is not None:
        pending()


def _sigmoid(x):
    return 1.0 / (1.0 + jnp.exp(-x))


def _log_sigmoid(x):
    return jnp.minimum(x, 0.0) - jnp.log(1.0 + jnp.exp(-jnp.abs(x)))


def _rmsnorm(x, g):
    return x * lax.rsqrt(jnp.mean(x * x, axis=-1, keepdims=True) + RMS_EPS) * g


def _modulate(x, g, shift, scale):
    return _rmsnorm(x, g) * (1.0 + scale) + shift


def _ada_kernel(c_ref, w_ref, b_ref, o_ref):
    c = c_ref[...]
    s = c * _sigmoid(c)
    o_ref[...] = jnp.dot(s, w_ref[...], precision=HIGHEST, preferred_element_type=F32) + b_ref[...]


def _ada_call(cc, w_ada, b_ada):
    d = cc.shape[1]
    n_out = w_ada.shape[1]
    return pl.pallas_call(
        _ada_kernel,
        out_shape=jax.ShapeDtypeStruct((MOD_ROWS, n_out), F32),
        grid=(n_out // d,),
        in_specs=[
            pl.BlockSpec((MOD_ROWS, d), lambda j: (0, 0)),
            pl.BlockSpec((d, d), lambda j: (0, j)),
            pl.BlockSpec((1, d), lambda j: (0, j)),
        ],
        out_specs=pl.BlockSpec((MOD_ROWS, d), lambda j: (0, j)),
        compiler_params=pltpu.CompilerParams(
            dimension_semantics=("arbitrary",),
            vmem_limit_bytes=_vmem_limit(_nbytes((d, d), F32) + _nbytes((MOD_ROWS, 2 * d), F32),
                                         0, 4 * _nbytes((d, d), F32)),
        ),
        name="ada",
    )(cc, w_ada, b_ada)


def _log_decays(lr, wa2_ref, ba2_ref):
    xg = _dot(lr.astype(BF16), wa2_ref[...]) + ba2_ref[...]
    return _log_sigmoid(xg) * (LOG2_E / GLA_TAU)


def _inproj_kernel(x_ref, sh_ref, sc_ref, g1_ref, w1_ref, b1_ref, cw_ref, w2_ref, b2_ref, wa2_ref, ba2_ref,
                   h_ref, ya_ref, q_ref, k_ref, v_ref, sr_ref, gf_ref, gb_ref, *, q_scale):
    tm, d = x_ref.shape
    qk = q_ref.shape[1]
    h_ref[...] = _modulate(x_ref[...], g1_ref[...], sh_ref[...], sc_ref[...]).astype(BF16)

    col_in_row = lax.broadcasted_iota(jnp.int32, (tm, 1), 0) % GRID_W
    has_left = col_in_row != 0
    has_right = col_in_row != GRID_W - 1

    def proj(w_ref, b_ref, c0, width):
        return _dot(h_ref[...], w_ref[:, c0:c0 + width]) + b_ref[:, c0:c0 + width]

    def decay_stage():
        lr = proj(w2_ref, b2_ref, 2 * qk + 2 * d, LR_PAD)

        def epilogue():
            g = _log_decays(lr, wa2_ref, ba2_ref)
            gf_ref[...] = g[:, 0:qk]
            gb_ref[...] = g[:, qk:2 * qk]
        return epilogue

    def conv_stage(j):
        cs = slice(j, j + COL_BLOCK)
        xa = proj(w1_ref, b1_ref, j, COL_BLOCK)
        ba = proj(w1_ref, b1_ref, d + j, COL_BLOCK)
        ca = proj(w1_ref, b1_ref, 2 * d + j, COL_BLOCK)

        def epilogue():
            u = ca * xa
            left = jnp.where(has_left, pltpu.roll(u, 1, 0), 0.0)
            right = jnp.where(has_right, pltpu.roll(u, tm - 1, 0), 0.0)
            y = left * cw_ref[0:1, cs] + u * cw_ref[1:2, cs] + right * cw_ref[2:3, cs]
            ya_ref[:, cs] = (ba * y).astype(BF16)
        return epilogue

    def swish_stage(j):
        r = proj(w2_ref, b2_ref, 2 * qk + d + j, COL_BLOCK)

        def epilogue():
            sr_ref[:, j:j + COL_BLOCK] = (r * _sigmoid(r)).astype(BF16)
        return epilogue

    def qk_stage():
        q = proj(w2_ref, b2_ref, 0, qk)
        k = proj(w2_ref, b2_ref, qk, qk)

        def epilogue():
            q_ref[...] = (q * q_scale).astype(BF16)
            k_ref[...] = k.astype(BF16)
        return epilogue

    def value_stage(j):
        v = proj(w2_ref, b2_ref, 2 * qk + j, COL_BLOCK)

        def epilogue():
            v_ref[:, j:j + COL_BLOCK] = v.astype(BF16)
        return epilogue

    _run_skewed([decay_stage] + [functools.partial(conv_stage, j) for j in range(0, d, COL_BLOCK)]
                + [functools.partial(swish_stage, j) for j in range(0, d, COL_BLOCK)] + [qk_stage]
                + [functools.partial(value_stage, j) for j in range(0, d, COL_BLOCK)])


def _inproj_ctx_kernel(x_ref, sh_ref, sc_ref, g1_ref, w_ref, b_ref, wa2_ref, ba2_ref,
                       k_ref, v_ref, gf_ref, gb_ref):
    d = x_ref.shape[1]
    qk = k_ref.shape[1]
    h = _modulate(x_ref[...], g1_ref[...], sh_ref[...], sc_ref[...]).astype(BF16)
    k_ref[...] = (_dot(h, w_ref[:, 0:qk]) + b_ref[:, 0:qk]).astype(BF16)
    for j in range(0, d, COL_BLOCK):
        c0 = qk + j
        v_ref[:, j:j + COL_BLOCK] = (_dot(h, w_ref[:, c0:c0 + COL_BLOCK]) + b_ref[:, c0:c0 + COL_BLOCK]).astype(BF16)
    c0 = qk + d
    lr = _dot(h, w_ref[:, c0:c0 + LR_PAD]) + b_ref[:, c0:c0 + LR_PAD]
    g = _log_decays(lr, wa2_ref, ba2_ref)
    gf_ref[...] = g[:, 0:qk]
    gb_ref[...] = g[:, qk:2 * qk]


def _resident():
    return pl.BlockSpec(memory_space=pltpu.VMEM)


def _inproj_call(x, mod3, g1, w1, b1, cw, w2, b2, wa2, ba2, qk, dk):
    bsz, t, d = x.shape
    tm = INPROJ_TM
    tok = lambda n: pl.BlockSpec((None, tm, n), lambda b, i: (b, i, 0))
    modrow = lambda col: pl.BlockSpec((None, 1, d), lambda b, i: (b, 0, col))
    out_shapes = (
        jax.ShapeDtypeStruct((bsz, t, d), BF16),
        jax.ShapeDtypeStruct((bsz, t, d), BF16),
        jax.ShapeDtypeStruct((bsz, t, qk), BF16),
        jax.ShapeDtypeStruct((bsz, t, qk), BF16),
        jax.ShapeDtypeStruct((bsz, t, d), BF16),
        jax.ShapeDtypeStruct((bsz, t, d), BF16),
        jax.ShapeDtypeStruct((bsz, t, qk), F32),
        jax.ShapeDtypeStruct((bsz, t, qk), F32),
    )
    pipelined = (_nbytes((tm, d), F32) + 4 * _nbytes((tm, d), BF16) + 2 * _nbytes((tm, qk), BF16)
                 + 2 * _nbytes((tm, qk), F32) + 2 * _nbytes((1, d), F32))
    resident = sum(_nbytes(a.shape, a.dtype) for a in (g1, w1, b1, cw, w2, b2, wa2, ba2))
    temps = 24 * _nbytes((tm, COL_BLOCK), F32) + 3 * _nbytes((tm, 2 * qk), F32)
    return pl.pallas_call(
        functools.partial(_inproj_kernel, q_scale=dk ** -0.5),
        out_shape=out_shapes,
        grid=(bsz, t // tm),
        in_specs=[tok(d), modrow(0), modrow(1)] + [_resident()] * 8,
        out_specs=(tok(d), tok(d), tok(qk), tok(qk), tok(d), tok(d), tok(qk), tok(qk)),
        compiler_params=pltpu.CompilerParams(
            dimension_semantics=("parallel", "parallel"),
            vmem_limit_bytes=_vmem_limit(pipelined, resident, temps),
        ),
        name="inproj",
    )(x, mod3, mod3, g1, w1, b1, cw, w2, b2, wa2, ba2)


def _inproj_ctx_call(ctx, mod3, ctx_row, g1, w, b, wa2, ba2, qk):
    bsz, tc, d = ctx.shape
    tok = lambda n: pl.BlockSpec((None, tc, n), lambda b: (b, 0, 0))
    modrow = lambda col: pl.BlockSpec((None, 1, d), lambda b: (ctx_row, 0, col))
    out_shapes = (
        jax.ShapeDtypeStruct((bsz, tc, qk), BF16),
        jax.ShapeDtypeStruct((bsz, tc, d), BF16),
        jax.ShapeDtypeStruct((bsz, tc, qk), F32),
        jax.ShapeDtypeStruct((bsz, tc, qk), F32),
    )
    pipelined = (_nbytes((tc, d), F32) + _nbytes((tc, d), BF16) + _nbytes((tc, qk), BF16)
                 + 2 * _nbytes((tc, qk), F32) + 2 * _nbytes((1, d), F32))
    resident = sum(_nbytes(a.shape, a.dtype) for a in (g1, w, b, wa2, ba2))
    temps = _nbytes((tc, d), BF16) + 6 * _nbytes((tc, COL_BLOCK), F32) + 3 * _nbytes((tc, 2 * qk), F32)
    return pl.pallas_call(
        _inproj_ctx_kernel,
        out_shape=out_shapes,
        grid=(bsz,),
        in_specs=[tok(d), modrow(0), modrow(1)] + [_resident()] * 5,
        out_specs=(tok(qk), tok(d), tok(qk), tok(qk)),
        compiler_params=pltpu.CompilerParams(
            dimension_semantics=("parallel",),
            vmem_limit_bytes=_vmem_limit(pipelined, resident, temps),
        ),
        name="inproj_ctx",
    )(ctx, mod3, mod3, g1, w, b, wa2, ba2)


def _tile_decays(g, tri, fwd, n_chunks):
    rows, dk = g.shape
    c = rows // n_chunks
    g_hi = g.astype(BF16)
    g_lo = (g - g_hi.astype(F32)).astype(BF16)
    bb = _dot(tri, jnp.concatenate([g_hi, g_lo], axis=1))
    b = (bb[:, :dk] + bb[:, dk:]).reshape(n_chunks, c, dk)
    if fwd:
        return b, b[:, c - 1:c, :], b[:, c // 2 - 1:c // 2, :]
    return b, b[:, 0:1, :], b[:, c // 2:c // 2 + 1, :]


def _pair_offsets(tot, fwd):
    n_chunks = tot.shape[0]
    zero = jnp.zeros_like(tot[0:1])
    on_odd = jnp.concatenate([t for p in range(0, n_chunks, GLA_PAIR) for t in (zero, tot[p:p + 1])], axis=0)
    on_even = jnp.concatenate([t for p in range(0, n_chunks, GLA_PAIR) for t in (tot[p + 1:p + 2], zero)], axis=0)
    pair_tot = jnp.concatenate([tot[p:p + 1] + tot[p + 1:p + 2] for p in range(0, n_chunks, GLA_PAIR)], axis=0)
    return (on_odd, on_even, pair_tot) if fwd else (on_even, on_odd, pair_tot)


def _lane_broadcast_column(row):
    n = row.shape[1]
    return jnp.broadcast_to(row, (n, n)).T


def _gla_kernel(q_ref, k_ref, v_ref, sr_ref, gf_ref, gb_ref, kc_ref, vc_ref, gfc_ref, gbc_ref, gn_ref,
                out_ref, a_scr, o_scr, ktf_scr, ktb_scr, qbf_scr, qbb_scr, dmf_scr, dmb_scr, sf_scr, sb_scr):
    c = GLA_KERNEL_CHUNK
    grp = GLA_GROUP
    tile = c * grp
    pair = c * GLA_PAIR
    pairs_per_tile = grp // GLA_PAIR
    t, dk = q_ref.shape
    dv = v_ref.shape[1]
    n_pairs = t // pair
    row = lax.broadcasted_iota(jnp.int32, (tile, tile), 0)
    col = lax.broadcasted_iota(jnp.int32, (tile, tile), 1)
    row_chunk = row // c
    col_chunk = col // c
    same_pair = (row // pair) == (col // pair)
    inside_f = (row_chunk == col_chunk) & (row >= col)
    inside_b = (row_chunk == col_chunk) & (row <= col)
    across_f = same_pair & (row_chunk == col_chunk + 1)
    across_b = same_pair & (row_chunk == col_chunk - 1)
    dirs = ((True, gf_ref, gfc_ref, inside_f, across_f, ktf_scr, qbf_scr, dmf_scr, sf_scr),
            (False, gb_ref, gbc_ref, inside_b, across_b, ktb_scr, qbb_scr, dmb_scr, sb_scr))

    def key_side(k3, b, tot, k_off):
        kt = k3 * jnp.exp2(tot - b)
        return kt, kt * jnp.exp2(k_off)

    ctx_tiles = kc_ref.shape[0] // tile
    for fwd, _, gc_ref, inside, _, _, _, _, s_scr in dirs:
        s = jnp.zeros((dk, dv), F32)
        for ti in (range(ctx_tiles) if fwd else reversed(range(ctx_tiles))):
            rows = slice(ti * tile, (ti + 1) * tile)
            b, tot, _ = _tile_decays(gc_ref[rows, :], inside.astype(BF16), fwd, grp)
            _, k_off, pair_tot = _pair_offsets(tot, fwd)
            _, kt_pair = key_side(kc_ref[rows, :].astype(F32).reshape(grp, c, dk), b, tot, k_off)
            kt_t = kt_pair.reshape(tile, dk).T.astype(BF16)
            decay = jnp.exp2(pair_tot)
            for p in (range(pairs_per_tile) if fwd else reversed(range(pairs_per_tile))):
                dm = _lane_broadcast_column(decay[p])
                s = (s * jnp.concatenate([dm] * (dv // dk), axis=1)
                     + _dot(kt_t[:, p * pair:(p + 1) * pair],
                            vc_ref[ti * tile + p * pair:ti * tile + (p + 1) * pair, :]))
        s_scr[...] = s

    def local_body(i, carry):
        tiles = [i * GLA_LOCAL_UNROLL + u for u in range(GLA_LOCAL_UNROLL)]
        rows = [pl.ds(pl.multiple_of(ti * tile, tile), tile) for ti in tiles]
        items = [(u, d) for u in range(GLA_LOCAL_UNROLL) for d in dirs]
        cums = [_tile_decays(d[1][rows[u], :], d[3].astype(BF16), d[0], grp) for u, d in items]
        q3 = [q_ref[sl, :].astype(F32).reshape(grp, c, dk) for sl in rows]
        k3 = [k_ref[sl, :].astype(F32).reshape(grp, c, dk) for sl in rows]
        products = []
        for (u, d), (b, tot, bm) in zip(items, cums):
            fwd, _, _, _, _, kt_scr, qb_scr, dm_scr, _ = d
            q_off, k_off, pair_tot = _pair_offsets(tot, fwd)
            qh = (q3[u] * jnp.exp2(b - bm)).reshape(tile, dk).astype(BF16)
            kh = (k3[u] * jnp.exp2(bm - b)).reshape(tile, dk).astype(BF16)
            qb = q3[u] * jnp.exp2(b)
            kt, kt_pair = key_side(k3[u], b, tot, k_off)
            products.append((_nt_dot(qh, kh),
                             _nt_dot(qb.reshape(tile, dk).astype(BF16), kt.reshape(tile, dk).astype(BF16))))
            qb_scr[rows[u], :] = (qb * jnp.exp2(q_off)).reshape(tile, dk).astype(BF16)
            kt_t = kt_pair.reshape(tile, dk).T.astype(BF16)
            decay = jnp.exp2(pair_tot)
            for p in range(pairs_per_tile):
                kt_scr[tiles[u] * pairs_per_tile + p] = kt_t[:, p * pair:(p + 1) * pair]
                dm_scr[tiles[u] * pairs_per_tile + p] = _lane_broadcast_column(decay[p])
        for u in range(GLA_LOCAL_UNROLL):
            scores = jnp.zeros((tile, tile), F32)
            for (iu, d), (inner, outer) in zip(items, products):
                if iu == u:
                    scores = scores + jnp.where(d[3], inner, 0.0) + jnp.where(d[4], outer, 0.0)
            scores = scores.astype(BF16)
            for p in range(pairs_per_tile):
                a_scr[pl.ds(pl.multiple_of(tiles[u] * tile + p * pair, pair), pair), :] = (
                    scores[p * pair:(p + 1) * pair, p * pair:(p + 1) * pair])
        return carry

    lax.fori_loop(0, t // (tile * GLA_LOCAL_UNROLL), local_body, 0)

    def scan_body(i, carry, finish):
        for fwd, _, _, _, _, kt_scr, qb_scr, dm_scr, s_scr in dirs:
            j = i if fwd else n_pairs - 1 - i
            sl = pl.ds(pl.multiple_of(j * pair, pair), pair)
            s = s_scr[...]
            v = v_ref[sl, :]
            o = _dot(qb_scr[sl, :], s.astype(BF16))
            s_scr[...] = s * jnp.concatenate([dm_scr[j]] * (dv // dk), axis=1) + _dot(kt_scr[j], v)
            if finish:
                y = _rmsnorm(o + o_scr[sl, :], gn_ref[...])
                out_ref[sl, :] = (y * sr_ref[sl, :].astype(F32)).astype(BF16)
            else:
                o_scr[sl, :] = o + _dot(a_scr[sl, :], v)
        return carry

    lax.fori_loop(0, n_pairs // 2, functools.partial(scan_body, finish=False), 0, unroll=GLA_SCAN_UNROLL)
    lax.fori_loop(n_pairs // 2, n_pairs, functools.partial(scan_body, finish=True), 0, unroll=GLA_SCAN_UNROLL)


def _gla_call(q, k, v, sr, gf, gb, kc, vc, gfc, gbc, gn):
    bsz, t, qk = q.shape
    d = v.shape[2]
    tc = kc.shape[1]
    dk = qk // GLA_HEADS
    dv = d // GLA_HEADS
    pair = GLA_KERNEL_CHUNK * GLA_PAIR
    n_pairs = t // pair
    tile = GLA_KERNEL_CHUNK * GLA_GROUP
    seq = lambda rows, width: pl.BlockSpec((None, rows, width), lambda b, h: (b, 0, h))
    scratch_shapes = [
        pltpu.VMEM((t, pair), BF16),
        pltpu.VMEM((t, dv), F32),
        pltpu.VMEM((n_pairs, dk, pair), BF16), pltpu.VMEM((n_pairs, dk, pair), BF16),
        pltpu.VMEM((t, dk), BF16), pltpu.VMEM((t, dk), BF16),
        pltpu.VMEM((n_pairs, dk, dk), F32), pltpu.VMEM((n_pairs, dk, dk), F32),
        pltpu.VMEM((dk, dv), F32), pltpu.VMEM((dk, dv), F32),
    ]
    pipelined = (2 * _nbytes((t, dk), BF16) + 3 * _nbytes((t, dv), BF16) + 2 * _nbytes((t, dk), F32)
                 + _nbytes((tc, dk), BF16) + _nbytes((tc, dv), BF16) + 2 * _nbytes((tc, dk), F32)
                 + _nbytes((1, dv), F32))
    scratch = (_nbytes((t, pair), BF16) + _nbytes((t, dv), F32) + 2 * _nbytes((n_pairs, dk, pair), BF16)
               + 2 * _nbytes((t, dk), BF16) + 2 * _nbytes((n_pairs, dk, dk), F32) + 2 * _nbytes((dk, dv), F32))
    temps = 24 * _nbytes((tile, dv), F32)
    return pl.pallas_call(
        _gla_kernel,
        out_shape=jax.ShapeDtypeStruct((bsz, t, d), BF16),
        grid=(bsz, GLA_HEADS),
        in_specs=[seq(t, dk), seq(t, dk), seq(t, dv), seq(t, dv), seq(t, dk), seq(t, dk),
                  seq(tc, dk), seq(tc, dv), seq(tc, dk), seq(tc, dk),
                  pl.BlockSpec((1, dv), lambda b, h: (0, h))],
        out_specs=seq(t, dv),
        scratch_shapes=scratch_shapes,
        compiler_params=pltpu.CompilerParams(
            dimension_semantics=("parallel", "parallel"),
            vmem_limit_bytes=_vmem_limit(pipelined, scratch, temps),
        ),
        name="gla",
    )(q, k, v, sr, gf, gb, kc, vc, gfc, gbc, gn)


def _out_kernel(x_ref, h1_ref, ya_ref, yb_ref, ga1_ref, sh2_ref, sc2_ref, ga2_ref, g2_ref, gfin_ref,
                wg_ref, bg_ref, wco_ref, wgo_ref, wo_ref, wup_ref, wdn_ref, o_ref, y_scr, h2_scr):
    tm, d = x_ref.shape
    d_ff = wup_ref.shape[1]

    def merge_stage(j):
        cs = slice(j, j + COL_BLOCK)
        h1 = h1_ref[...]
        za = _dot(h1, wg_ref[:, j:j + COL_BLOCK]) + bg_ref[:, j:j + COL_BLOCK]
        zb = _dot(h1, wg_ref[:, d + j:d + j + COL_BLOCK]) + bg_ref[:, d + j:d + j + COL_BLOCK]
        pa = _dot(ya_ref[...], wco_ref[:, cs])
        pb = _dot(yb_ref[...], wgo_ref[:, cs])

        def epilogue():
            y_scr[:, cs] = (_sigmoid(za) * pa + _sigmoid(zb) * pb).astype(BF16)
        return epilogue

    _run_skewed([functools.partial(merge_stage, j) for j in range(0, d, COL_BLOCK)])
    o_ref[...] = x_ref[...] + ga1_ref[...] * _dot(y_scr[...], wo_ref[...])
    h2_scr[...] = _modulate(o_ref[...], g2_ref[...], sh2_ref[...], sc2_ref[...]).astype(BF16)

    parts = []

    def mlp_stage(j):
        u = _dot(h2_scr[...], wup_ref[:, j:j + FF_BLOCK])

        def epilogue():
            a = jnp.maximum(u, 0.0)
            parts.append(_dot((a * a).astype(BF16), wdn_ref[j:j + FF_BLOCK, :]))
        return epilogue

    _run_skewed([functools.partial(mlp_stage, j) for j in range(0, d_ff, FF_BLOCK)])
    x2 = o_ref[...] + ga2_ref[...] * functools.reduce(lambda a, b: a + b, parts)
    o_ref[...] = _rmsnorm(x2, gfin_ref[...])


def _out_call(x, h1, ya, yb, mod3, g2, gfin, wg, bg, wco, wgo, wo, wup, wdn):
    bsz, t, d = x.shape
    tm = OUT_TM
    tok = lambda: pl.BlockSpec((None, tm, d), lambda b, i: (b, i, 0))
    modrow = lambda col: pl.BlockSpec((None, 1, d), lambda b, i: (b, 0, col))
    pipelined = 2 * _nbytes((tm, d), F32) + 3 * _nbytes((tm, d), BF16) + 4 * _nbytes((1, d), F32)
    resident = sum(_nbytes(a.shape, a.dtype) for a in (g2, gfin, wg, bg, wco, wgo, wo, wup, wdn))
    scratch = 2 * _nbytes((tm, d), BF16)
    temps = 10 * _nbytes((tm, d), F32)
    return pl.pallas_call(
        _out_kernel,
        out_shape=jax.ShapeDtypeStruct((bsz, t, d), F32),
        grid=(bsz, t // tm),
        in_specs=[tok(), tok(), tok(), tok()] + [modrow(cidx) for cidx in (2, 3, 4, 5)] + [_resident()] * 9,
        out_specs=tok(),
        scratch_shapes=[pltpu.VMEM((tm, d), BF16), pltpu.VMEM((tm, d), BF16)],
        compiler_params=pltpu.CompilerParams(
            dimension_semantics=("parallel", "parallel"),
            vmem_limit_bytes=_vmem_limit(pipelined, resident + scratch, temps),
        ),
        name="out",
    )(x, h1, ya, yb, mod3, mod3, mod3, mod3, g2, gfin, wg, bg, wco, wgo, wo, wup, wdn)


def kernel(x, c, ctx, c_ctx, w_ada, b_ada, g_norm1, w_in, b_in, conv_w, w_conv_out, w_a2_f, b_a_f,
           w_a2_b, b_a_b, g_gla_norm, w_gla_out, w_o, g_norm2, w_up, w_down, g_final):
    depth = w_ada.shape[0]
    assert depth == 1, "only the single-layer block is implemented"
    bsz, t, d = x.shape
    qk = w_a2_f.shape[2]
    dk = qk // GLA_HEADS
    rank = w_a2_f.shape[1]
    gla_tile = GLA_GROUP * GLA_KERNEL_CHUNK
    assert t % INPROJ_TM == 0 and t % OUT_TM == 0 and INPROJ_TM % GRID_W == 0
    assert GLA_PAIR == 2 and GLA_GROUP % GLA_PAIR == 0 and GLA_PAIR * GLA_KERNEL_CHUNK == dk
    gla_pair = GLA_PAIR * GLA_KERNEL_CHUNK
    assert t % (GLA_LOCAL_UNROLL * gla_tile) == 0 and ctx.shape[1] % gla_tile == 0
    assert t % (2 * GLA_SCAN_UNROLL * gla_pair) == 0
    assert bsz + 1 <= MOD_ROWS and 2 * rank <= LR_PAD

    cc = jnp.zeros((MOD_ROWS, d), F32).at[:bsz].set(c).at[bsz].set(c_ctx)
    mod = _ada_call(cc, w_ada[0], b_ada[0][None, :])
    mod3 = mod.reshape(MOD_ROWS, 1, N_MOD * d)

    w = w_in[0]
    bias = b_in[0][None, :]
    o_q = 3 * d
    o_k = o_q + qk
    o_v = o_k + qk
    o_r = o_v + d
    o_lr = o_r + d
    o_g = o_lr + 2 * rank
    lr_pad = LR_PAD - 2 * rank
    w1 = w[:, :o_q].astype(BF16)
    b1 = bias[:, :o_q]
    w_lr = jnp.pad(w[:, o_lr:o_g], ((0, 0), (0, lr_pad)))
    b_lr = jnp.pad(bias[:, o_lr:o_g], ((0, 0), (0, lr_pad)))
    w2 = jnp.concatenate([w[:, o_q:o_lr], w_lr], axis=1).astype(BF16)
    b2 = jnp.concatenate([bias[:, o_q:o_lr], b_lr], axis=1)
    w2c = jnp.concatenate([w[:, o_k:o_r], w_lr], axis=1).astype(BF16)
    b2c = jnp.concatenate([bias[:, o_k:o_r], b_lr], axis=1)
    wg = w[:, o_g:].astype(BF16)
    bg = bias[:, o_g:]
    wa2 = jnp.zeros((LR_PAD, 2 * qk), F32)
    wa2 = wa2.at[:rank, :qk].set(w_a2_f[0]).at[rank:2 * rank, qk:].set(w_a2_b[0]).astype(BF16)
    ba2 = jnp.concatenate([b_a_f[0], b_a_b[0]])[None, :]
    g1 = g_norm1[0][None, :]

    h1, ya, q, k, v, sr, gf, gb = _inproj_call(x, mod3, g1, w1, b1, conv_w[0], w2, b2, wa2, ba2, qk, dk)
    kc, vc, gfc, gbc = _inproj_ctx_call(ctx, mod3, bsz, g1, w2c, b2c, wa2, ba2, qk)
    yb = _gla_call(q, k, v, sr, gf, gb, kc, vc, gfc, gbc, g_gla_norm[0][None, :])
    return _out_call(x, h1, ya, yb, mod3, g_norm2[0][None, :], g_final[None, :], wg, bg,
                     w_conv_out[0].astype(BF16), w_gla_out[0].astype(BF16), w_o[0].astype(BF16),
                     w_up[0].astype(BF16), w_down[0].astype(BF16))
```

```python
import functools

import jax
import jax.numpy as jnp
from jax import lax
from jax.experimental import pallas as pl
from jax.experimental.pallas import tpu as pltpu

F32 = jnp.float32
BF16 = jnp.bfloat16
HIGHEST = lax.Precision.HIGHEST

GLA_HEADS = 4
GLA_TAU = 16.0
GRID_W = 64
N_MOD = 6
RMS_EPS = 1e-6
GLA_RANK = 16
LOG2_E = 1.4426950408889634

LR_PAD = 128
GLA_KERNEL_CHUNK = 64
GLA_PAIR = 2
GLA_GROUP = 4
GLA_LOCAL_UNROLL = 4
GLA_SCAN_UNROLL = 8
INPROJ_TM = 1024
OUT_TM = 512
COL_BLOCK = 256
FF_BLOCK = 1024
MOD_ROWS = 16
V7X_VMEM_LIMIT_BYTES = 60000 * 1024


def _vmem_limit(pipelined_bytes, resident_bytes, temp_bytes):
    need = 2 * pipelined_bytes + resident_bytes + temp_bytes
    return int(min(V7X_VMEM_LIMIT_BYTES, need))


def _nbytes(shape, dtype):
    n = 1
    for s in shape:
        n *= s
    return n * jnp.dtype(dtype).itemsize


def _dot(a, b):
    return jnp.dot(a, b, preferred_element_type=F32)


def _nt_dot(a, b):
    return lax.dot_general(a, b, (((1,), (1,)), ((), ())), preferred_element_type=F32)


def _run_skewed(stages):
    pending = None
    for stage in stages:
        epilogue = stage()
        if pending is not None:
            pending()
        pending = epilogue
    if pending is not None:
        pending()


def _sigmoid(x):
    return 1.0 / (1.0 + jnp.exp(-x))


def _log_sigmoid(x):
    return jnp.minimum(x, 0.0) - jnp.log(1.0 + jnp.exp(-jnp.abs(x)))


def _rmsnorm(x, g):
    return x * lax.rsqrt(jnp.mean(x * x, axis=-1, keepdims=True) + RMS_EPS) * g


def _modulate(x, g, shift, scale):
    return _rmsnorm(x, g) * (1.0 + scale) + shift


def _ada_kernel(c_ref, w_ref, b_ref, o_ref):
    c = c_ref[...]
    s = c * _sigmoid(c)
    o_ref[...] = jnp.dot(s, w_ref[...], precision=HIGHEST, preferred_element_type=F32) + b_ref[...]


def _ada_call(cc, w_ada, b_ada):
    d = cc.shape[1]
    n_out = w_ada.shape[1]
    return pl.pallas_call(
        _ada_kernel,
        out_shape=jax.ShapeDtypeStruct((MOD_ROWS, n_out), F32),
        grid=(n_out // d,),
        in_specs=[
            pl.BlockSpec((MOD_ROWS, d), lambda j: (0, 0)),
            pl.BlockSpec((d, d), lambda j: (0, j)),
            pl.BlockSpec((1, d), lambda j: (0, j)),
        ],
        out_specs=pl.BlockSpec((MOD_ROWS, d), lambda j: (0, j)),
        compiler_params=pltpu.CompilerParams(
            dimension_semantics=("arbitrary",),
            vmem_limit_bytes=_vmem_limit(_nbytes((d, d), F32) + _nbytes((MOD_ROWS, 2 * d), F32),
                                         0, 4 * _nbytes((d, d), F32)),
        ),
        name="ada",
    )(cc, w_ada, b_ada)


def _log_decays(lr, wa2_ref, ba2_ref):
    xg = _dot(lr.astype(BF16), wa2_ref[...]) + ba2_ref[...]
    return _log_sigmoid(xg) * (LOG2_E / GLA_TAU)


def _inproj_kernel(x_ref, sh_ref, sc_ref, g1_ref, w1_ref, b1_ref, cw_ref, w2_ref, b2_ref, wa2_ref, ba2_ref,
                   h_ref, ya_ref, q_ref, k_ref, v_ref, sr_ref, gf_ref, gb_ref, *, q_scale):
    tm, d = x_ref.shape
    qk = q_ref.shape[1]
    h_ref[...] = _modulate(x_ref[...], g1_ref[...], sh_ref[...], sc_ref[...]).astype(BF16)

    col_in_row = lax.broadcasted_iota(jnp.int32, (tm, 1), 0) % GRID_W
    has_left = col_in_row != 0
    has_right = col_in_row != GRID_W - 1

    def proj(w_ref, b_ref, c0, width):
        return _dot(h_ref[...], w_ref[:, c0:c0 + width]) + b_ref[:, c0:c0 + width]

    def decay_stage():
        lr = proj(w2_ref, b2_ref, 2 * qk + 2 * d, LR_PAD)

        def epilogue():
            g = _log_decays(lr, wa2_ref, ba2_ref)
            gf_ref[...] = g[:, 0:qk]
            gb_ref[...] = g[:, qk:2 * qk]
        return epilogue

    def conv_stage(j):
        cs = slice(j, j + COL_BLOCK)
        xa = proj(w1_ref, b1_ref, j, COL_BLOCK)
        ba = proj(w1_ref, b1_ref, d + j, COL_BLOCK)
        ca = proj(w1_ref, b1_ref, 2 * d + j, COL_BLOCK)

        def epilogue():
            u = ca * xa
            left = jnp.where(has_left, pltpu.roll(u, 1, 0), 0.0)
            right = jnp.where(has_right, pltpu.roll(u, tm - 1, 0), 0.0)
            y = left * cw_ref[0:1, cs] + u * cw_ref[1:2, cs] + right * cw_ref[2:3, cs]
            ya_ref[:, cs] = (ba * y).astype(BF16)
        return epilogue

    def swish_stage(j):
        r = proj(w2_ref, b2_ref, 2 * qk + d + j, COL_BLOCK)

        def epilogue():
            sr_ref[:, j:j + COL_BLOCK] = (r * _sigmoid(r)).astype(BF16)
        return epilogue

    def qk_stage():
        q = proj(w2_ref, b2_ref, 0, qk)
        k = proj(w2_ref, b2_ref, qk, qk)

        def epilogue():
            q_ref[...] = (q * q_scale).astype(BF16)
            k_ref[...] = k.astype(BF16)
        return epilogue

    def value_stage(j):
        v = proj(w2_ref, b2_ref, 2 * qk + j, COL_BLOCK)

        def epilogue():
            v_ref[:, j:j + COL_BLOCK] = v.astype(BF16)
        return epilogue

    _run_skewed([decay_stage] + [functools.partial(conv_stage, j) for j in range(0, d, COL_BLOCK)]
                + [functools.partial(swish_stage, j) for j in range(0, d, COL_BLOCK)] + [qk_stage]
                + [functools.partial(value_stage, j) for j in range(0, d, COL_BLOCK)])


def _inproj_ctx_kernel(x_ref, sh_ref, sc_ref, g1_ref, w_ref, b_ref, wa2_ref, ba2_ref,
                       k_ref, v_ref, gf_ref, gb_ref):
    d = x_ref.shape[1]
    qk = k_ref.shape[1]
    h = _modulate(x_ref[...], g1_ref[...], sh_ref[...], sc_ref[...]).astype(BF16)
    k_ref[...] = (_dot(h, w_ref[:, 0:qk]) + b_ref[:, 0:qk]).astype(BF16)
    for j in range(0, d, COL_BLOCK):
        c0 = qk + j
        v_ref[:, j:j + COL_BLOCK] = (_dot(h, w_ref[:, c0:c0 + COL_BLOCK]) + b_ref[:, c0:c0 + COL_BLOCK]).astype(BF16)
    c0 = qk + d
    lr = _dot(h, w_ref[:, c0:c0 + LR_PAD]) + b_ref[:, c0:c0 + LR_PAD]
    g = _log_decays(lr, wa2_ref, ba2_ref)
    gf_ref[...] = g[:, 0:qk]
    gb_ref[...] = g[:, qk:2 * qk]


def _resident():
    return pl.BlockSpec(memory_space=pltpu.VMEM)


def _inproj_call(x, mod3, g1, w1, b1, cw, w2, b2, wa2, ba2, qk, dk):
    bsz, t, d = x.shape
    tm = INPROJ_TM
    tok = lambda n: pl.BlockSpec((None, tm, n), lambda b, i: (b, i, 0))
    modrow = lambda col: pl.BlockSpec((None, 1, d), lambda b, i: (b, 0, col))
    out_shapes = (
        jax.ShapeDtypeStruct((bsz, t, d), BF16),
        jax.ShapeDtypeStruct((bsz, t, d), BF16),
        jax.ShapeDtypeStruct((bsz, t, qk), BF16),
        jax.ShapeDtypeStruct((bsz, t, qk), BF16),
        jax.ShapeDtypeStruct((bsz, t, d), BF16),
        jax.ShapeDtypeStruct((bsz, t, d), BF16),
        jax.ShapeDtypeStruct((bsz, t, qk), F32),
        jax.ShapeDtypeStruct((bsz, t, qk), F32),
    )
    pipelined = (_nbytes((tm, d), F32) + 4 * _nbytes((tm, d), BF16) + 2 * _nbytes((tm, qk), BF16)
                 + 2 * _nbytes((tm, qk), F32) + 2 * _nbytes((1, d), F32))
    resident = sum(_nbytes(a.shape, a.dtype) for a in (g1, w1, b1, cw, w2, b2, wa2, ba2))
    temps = 24 * _nbytes((tm, COL_BLOCK), F32) + 3 * _nbytes((tm, 2 * qk), F32)
    return pl.pallas_call(
        functools.partial(_inproj_kernel, q_scale=dk ** -0.5),
        out_shape=out_shapes,
        grid=(bsz, t // tm),
        in_specs=[tok(d), modrow(0), modrow(1)] + [_resident()] * 8,
        out_specs=(tok(d), tok(d), tok(qk), tok(qk), tok(d), tok(d), tok(qk), tok(qk)),
        compiler_params=pltpu.CompilerParams(
            dimension_semantics=("parallel", "parallel"),
            vmem_limit_bytes=_vmem_limit(pipelined, resident, temps),
        ),
        name="inproj",
    )(x, mod3, mod3, g1, w1, b1, cw, w2, b2, wa2, ba2)


def _inproj_ctx_call(ctx, mod3, ctx_row, g1, w, b, wa2, ba2, qk):
    bsz, tc, d = ctx.shape
    tok = lambda n: pl.BlockSpec((None, tc, n), lambda b: (b, 0, 0))
    modrow = lambda col: pl.BlockSpec((None, 1, d), lambda b: (ctx_row, 0, col))
    out_shapes = (
        jax.ShapeDtypeStruct((bsz, tc, qk), BF16),
        jax.ShapeDtypeStruct((bsz, tc, d), BF16),
        jax.ShapeDtypeStruct((bsz, tc, qk), F32),
        jax.ShapeDtypeStruct((bsz, tc, qk), F32),
    )
    pipelined = (_nbytes((tc, d), F32) + _nbytes((tc, d), BF16) + _nbytes((tc, qk), BF16)
                 + 2 * _nbytes((tc, qk), F32) + 2 * _nbytes((1, d), F32))
    resident = sum(_nbytes(a.shape, a.dtype) for a in (g1, w, b, wa2, ba2))
    temps = _nbytes((tc, d), BF16) + 6 * _nbytes((tc, COL_BLOCK), F32) + 3 * _nbytes((tc, 2 * qk), F32)
    return pl.pallas_call(
        _inproj_ctx_kernel,
        out_shape=out_shapes,
        grid=(bsz,),
        in_specs=[tok(d), modrow(0), modrow(1)] + [_resident()] * 5,
        out_specs=(tok(qk), tok(d), tok(qk), tok(qk)),
        compiler_params=pltpu.CompilerParams(
            dimension_semantics=("parallel",),
            vmem_limit_bytes=_vmem_limit(pipelined, resident, temps),
        ),
        name="inproj_ctx",
    )(ctx, mod3, mod3, g1, w, b, wa2, ba2)


def _tile_decays(g, tri, fwd, n_chunks):
    rows, dk = g.shape
    c = rows // n_chunks
    g_hi = g.astype(BF16)
    g_lo = (g - g_hi.astype(F32)).astype(BF16)
    bb = _dot(tri, jnp.concatenate([g_hi, g_lo], axis=1))
    b = (bb[:, :dk] + bb[:, dk:]).reshape(n_chunks, c, dk)
    if fwd:
        return b, b[:, c - 1:c, :], b[:, c // 2 - 1:c // 2, :]
    return b, b[:, 0:1, :], b[:, c // 2:c // 2 + 1, :]


def _pair_offsets(tot, fwd):
    n_chunks = tot.shape[0]
    zero = jnp.zeros_like(tot[0:1])
    on_odd = jnp.concatenate([t for p in range(0, n_chunks, GLA_PAIR) for t in (zero, tot[p:p + 1])], axis=0)
    on_even = jnp.concatenate([t for p in range(0, n_chunks, GLA_PAIR) for t in (tot[p + 1:p + 2], zero)], axis=0)
    pair_tot = jnp.concatenate([tot[p:p + 1] + tot[p + 1:p + 2] for p in range(0, n_chunks, GLA_PAIR)], axis=0)
    return (on_odd, on_even, pair_tot) if fwd else (on_even, on_odd, pair_tot)


def _lane_broadcast_column(row):
    n = row.shape[1]
    return jnp.broadcast_to(row, (n, n)).T


def _gla_kernel(q_ref, k_ref, v_ref, sr_ref, gf_ref, gb_ref, kc_ref, vc_ref, gfc_ref, gbc_ref, gn_ref,
                out_ref, a_scr, o_scr, ktf_scr, ktb_scr, qbf_scr, qbb_scr, dmf_scr, dmb_scr, sf_scr, sb_scr):
    c = GLA_KERNEL_CHUNK
    grp = GLA_GROUP
    tile = c * grp
    pair = c * GLA_PAIR
    pairs_per_tile = grp // GLA_PAIR
    t, dk = q_ref.shape
    dv = v_ref.shape[1]
    n_pairs = t // pair
    row = lax.broadcasted_iota(jnp.int32, (tile, tile), 0)
    col = lax.broadcasted_iota(jnp.int32, (tile, tile), 1)
    row_chunk = row // c
    col_chunk = col // c
    same_pair = (row // pair) == (col // pair)
    inside_f = (row_chunk == col_chunk) & (row >= col)
    inside_b = (row_chunk == col_chunk) & (row <= col)
    across_f = same_pair & (row_chunk == col_chunk + 1)
    across_b = same_pair & (row_chunk == col_chunk - 1)
    dirs = ((True, gf_ref, gfc_ref, inside_f, across_f, ktf_scr, qbf_scr, dmf_scr, sf_scr),
            (False, gb_ref, gbc_ref, inside_b, across_b, ktb_scr, qbb_scr, dmb_scr, sb_scr))

    def key_side(k3, b, tot, k_off):
        kt = k3 * jnp.exp2(tot - b)
        return kt, kt * jnp.exp2(k_off)

    ctx_tiles = kc_ref.shape[0] // tile
    ctx_items = [(d, ti) for d in dirs for ti in range(ctx_tiles)]
    ctx_cums = [_tile_decays(d[2][ti * tile:(ti + 1) * tile, :], d[3].astype(BF16), d[0], grp)
                for d, ti in ctx_items]
    ctx_terms = {}
    for (d, ti), (b, tot, _) in zip(ctx_items, ctx_cums):
        _, k_off, pair_tot = _pair_offsets(tot, d[0])
        kc3 = kc_ref[ti * tile:(ti + 1) * tile, :].astype(F32).reshape(grp, c, dk)
        _, kt_pair = key_side(kc3, b, tot, k_off)
        kt_t = kt_pair.reshape(tile, dk).T.astype(BF16)
        decay = jnp.exp2(pair_tot)
        for p in range(pairs_per_tile):
            r0 = ti * tile + p * pair
            ctx_terms[(d[0], ti * pairs_per_tile + p)] = (
                _dot(kt_t[:, p * pair:(p + 1) * pair], vc_ref[r0:r0 + pair, :]),
                _lane_broadcast_column(decay[p]))
    for d in dirs:
        order = range(ctx_tiles * pairs_per_tile)
        s = jnp.zeros((dk, dv), F32)
        for p in (order if d[0] else reversed(order)):
            inc, dm = ctx_terms[(d[0], p)]
            s = s * jnp.concatenate([dm] * (dv // dk), axis=1) + inc
        d[8][...] = s

    def local_body(i, carry):
        tiles = [i * GLA_LOCAL_UNROLL + u for u in range(GLA_LOCAL_UNROLL)]
        rows = [pl.ds(pl.multiple_of(ti * tile, tile), tile) for ti in tiles]
        items = [(u, d) for u in range(GLA_LOCAL_UNROLL) for d in dirs]
        cums = [_tile_decays(d[1][rows[u], :], d[3].astype(BF16), d[0], grp) for u, d in items]
        q3 = [q_ref[sl, :].astype(F32).reshape(grp, c, dk) for sl in rows]
        k3 = [k_ref[sl, :].astype(F32).reshape(grp, c, dk) for sl in rows]
        products = []
        for (u, d), (b, tot, bm) in zip(items, cums):
            fwd, _, _, _, _, kt_scr, qb_scr, dm_scr, _ = d
            q_off, k_off, pair_tot = _pair_offsets(tot, fwd)
            qh = (q3[u] * jnp.exp2(b - bm)).reshape(tile, dk).astype(BF16)
            kh = (k3[u] * jnp.exp2(bm - b)).reshape(tile, dk).astype(BF16)
            qb = q3[u] * jnp.exp2(b)
            kt, kt_pair = key_side(k3[u], b, tot, k_off)
            products.append((_nt_dot(qh, kh),
                             _nt_dot(qb.reshape(tile, dk).astype(BF16), kt.reshape(tile, dk).astype(BF16))))
            qb_scr[rows[u], :] = (qb * jnp.exp2(q_off)).reshape(tile, dk).astype(BF16)
            kt_t = kt_pair.reshape(tile, dk).T.astype(BF16)
            decay = jnp.exp2(pair_tot)
            for p in range(pairs_per_tile):
                kt_scr[tiles[u] * pairs_per_tile + p] = kt_t[:, p * pair:(p + 1) * pair]
                dm_scr[tiles[u] * pairs_per_tile + p] = _lane_broadcast_column(decay[p])
        for u in range(GLA_LOCAL_UNROLL):
            scores = jnp.zeros((tile, tile), F32)
            for (iu, d), (inner, outer) in zip(items, products):
                if iu == u:
                    scores = scores + jnp.where(d[3], inner, 0.0) + jnp.where(d[4], outer, 0.0)
            scores = scores.astype(BF16)
            for p in range(pairs_per_tile):
                a_scr[pl.ds(pl.multiple_of(tiles[u] * tile + p * pair, pair), pair), :] = (
                    scores[p * pair:(p + 1) * pair, p * pair:(p + 1) * pair])
        return carry

    lax.fori_loop(0, t // (tile * GLA_LOCAL_UNROLL), local_body, 0)

    def scan_body(i, carry, finish):
        steps = []
        for u in range(GLA_SCAN_UNROLL):
            for d in dirs:
                step = i * GLA_SCAN_UNROLL + u
                j = step if d[0] else n_pairs - 1 - step
                steps.append((d, j, pl.ds(pl.multiple_of(j * pair, pair), pair)))
        increments = [_dot(d[5][j], v_ref[sl, :]) for d, j, sl in steps]
        local = [None if finish else _dot(a_scr[sl, :], v_ref[sl, :]) for _, _, sl in steps]
        outs = []
        for (d, j, sl), inc in zip(steps, increments):
            qb_scr, dm_scr, s_scr = d[6], d[7], d[8]
            s = s_scr[...]
            outs.append(_dot(qb_scr[sl, :], s.astype(BF16)))
            s_scr[...] = s * jnp.concatenate([dm_scr[j]] * (dv // dk), axis=1) + inc
        for (_, _, sl), o, loc in zip(steps, outs, local):
            if finish:
                y = _rmsnorm(o + o_scr[sl, :], gn_ref[...])
                out_ref[sl, :] = (y * sr_ref[sl, :].astype(F32)).astype(BF16)
            else:
                o_scr[sl, :] = o + loc
        return carry

    half = n_pairs // (2 * GLA_SCAN_UNROLL)
    lax.fori_loop(0, half, functools.partial(scan_body, finish=False), 0)
    lax.fori_loop(half, 2 * half, functools.partial(scan_body, finish=True), 0)


def _gla_call(q, k, v, sr, gf, gb, kc, vc, gfc, gbc, gn):
    bsz, t, qk = q.shape
    d = v.shape[2]
    tc = kc.shape[1]
    dk = qk // GLA_HEADS
    dv = d // GLA_HEADS
    pair = GLA_KERNEL_CHUNK * GLA_PAIR
    n_pairs = t // pair
    tile = GLA_KERNEL_CHUNK * GLA_GROUP
    seq = lambda rows, width: pl.BlockSpec((None, rows, width), lambda b, h: (b, 0, h))
    scratch_shapes = [
        pltpu.VMEM((t, pair), BF16),
        pltpu.VMEM((t, dv), F32),
        pltpu.VMEM((n_pairs, dk, pair), BF16), pltpu.VMEM((n_pairs, dk, pair), BF16),
        pltpu.VMEM((t, dk), BF16), pltpu.VMEM((t, dk), BF16),
        pltpu.VMEM((n_pairs, dk, dk), F32), pltpu.VMEM((n_pairs, dk, dk), F32),
        pltpu.VMEM((dk, dv), F32), pltpu.VMEM((dk, dv), F32),
    ]
    pipelined = (2 * _nbytes((t, dk), BF16) + 3 * _nbytes((t, dv), BF16) + 2 * _nbytes((t, dk), F32)
                 + _nbytes((tc, dk), BF16) + _nbytes((tc, dv), BF16) + 2 * _nbytes((tc, dk), F32)
                 + _nbytes((1, dv), F32))
    scratch = (_nbytes((t, pair), BF16) + _nbytes((t, dv), F32) + 2 * _nbytes((n_pairs, dk, pair), BF16)
               + 2 * _nbytes((t, dk), BF16) + 2 * _nbytes((n_pairs, dk, dk), F32) + 2 * _nbytes((dk, dv), F32))
    temps = 24 * _nbytes((tile, dv), F32)
    return pl.pallas_call(
        _gla_kernel,
        out_shape=jax.ShapeDtypeStruct((bsz, t, d), BF16),
        grid=(bsz, GLA_HEADS),
        in_specs=[seq(t, dk), seq(t, dk), seq(t, dv), seq(t, dv), seq(t, dk), seq(t, dk),
                  seq(tc, dk), seq(tc, dv), seq(tc, dk), seq(tc, dk),
                  pl.BlockSpec((1, dv), lambda b, h: (0, h))],
        out_specs=seq(t, dv),
        scratch_shapes=scratch_shapes,
        compiler_params=pltpu.CompilerParams(
            dimension_semantics=("parallel", "parallel"),
            vmem_limit_bytes=_vmem_limit(pipelined, scratch, temps),
        ),
        name="gla",
    )(q, k, v, sr, gf, gb, kc, vc, gfc, gbc, gn)


def _out_kernel(x_ref, h1_ref, ya_ref, yb_ref, ga1_ref, sh2_ref, sc2_ref, ga2_ref, g2_ref, gfin_ref,
                wg_ref, bg_ref, wco_ref, wgo_ref, wo_ref, wup_ref, wdn_ref, o_ref, y_scr, h2_scr):
    tm, d = x_ref.shape
    d_ff = wup_ref.shape[1]

    def merge_stage(j):
        cs = slice(j, j + COL_BLOCK)
        h1 = h1_ref[...]
        za = _dot(h1, wg_ref[:, j:j + COL_BLOCK]) + bg_ref[:, j:j + COL_BLOCK]
        zb = _dot(h1, wg_ref[:, d + j:d + j + COL_BLOCK]) + bg_ref[:, d + j:d + j + COL_BLOCK]
        pa = _dot(ya_ref[...], wco_ref[:, cs])
        pb = _dot(yb_ref[...], wgo_ref[:, cs])

        def epilogue():
            y_scr[:, cs] = (_sigmoid(za) * pa + _sigmoid(zb) * pb).astype(BF16)
        return epilogue

    _run_skewed([functools.partial(merge_stage, j) for j in range(0, d, COL_BLOCK)])
    o_ref[...] = x_ref[...] + ga1_ref[...] * _dot(y_scr[...], wo_ref[...])
    h2_scr[...] = _modulate(o_ref[...], g2_ref[...], sh2_ref[...], sc2_ref[...]).astype(BF16)

    parts = []

    def mlp_stage(j):
        u = _dot(h2_scr[...], wup_ref[:, j:j + FF_BLOCK])

        def epilogue():
            a = jnp.maximum(u, 0.0)
            parts.append(_dot((a * a).astype(BF16), wdn_ref[j:j + FF_BLOCK, :]))
        return epilogue

    _run_skewed([functools.partial(mlp_stage, j) for j in range(0, d_ff, FF_BLOCK)])
    x2 = o_ref[...] + ga2_ref[...] * functools.reduce(lambda a, b: a + b, parts)
    o_ref[...] = _rmsnorm(x2, gfin_ref[...])


def _out_call(x, h1, ya, yb, mod3, g2, gfin, wg, bg, wco, wgo, wo, wup, wdn):
    bsz, t, d = x.shape
    tm = OUT_TM
    tok = lambda: pl.BlockSpec((None, tm, d), lambda b, i: (b, i, 0))
    modrow = lambda col: pl.BlockSpec((None, 1, d), lambda b, i: (b, 0, col))
    pipelined = 2 * _nbytes((tm, d), F32) + 3 * _nbytes((tm, d), BF16) + 4 * _nbytes((1, d), F32)
    resident = sum(_nbytes(a.shape, a.dtype) for a in (g2, gfin, wg, bg, wco, wgo, wo, wup, wdn))
    scratch = 2 * _nbytes((tm, d), BF16)
    temps = 10 * _nbytes((tm, d), F32)
    return pl.pallas_call(
        _out_kernel,
        out_shape=jax.ShapeDtypeStruct((bsz, t, d), F32),
        grid=(bsz, t // tm),
        in_specs=[tok(), tok(), tok(), tok()] + [modrow(cidx) for cidx in (2, 3, 4, 5)] + [_resident()] * 9,
        out_specs=tok(),
        scratch_shapes=[pltpu.VMEM((tm, d), BF16), pltpu.VMEM((tm, d), BF16)],
        compiler_params=pltpu.CompilerParams(
            dimension_semantics=("parallel", "parallel"),
            vmem_limit_bytes=_vmem_limit(pipelined, resident + scratch, temps),
        ),
        name="out",
    )(x, h1, ya, yb, mod3, mod3, mod3, mod3, g2, gfin, wg, bg, wco, wgo, wo, wup, wdn)


def kernel(x, c, ctx, c_ctx, w_ada, b_ada, g_norm1, w_in, b_in, conv_w, w_conv_out, w_a2_f, b_a_f,
           w_a2_b, b_a_b, g_gla_norm, w_gla_out, w_o, g_norm2, w_up, w_down, g_final):
    depth = w_ada.shape[0]
    assert depth == 1, "only the single-layer block is implemented"
    bsz, t, d = x.shape
    qk = w_a2_f.shape[2]
    dk = qk // GLA_HEADS
    rank = w_a2_f.shape[1]
    gla_tile = GLA_GROUP * GLA_KERNEL_CHUNK
    assert t % INPROJ_TM == 0 and t % OUT_TM == 0 and INPROJ_TM % GRID_W == 0
    assert GLA_PAIR == 2 and GLA_GROUP % GLA_PAIR == 0 and GLA_PAIR * GLA_KERNEL_CHUNK == dk
    gla_pair = GLA_PAIR * GLA_KERNEL_CHUNK
    assert t % (GLA_LOCAL_UNROLL * gla_tile) == 0 and ctx.shape[1] % gla_tile == 0
    assert t % (2 * GLA_SCAN_UNROLL * gla_pair) == 0
    assert bsz + 1 <= MOD_ROWS and 2 * rank <= LR_PAD

    cc = jnp.zeros((MOD_ROWS, d), F32).at[:bsz].set(c).at[bsz].set(c_ctx)
    mod = _ada_call(cc, w_ada[0], b_ada[0][None, :])
    mod3 = mod.reshape(MOD_ROWS, 1, N_MOD * d)

    w = w_in[0]
    bias = b_in[0][None, :]
    o_q = 3 * d
    o_k = o_q + qk
    o_v = o_k + qk
    o_r = o_v + d
    o_lr = o_r + d
    o_g = o_lr + 2 * rank
    lr_pad = LR_PAD - 2 * rank
    w1 = w[:, :o_q].astype(BF16)
    b1 = bias[:, :o_q]
    w_lr = jnp.pad(w[:, o_lr:o_g], ((0, 0), (0, lr_pad)))
    b_lr = jnp.pad(bias[:, o_lr:o_g], ((0, 0), (0, lr_pad)))
    w2 = jnp.concatenate([w[:, o_q:o_lr], w_lr], axis=1).astype(BF16)
    b2 = jnp.concatenate([bias[:, o_q:o_lr], b_lr], axis=1)
    w2c = jnp.concatenate([w[:, o_k:o_r], w_lr], axis=1).astype(BF16)
    b2c = jnp.concatenate([bias[:, o_k:o_r], b_lr], axis=1)
    wg = w[:, o_g:].astype(BF16)
    bg = bias[:, o_g:]
    wa2 = jnp.zeros((LR_PAD, 2 * qk), F32)
    wa2 = wa2.at[:rank, :qk].set(w_a2_f[0]).at[rank:2 * rank, qk:].set(w_a2_b[0]).astype(BF16)
    ba2 = jnp.concatenate([b_a_f[0], b_a_b[0]])[None, :]
    g1 = g_norm1[0][None, :]

    h1, ya, q, k, v, sr, gf, gb = _inproj_call(x, mod3, g1, w1, b1, conv_w[0], w2, b2, wa2, ba2, qk, dk)
    kc, vc, gfc, gbc = _inproj_ctx_call(ctx, mod3, bsz, g1, w2c, b2c, wa2, ba2, qk)
    yb = _gla_call(q, k, v, sr, gf, gb, kc, vc, gfc, gbc, g_gla_norm[0][None, :])
    return _out_call(x, h1, ya, yb, mod3, g_norm2[0][None, :], g_final[None, :], wg, bg,
                     w_conv_out[0].astype(BF16), w_gla_out[0].astype(BF16), w_o[0].astype(BF16),
                     w_up[0].astype(BF16), w_down[0].astype(BF16))
```

```python
import functools
from typing import NamedTuple

import jax
import jax.numpy as jnp
from jax import lax
from jax.experimental import pallas as pl
from jax.experimental.pallas import tpu as pltpu

F32 = jnp.float32
BF16 = jnp.bfloat16
HIGHEST = lax.Precision.HIGHEST

GLA_HEADS = 4
GLA_TAU = 16.0
GRID_W = 64
N_MOD = 6
RMS_EPS = 1e-6
GLA_RANK = 16
LOG2_E = 1.4426950408889634

LR_PAD = 128
GLA_KERNEL_CHUNK = 64
GLA_DIAG_BLOCK = 8
GLA_DIAG_DECAY_FLOOR = -31.0
GLA_PAIR = 2
GLA_GROUP = 4
GLA_LOCAL_UNROLL = 4
GLA_SCAN_UNROLL = 8
INPROJ_TM = 1024
OUT_TM = 512
COL_BLOCK = 256
FF_BLOCK = 1024
MOD_ROWS = 16
V7X_VMEM_LIMIT_BYTES = 60000 * 1024


def _vmem_limit(pipelined_bytes, resident_bytes, temp_bytes):
    need = 2 * pipelined_bytes + resident_bytes + temp_bytes
    return int(min(V7X_VMEM_LIMIT_BYTES, need))


def _nbytes(shape, dtype):
    n = 1
    for s in shape:
        n *= s
    return n * jnp.dtype(dtype).itemsize


def _dot(a, b):
    return jnp.dot(a, b, preferred_element_type=F32)


def _nt_dot(a, b):
    return lax.dot_general(a, b, (((1,), (1,)), ((), ())), preferred_element_type=F32)


def _run_skewed(stages):
    pending = None
    for stage in stages:
        epilogue = stage()
        if pending is not None:
            pending()
        pending = epilogue
    if pending is not None:
        pending()


def _sigmoid(x):
    return 1.0 / (1.0 + jnp.exp(-x))


def _log_sigmoid(x):
    return jnp.minimum(x, 0.0) - jnp.log(1.0 + jnp.exp(-jnp.abs(x)))


def _rmsnorm(x, g):
    return x * lax.rsqrt(jnp.mean(x * x, axis=-1, keepdims=True) + RMS_EPS) * g


def _modulate(x, g, shift, scale):
    return _rmsnorm(x, g) * (1.0 + scale) + shift


def _ada_kernel(c_ref, w_ref, b_ref, o_ref):
    c = c_ref[...]
    s = c * _sigmoid(c)
    o_ref[...] = jnp.dot(s, w_ref[...], precision=HIGHEST, preferred_element_type=F32) + b_ref[...]


def _ada_call(cc, w_ada, b_ada):
    d = cc.shape[1]
    n_out = w_ada.shape[1]
    return pl.pallas_call(
        _ada_kernel,
        out_shape=jax.ShapeDtypeStruct((MOD_ROWS, n_out), F32),
        grid=(n_out // d,),
        in_specs=[
            pl.BlockSpec((MOD_ROWS, d), lambda j: (0, 0)),
            pl.BlockSpec((d, d), lambda j: (0, j)),
            pl.BlockSpec((1, d), lambda j: (0, j)),
        ],
        out_specs=pl.BlockSpec((MOD_ROWS, d), lambda j: (0, j)),
        compiler_params=pltpu.CompilerParams(
            dimension_semantics=("arbitrary",),
            vmem_limit_bytes=_vmem_limit(_nbytes((d, d), F32) + _nbytes((MOD_ROWS, 2 * d), F32),
                                         0, 4 * _nbytes((d, d), F32)),
        ),
        name="ada",
    )(cc, w_ada, b_ada)


def _log_decays(lr, wa2_ref, ba2_ref):
    xg = _dot(lr.astype(BF16), wa2_ref[...]) + ba2_ref[...]
    return _log_sigmoid(xg) * (LOG2_E / GLA_TAU)


def _inproj_kernel(x_ref, sh_ref, sc_ref, g1_ref, w1_ref, b1_ref, cw_ref, w2_ref, b2_ref, wa2_ref, ba2_ref,
                   h_ref, ya_ref, q_ref, k_ref, v_ref, sr_ref, gf_ref, gb_ref, *, q_scale):
    tm, d = x_ref.shape
    qk = q_ref.shape[1]
    h_ref[...] = _modulate(x_ref[...], g1_ref[...], sh_ref[...], sc_ref[...]).astype(BF16)

    col_in_row = lax.broadcasted_iota(jnp.int32, (tm, 1), 0) % GRID_W
    has_left = col_in_row != 0
    has_right = col_in_row != GRID_W - 1

    def proj(w_ref, b_ref, c0, width):
        return _dot(h_ref[...], w_ref[:, c0:c0 + width]) + b_ref[:, c0:c0 + width]

    def decay_stage():
        lr = proj(w2_ref, b2_ref, 2 * qk + 2 * d, LR_PAD)

        def epilogue():
            g = _log_decays(lr, wa2_ref, ba2_ref)
            gf_ref[...] = g[:, 0:qk]
            gb_ref[...] = g[:, qk:2 * qk]
        return epilogue

    def conv_stage(j):
        cs = slice(j, j + COL_BLOCK)
        xa = proj(w1_ref, b1_ref, j, COL_BLOCK)
        ba = proj(w1_ref, b1_ref, d + j, COL_BLOCK)
        ca = proj(w1_ref, b1_ref, 2 * d + j, COL_BLOCK)

        def epilogue():
            u = ca * xa
            left = jnp.where(has_left, pltpu.roll(u, 1, 0), 0.0)
            right = jnp.where(has_right, pltpu.roll(u, tm - 1, 0), 0.0)
            y = left * cw_ref[0:1, cs] + u * cw_ref[1:2, cs] + right * cw_ref[2:3, cs]
            ya_ref[:, cs] = (ba * y).astype(BF16)
        return epilogue

    def swish_stage(j):
        r = proj(w2_ref, b2_ref, 2 * qk + d + j, COL_BLOCK)

        def epilogue():
            sr_ref[:, j:j + COL_BLOCK] = (r * _sigmoid(r)).astype(BF16)
        return epilogue

    def qk_stage():
        q = proj(w2_ref, b2_ref, 0, qk)
        k = proj(w2_ref, b2_ref, qk, qk)

        def epilogue():
            q_ref[...] = (q * q_scale).astype(BF16)
            k_ref[...] = k.astype(BF16)
        return epilogue

    def value_stage(j):
        v = proj(w2_ref, b2_ref, 2 * qk + j, COL_BLOCK)

        def epilogue():
            v_ref[:, j:j + COL_BLOCK] = v.astype(BF16)
        return epilogue

    _run_skewed([decay_stage] + [functools.partial(conv_stage, j) for j in range(0, d, COL_BLOCK)]
                + [functools.partial(swish_stage, j) for j in range(0, d, COL_BLOCK)] + [qk_stage]
                + [functools.partial(value_stage, j) for j in range(0, d, COL_BLOCK)])


def _inproj_ctx_kernel(x_ref, sh_ref, sc_ref, g1_ref, w_ref, b_ref, wa2_ref, ba2_ref,
                       k_ref, v_ref, gf_ref, gb_ref):
    d = x_ref.shape[1]
    qk = k_ref.shape[1]
    h = _modulate(x_ref[...], g1_ref[...], sh_ref[...], sc_ref[...]).astype(BF16)
    k_ref[...] = (_dot(h, w_ref[:, 0:qk]) + b_ref[:, 0:qk]).astype(BF16)
    for j in range(0, d, COL_BLOCK):
        c0 = qk + j
        v_ref[:, j:j + COL_BLOCK] = (_dot(h, w_ref[:, c0:c0 + COL_BLOCK]) + b_ref[:, c0:c0 + COL_BLOCK]).astype(BF16)
    c0 = qk + d
    lr = _dot(h, w_ref[:, c0:c0 + LR_PAD]) + b_ref[:, c0:c0 + LR_PAD]
    g = _log_decays(lr, wa2_ref, ba2_ref)
    gf_ref[...] = g[:, 0:qk]
    gb_ref[...] = g[:, qk:2 * qk]


def _resident():
    return pl.BlockSpec(memory_space=pltpu.VMEM)


def _inproj_call(x, mod3, g1, w1, b1, cw, w2, b2, wa2, ba2, qk, dk):
    bsz, t, d = x.shape
    tm = INPROJ_TM
    tok = lambda n: pl.BlockSpec((None, tm, n), lambda b, i: (b, i, 0))
    modrow = lambda col: pl.BlockSpec((None, 1, d), lambda b, i: (b, 0, col))
    out_shapes = (
        jax.ShapeDtypeStruct((bsz, t, d), BF16),
        jax.ShapeDtypeStruct((bsz, t, d), BF16),
        jax.ShapeDtypeStruct((bsz, t, qk), BF16),
        jax.ShapeDtypeStruct((bsz, t, qk), BF16),
        jax.ShapeDtypeStruct((bsz, t, d), BF16),
        jax.ShapeDtypeStruct((bsz, t, d), BF16),
        jax.ShapeDtypeStruct((bsz, t, qk), F32),
        jax.ShapeDtypeStruct((bsz, t, qk), F32),
    )
    pipelined = (_nbytes((tm, d), F32) + 4 * _nbytes((tm, d), BF16) + 2 * _nbytes((tm, qk), BF16)
                 + 2 * _nbytes((tm, qk), F32) + 2 * _nbytes((1, d), F32))
    resident = sum(_nbytes(a.shape, a.dtype) for a in (g1, w1, b1, cw, w2, b2, wa2, ba2))
    temps = 24 * _nbytes((tm, COL_BLOCK), F32) + 3 * _nbytes((tm, 2 * qk), F32)
    return pl.pallas_call(
        functools.partial(_inproj_kernel, q_scale=dk ** -0.5),
        out_shape=out_shapes,
        grid=(bsz, t // tm),
        in_specs=[tok(d), modrow(0), modrow(1)] + [_resident()] * 8,
        out_specs=(tok(d), tok(d), tok(qk), tok(qk), tok(d), tok(d), tok(qk), tok(qk)),
        compiler_params=pltpu.CompilerParams(
            dimension_semantics=("parallel", "parallel"),
            vmem_limit_bytes=_vmem_limit(pipelined, resident, temps),
        ),
        name="inproj",
    )(x, mod3, mod3, g1, w1, b1, cw, w2, b2, wa2, ba2)


def _inproj_ctx_call(ctx, mod3, ctx_row, g1, w, b, wa2, ba2, qk):
    bsz, tc, d = ctx.shape
    tok = lambda n: pl.BlockSpec((None, tc, n), lambda b: (b, 0, 0))
    modrow = lambda col: pl.BlockSpec((None, 1, d), lambda b: (ctx_row, 0, col))
    out_shapes = (
        jax.ShapeDtypeStruct((bsz, tc, qk), BF16),
        jax.ShapeDtypeStruct((bsz, tc, d), BF16),
        jax.ShapeDtypeStruct((bsz, tc, qk), F32),
        jax.ShapeDtypeStruct((bsz, tc, qk), F32),
    )
    pipelined = (_nbytes((tc, d), F32) + _nbytes((tc, d), BF16) + _nbytes((tc, qk), BF16)
                 + 2 * _nbytes((tc, qk), F32) + 2 * _nbytes((1, d), F32))
    resident = sum(_nbytes(a.shape, a.dtype) for a in (g1, w, b, wa2, ba2))
    temps = _nbytes((tc, d), BF16) + 6 * _nbytes((tc, COL_BLOCK), F32) + 3 * _nbytes((tc, 2 * qk), F32)
    return pl.pallas_call(
        _inproj_ctx_kernel,
        out_shape=out_shapes,
        grid=(bsz,),
        in_specs=[tok(d), modrow(0), modrow(1)] + [_resident()] * 5,
        out_specs=(tok(qk), tok(d), tok(qk), tok(qk)),
        compiler_params=pltpu.CompilerParams(
            dimension_semantics=("parallel",),
            vmem_limit_bytes=_vmem_limit(pipelined, resident, temps),
        ),
        name="inproj_ctx",
    )(ctx, mod3, mod3, g1, w, b, wa2, ba2)


class _ScanDir(NamedTuple):
    fwd: bool
    g_ref: object
    gc_ref: object
    chunk_sum: object
    diag_sum: object
    split_masks: tuple
    diag_mask: object
    kt_scr: object
    qb_scr: object
    dm_scr: object
    s_scr: object


def _block_sums(sum_matrix, g):
    dk = g.shape[1]
    g_hi = g.astype(BF16)
    g_lo = (g - g_hi.astype(F32)).astype(BF16)
    bb = _dot(sum_matrix, jnp.concatenate([g_hi, g_lo], axis=1))
    return bb[:, :dk] + bb[:, dk:]


def _tile_decays(g, sum_matrix, fwd, n_chunks):
    rows, dk = g.shape
    c = rows // n_chunks
    b = _block_sums(sum_matrix, g).reshape(n_chunks, c, dk)
    return b, (b[:, c - 1:c, :] if fwd else b[:, 0:1, :])


def _block_row(x, block, row):
    rows, n = x.shape
    x3 = x.reshape(rows // block, block, n)
    return jnp.broadcast_to(x3[:, row:row + 1, :], x3.shape).reshape(rows, n)


def _pair_offsets(tot, fwd):
    n_chunks = tot.shape[0]
    zero = jnp.zeros_like(tot[0:1])
    on_odd = jnp.concatenate([t for p in range(0, n_chunks, GLA_PAIR) for t in (zero, tot[p:p + 1])], axis=0)
    on_even = jnp.concatenate([t for p in range(0, n_chunks, GLA_PAIR) for t in (tot[p + 1:p + 2], zero)], axis=0)
    pair_tot = jnp.concatenate([tot[p:p + 1] + tot[p + 1:p + 2] for p in range(0, n_chunks, GLA_PAIR)], axis=0)
    return (on_odd, on_even, pair_tot) if fwd else (on_even, on_odd, pair_tot)


def _lane_broadcast_column(row):
    n = row.shape[1]
    return jnp.broadcast_to(row, (n, n)).T


def _gla_kernel(q_ref, k_ref, v_ref, sr_ref, gf_ref, gb_ref, kc_ref, vc_ref, gfc_ref, gbc_ref, gn_ref,
                out_ref, a_scr, o_scr, ktf_scr, ktb_scr, qbf_scr, qbb_scr, dmf_scr, dmb_scr, sf_scr, sb_scr):
    c = GLA_KERNEL_CHUNK
    grp = GLA_GROUP
    tile = c * grp
    pair = c * GLA_PAIR
    pairs_per_tile = grp // GLA_PAIR
    diag = GLA_DIAG_BLOCK
    t, dk = q_ref.shape
    dv = v_ref.shape[1]
    n_pairs = t // pair
    row = lax.broadcasted_iota(jnp.int32, (tile, tile), 0)
    col = lax.broadcasted_iota(jnp.int32, (tile, tile), 1)
    halves = []
    half = pair // 2
    while half >= diag:
        halves.append(half)
        half //= 2

    def same_block(size):
        return (row // size) == (col // size)

    def scan_dir(fwd, g_ref, gc_ref, kt_scr, qb_scr, dm_scr, s_scr):
        seen = (row >= col) if fwd else (row <= col)
        split_masks = []
        for h in halves:
            later, earlier = (row % (2 * h) >= h), (col % (2 * h) < h)
            if not fwd:
                later, earlier = (row % (2 * h) < h), (col % (2 * h) >= h)
            split_masks.append((h, (same_block(2 * h) & later & earlier).astype(BF16)))
        return _ScanDir(fwd, g_ref, gc_ref, (same_block(c) & seen).astype(BF16),
                        (same_block(diag) & seen).astype(BF16), tuple(split_masks),
                        same_block(diag) & seen, kt_scr, qb_scr, dm_scr, s_scr)

    dirs = (scan_dir(True, gf_ref, gfc_ref, ktf_scr, qbf_scr, dmf_scr, sf_scr),
            scan_dir(False, gb_ref, gbc_ref, ktb_scr, qbb_scr, dmb_scr, sb_scr))

    def key_side(k3, b, tot, k_off):
        kt = k3 * jnp.exp2(tot - b)
        return kt, kt * jnp.exp2(k_off)

    ctx_tiles = kc_ref.shape[0] // tile
    ctx_items = [(d, ti) for d in dirs for ti in range(ctx_tiles)]
    ctx_cums = [_tile_decays(d.gc_ref[ti * tile:(ti + 1) * tile, :], d.chunk_sum, d.fwd, grp)
                for d, ti in ctx_items]
    ctx_terms = {}
    for (d, ti), (b, tot) in zip(ctx_items, ctx_cums):
        _, k_off, pair_tot = _pair_offsets(tot, d.fwd)
        kc3 = kc_ref[ti * tile:(ti + 1) * tile, :].astype(F32).reshape(grp, c, dk)
        _, kt_pair = key_side(kc3, b, tot, k_off)
        kt_t = kt_pair.reshape(tile, dk).T.astype(BF16)
        decay = jnp.exp2(pair_tot)
        for p in range(pairs_per_tile):
            r0 = ti * tile + p * pair
            ctx_terms[(d.fwd, ti * pairs_per_tile + p)] = (
                _dot(kt_t[:, p * pair:(p + 1) * pair], vc_ref[r0:r0 + pair, :]),
                _lane_broadcast_column(decay[p]))
    for d in dirs:
        order = range(ctx_tiles * pairs_per_tile)
        s = jnp.zeros((dk, dv), F32)
        for p in (order if d.fwd else reversed(order)):
            inc, dm = ctx_terms[(d.fwd, p)]
            s = s * jnp.concatenate([dm] * (dv // dk), axis=1) + inc
        d.s_scr[...] = s

    def local_body(i, carry):
        tiles = [i * GLA_LOCAL_UNROLL + u for u in range(GLA_LOCAL_UNROLL)]
        rows = [pl.ds(pl.multiple_of(ti * tile, tile), tile) for ti in tiles]
        items = [(u, d) for u in range(GLA_LOCAL_UNROLL) for d in dirs]
        g_tiles = [d.g_ref[rows[u], :] for u, d in items]
        cums = [_tile_decays(g, d.chunk_sum, d.fwd, grp) for (u, d), g in zip(items, g_tiles)]
        diag_cums = [_block_sums(d.diag_sum, jnp.maximum(g, GLA_DIAG_DECAY_FLOOR))
                     for (u, d), g in zip(items, g_tiles)]
        q2 = [q_ref[sl, :].astype(F32) for sl in rows]
        k2 = [k_ref[sl, :].astype(F32) for sl in rows]
        products = []
        for (u, d), (b, tot), bd in zip(items, cums, diag_cums):
            q_off, k_off, pair_tot = _pair_offsets(tot, d.fwd)
            q3, k3 = q2[u].reshape(grp, c, dk), k2[u].reshape(grp, c, dk)
            qb = q3 * jnp.exp2(b)
            kt, kt_pair = key_side(k3, b, tot, k_off)
            flat = b.reshape(tile, dk)
            split = []
            for h, mask in d.split_masks:
                if h == c:
                    qs, ks = qb.reshape(tile, dk), kt.reshape(tile, dk)
                else:
                    e = jnp.exp2(-jnp.abs(flat - _block_row(flat, 2 * h, h - 1 if d.fwd else h)))
                    qs, ks = q2[u] * e, k2[u] * e
                split.append((_nt_dot(qs.astype(BF16), ks.astype(BF16)), mask))
            mid = _block_row(bd, diag, diag // 2 - 1 if d.fwd else diag // 2)
            inner = _nt_dot((q2[u] * jnp.exp2(bd - mid)).astype(BF16), (k2[u] * jnp.exp2(mid - bd)).astype(BF16))
            products.append((split, inner))
            d.qb_scr[rows[u], :] = (qb * jnp.exp2(q_off)).reshape(tile, dk).astype(BF16)
            kt_t = kt_pair.reshape(tile, dk).T.astype(BF16)
            decay = jnp.exp2(pair_tot)
            for p in range(pairs_per_tile):
                d.kt_scr[tiles[u] * pairs_per_tile + p] = kt_t[:, p * pair:(p + 1) * pair]
                d.dm_scr[tiles[u] * pairs_per_tile + p] = _lane_broadcast_column(decay[p])
        for u in range(GLA_LOCAL_UNROLL):
            scores = None
            for (iu, d), (split, inner) in zip(items, products):
                if iu != u:
                    continue
                part = jnp.where(d.diag_mask, inner, 0.0).astype(BF16)
                for product, mask in split:
                    part = part + product.astype(BF16) * mask
                scores = part if scores is None else scores + part
            for p in range(pairs_per_tile):
                a_scr[pl.ds(pl.multiple_of(tiles[u] * tile + p * pair, pair), pair), :] = (
                    scores[p * pair:(p + 1) * pair, p * pair:(p + 1) * pair])
        return carry

    lax.fori_loop(0, t // (tile * GLA_LOCAL_UNROLL), local_body, 0)

    def scan_body(i, carry, finish):
        steps = []
        for u in range(GLA_SCAN_UNROLL):
            for d in dirs:
                step = i * GLA_SCAN_UNROLL + u
                j = step if d.fwd else n_pairs - 1 - step
                steps.append((d, j, pl.ds(pl.multiple_of(j * pair, pair), pair)))
        increments = [_dot(d.kt_scr[j], v_ref[sl, :]) for d, j, sl in steps]
        local = [None if finish else _dot(a_scr[sl, :], v_ref[sl, :]) for _, _, sl in steps]
        outs = []
        for (d, j, sl), inc in zip(steps, increments):
            qb_scr, dm_scr, s_scr = d.qb_scr, d.dm_scr, d.s_scr
            s = s_scr[...]
            outs.append(_dot(qb_scr[sl, :], s.astype(BF16)))
            s_scr[...] = s * jnp.concatenate([dm_scr[j]] * (dv // dk), axis=1) + inc
        for (_, _, sl), o, loc in zip(steps, outs, local):
            if finish:
                y = _rmsnorm(o + o_scr[sl, :], gn_ref[...])
                out_ref[sl, :] = (y * sr_ref[sl, :].astype(F32)).astype(BF16)
            else:
                o_scr[sl, :] = o + loc
        return carry

    half = n_pairs // (2 * GLA_SCAN_UNROLL)
    lax.fori_loop(0, half, functools.partial(scan_body, finish=False), 0)
    lax.fori_loop(half, 2 * half, functools.partial(scan_body, finish=True), 0)


def _gla_call(q, k, v, sr, gf, gb, kc, vc, gfc, gbc, gn):
    bsz, t, qk = q.shape
    d = v.shape[2]
    tc = kc.shape[1]
    dk = qk // GLA_HEADS
    dv = d // GLA_HEADS
    pair = GLA_KERNEL_CHUNK * GLA_PAIR
    n_pairs = t // pair
    tile = GLA_KERNEL_CHUNK * GLA_GROUP
    seq = lambda rows, width: pl.BlockSpec((None, rows, width), lambda b, h: (b, 0, h))
    scratch_shapes = [
        pltpu.VMEM((t, pair), BF16),
        pltpu.VMEM((t, dv), F32),
        pltpu.VMEM((n_pairs, dk, pair), BF16), pltpu.VMEM((n_pairs, dk, pair), BF16),
        pltpu.VMEM((t, dk), BF16), pltpu.VMEM((t, dk), BF16),
        pltpu.VMEM((n_pairs, dk, dk), F32), pltpu.VMEM((n_pairs, dk, dk), F32),
        pltpu.VMEM((dk, dv), F32), pltpu.VMEM((dk, dv), F32),
    ]
    pipelined = (2 * _nbytes((t, dk), BF16) + 3 * _nbytes((t, dv), BF16) + 2 * _nbytes((t, dk), F32)
                 + _nbytes((tc, dk), BF16) + _nbytes((tc, dv), BF16) + 2 * _nbytes((tc, dk), F32)
                 + _nbytes((1, dv), F32))
    scratch = (_nbytes((t, pair), BF16) + _nbytes((t, dv), F32) + 2 * _nbytes((n_pairs, dk, pair), BF16)
               + 2 * _nbytes((t, dk), BF16) + 2 * _nbytes((n_pairs, dk, dk), F32) + 2 * _nbytes((dk, dv), F32))
    temps = 24 * _nbytes((tile, dv), F32)
    return pl.pallas_call(
        _gla_kernel,
        out_shape=jax.ShapeDtypeStruct((bsz, t, d), BF16),
        grid=(bsz, GLA_HEADS),
        in_specs=[seq(t, dk), seq(t, dk), seq(t, dv), seq(t, dv), seq(t, dk), seq(t, dk),
                  seq(tc, dk), seq(tc, dv), seq(tc, dk), seq(tc, dk),
                  pl.BlockSpec((1, dv), lambda b, h: (0, h))],
        out_specs=seq(t, dv),
        scratch_shapes=scratch_shapes,
        compiler_params=pltpu.CompilerParams(
            dimension_semantics=("parallel", "parallel"),
            vmem_limit_bytes=_vmem_limit(pipelined, scratch, temps),
        ),
        name="gla",
    )(q, k, v, sr, gf, gb, kc, vc, gfc, gbc, gn)


def _out_kernel(x_ref, h1_ref, ya_ref, yb_ref, ga1_ref, sh2_ref, sc2_ref, ga2_ref, g2_ref, gfin_ref,
                wg_ref, bg_ref, wco_ref, wgo_ref, wo_ref, wup_ref, wdn_ref, o_ref, y_scr, h2_scr):
    tm, d = x_ref.shape
    d_ff = wup_ref.shape[1]

    def merge_stage(j):
        cs = slice(j, j + COL_BLOCK)
        h1 = h1_ref[...]
        za = _dot(h1, wg_ref[:, j:j + COL_BLOCK]) + bg_ref[:, j:j + COL_BLOCK]
        zb = _dot(h1, wg_ref[:, d + j:d + j + COL_BLOCK]) + bg_ref[:, d + j:d + j + COL_BLOCK]
        pa = _dot(ya_ref[...], wco_ref[:, cs])
        pb = _dot(yb_ref[...], wgo_ref[:, cs])

        def epilogue():
            y_scr[:, cs] = (_sigmoid(za) * pa + _sigmoid(zb) * pb).astype(BF16)
        return epilogue

    _run_skewed([functools.partial(merge_stage, j) for j in range(0, d, COL_BLOCK)])
    o_ref[...] = x_ref[...] + ga1_ref[...] * _dot(y_scr[...], wo_ref[...])
    h2_scr[...] = _modulate(o_ref[...], g2_ref[...], sh2_ref[...], sc2_ref[...]).astype(BF16)

    parts = []

    def mlp_stage(j):
        u = _dot(h2_scr[...], wup_ref[:, j:j + FF_BLOCK])

        def epilogue():
            a = jnp.maximum(u, 0.0)
            parts.append(_dot((a * a).astype(BF16), wdn_ref[j:j + FF_BLOCK, :]))
        return epilogue

    _run_skewed([functools.partial(mlp_stage, j) for j in range(0, d_ff, FF_BLOCK)])
    x2 = o_ref[...] + ga2_ref[...] * functools.reduce(lambda a, b: a + b, parts)
    o_ref[...] = _rmsnorm(x2, gfin_ref[...])


def _out_call(x, h1, ya, yb, mod3, g2, gfin, wg, bg, wco, wgo, wo, wup, wdn):
    bsz, t, d = x.shape
    tm = OUT_TM
    tok = lambda: pl.BlockSpec((None, tm, d), lambda b, i: (b, i, 0))
    modrow = lambda col: pl.BlockSpec((None, 1, d), lambda b, i: (b, 0, col))
    pipelined = 2 * _nbytes((tm, d), F32) + 3 * _nbytes((tm, d), BF16) + 4 * _nbytes((1, d), F32)
    resident = sum(_nbytes(a.shape, a.dtype) for a in (g2, gfin, wg, bg, wco, wgo, wo, wup, wdn))
    scratch = 2 * _nbytes((tm, d), BF16)
    temps = 10 * _nbytes((tm, d), F32)
    return pl.pallas_call(
        _out_kernel,
        out_shape=jax.ShapeDtypeStruct((bsz, t, d), F32),
        grid=(bsz, t // tm),
        in_specs=[tok(), tok(), tok(), tok()] + [modrow(cidx) for cidx in (2, 3, 4, 5)] + [_resident()] * 9,
        out_specs=tok(),
        scratch_shapes=[pltpu.VMEM((tm, d), BF16), pltpu.VMEM((tm, d), BF16)],
        compiler_params=pltpu.CompilerParams(
            dimension_semantics=("parallel", "parallel"),
            vmem_limit_bytes=_vmem_limit(pipelined, resident + scratch, temps),
        ),
        name="out",
    )(x, h1, ya, yb, mod3, mod3, mod3, mod3, g2, gfin, wg, bg, wco, wgo, wo, wup, wdn)


def kernel(x, c, ctx, c_ctx, w_ada, b_ada, g_norm1, w_in, b_in, conv_w, w_conv_out, w_a2_f, b_a_f,
           w_a2_b, b_a_b, g_gla_norm, w_gla_out, w_o, g_norm2, w_up, w_down, g_final):
    depth = w_ada.shape[0]
    assert depth == 1, "only the single-layer block is implemented"
    bsz, t, d = x.shape
    qk = w_a2_f.shape[2]
    dk = qk // GLA_HEADS
    rank = w_a2_f.shape[1]
    gla_tile = GLA_GROUP * GLA_KERNEL_CHUNK
    assert t % INPROJ_TM == 0 and t % OUT_TM == 0 and INPROJ_TM % GRID_W == 0
    assert GLA_PAIR == 2 and GLA_GROUP % GLA_PAIR == 0 and GLA_PAIR * GLA_KERNEL_CHUNK == dk
    gla_pair = GLA_PAIR * GLA_KERNEL_CHUNK
    assert t % (GLA_LOCAL_UNROLL * gla_tile) == 0 and ctx.shape[1] % gla_tile == 0
    assert t % (2 * GLA_SCAN_UNROLL * gla_pair) == 0
    assert bsz + 1 <= MOD_ROWS and 2 * rank <= LR_PAD

    cc = jnp.zeros((MOD_ROWS, d), F32).at[:bsz].set(c).at[bsz].set(c_ctx)
    mod = _ada_call(cc, w_ada[0], b_ada[0][None, :])
    mod3 = mod.reshape(MOD_ROWS, 1, N_MOD * d)

    w = w_in[0]
    bias = b_in[0][None, :]
    o_q = 3 * d
    o_k = o_q + qk
    o_v = o_k + qk
    o_r = o_v + d
    o_lr = o_r + d
    o_g = o_lr + 2 * rank
    lr_pad = LR_PAD - 2 * rank
    w1 = w[:, :o_q].astype(BF16)
    b1 = bias[:, :o_q]
    w_lr = jnp.pad(w[:, o_lr:o_g], ((0, 0), (0, lr_pad)))
    b_lr = jnp.pad(bias[:, o_lr:o_g], ((0, 0), (0, lr_pad)))
    w2 = jnp.concatenate([w[:, o_q:o_lr], w_lr], axis=1).astype(BF16)
    b2 = jnp.concatenate([bias[:, o_q:o_lr], b_lr], axis=1)
    w2c = jnp.concatenate([w[:, o_k:o_r], w_lr], axis=1).astype(BF16)
    b2c = jnp.concatenate([bias[:, o_k:o_r], b_lr], axis=1)
    wg = w[:, o_g:].astype(BF16)
    bg = bias[:, o_g:]
    wa2 = jnp.zeros((LR_PAD, 2 * qk), F32)
    wa2 = wa2.at[:rank, :qk].set(w_a2_f[0]).at[rank:2 * rank, qk:].set(w_a2_b[0]).astype(BF16)
    ba2 = jnp.concatenate([b_a_f[0], b_a_b[0]])[None, :]
    g1 = g_norm1[0][None, :]

    h1, ya, q, k, v, sr, gf, gb = _inproj_call(x, mod3, g1, w1, b1, conv_w[0], w2, b2, wa2, ba2, qk, dk)
    kc, vc, gfc, gbc = _inproj_ctx_call(ctx, mod3, bsz, g1, w2c, b2c, wa2, ba2, qk)
    yb = _gla_call(q, k, v, sr, gf, gb, kc, vc, gfc, gbc, g_gla_norm[0][None, :])
    return _out_call(x, h1, ya, yb, mod3, g_norm2[0][None, :], g_final[None, :], wg, bg,
                     w_conv_out[0].astype(BF16), w_gla_out[0].astype(BF16), w_o[0].astype(BF16),
                     w_up[0].astype(BF16), w_down[0].astype(BF16))
```

```python
import functools
from typing import NamedTuple

import jax
import jax.numpy as jnp
from jax import lax
from jax.experimental import pallas as pl
from jax.experimental.pallas import tpu as pltpu

F32 = jnp.float32
BF16 = jnp.bfloat16
HIGHEST = lax.Precision.HIGHEST

GLA_HEADS = 4
GLA_TAU = 16.0
GRID_W = 64
N_MOD = 6
RMS_EPS = 1e-6
GLA_RANK = 16
LOG2_E = 1.4426950408889634

LR_PAD = 128
GLA_KERNEL_CHUNK = 64
GLA_DIAG_BLOCK = 8
GLA_DIAG_DECAY_FLOOR = -31.0
GLA_MILD_DECAY_FLOOR = -126.0 / (GLA_KERNEL_CHUNK // 2)
GLA_PAIR = 2
GLA_GROUP = 4
GLA_LOCAL_UNROLL = 4
GLA_SCAN_UNROLL = 8
INPROJ_TM = 1024
OUT_TM = 512
COL_BLOCK = 256
FF_BLOCK = 1024
MOD_ROWS = 16
V7X_VMEM_LIMIT_BYTES = 60000 * 1024


def _vmem_limit(pipelined_bytes, resident_bytes, temp_bytes):
    need = 2 * pipelined_bytes + resident_bytes + temp_bytes
    return int(min(V7X_VMEM_LIMIT_BYTES, need))


def _nbytes(shape, dtype):
    n = 1
    for s in shape:
        n *= s
    return n * jnp.dtype(dtype).itemsize


def _dot(a, b):
    return jnp.dot(a, b, preferred_element_type=F32)


def _nt_dot(a, b):
    return lax.dot_general(a, b, (((1,), (1,)), ((), ())), preferred_element_type=F32)


def _run_skewed(stages):
    pending = None
    for stage in stages:
        epilogue = stage()
        if pending is not None:
            pending()
        pending = epilogue
    if pending is not None:
        pending()


def _sigmoid(x):
    return 1.0 / (1.0 + jnp.exp(-x))


def _log_sigmoid(x):
    return jnp.minimum(x, 0.0) - jnp.log(1.0 + jnp.exp(-jnp.abs(x)))


def _rmsnorm(x, g):
    return x * lax.rsqrt(jnp.mean(x * x, axis=-1, keepdims=True) + RMS_EPS) * g


def _modulate(x, g, shift, scale):
    return _rmsnorm(x, g) * (1.0 + scale) + shift


def _ada_kernel(c_ref, w_ref, b_ref, o_ref):
    c = c_ref[...]
    s = c * _sigmoid(c)
    o_ref[...] = jnp.dot(s, w_ref[...], precision=HIGHEST, preferred_element_type=F32) + b_ref[...]


def _ada_call(cc, w_ada, b_ada):
    d = cc.shape[1]
    n_out = w_ada.shape[1]
    return pl.pallas_call(
        _ada_kernel,
        out_shape=jax.ShapeDtypeStruct((MOD_ROWS, n_out), F32),
        grid=(n_out // d,),
        in_specs=[
            pl.BlockSpec((MOD_ROWS, d), lambda j: (0, 0)),
            pl.BlockSpec((d, d), lambda j: (0, j)),
            pl.BlockSpec((1, d), lambda j: (0, j)),
        ],
        out_specs=pl.BlockSpec((MOD_ROWS, d), lambda j: (0, j)),
        compiler_params=pltpu.CompilerParams(
            dimension_semantics=("arbitrary",),
            vmem_limit_bytes=_vmem_limit(_nbytes((d, d), F32) + _nbytes((MOD_ROWS, 2 * d), F32),
                                         0, 4 * _nbytes((d, d), F32)),
        ),
        name="ada",
    )(cc, w_ada, b_ada)


def _log_decays(lr, wa2_ref, ba2_ref):
    xg = _dot(lr.astype(BF16), wa2_ref[...]) + ba2_ref[...]
    return _log_sigmoid(xg) * (LOG2_E / GLA_TAU)


def _inproj_kernel(x_ref, sh_ref, sc_ref, g1_ref, w1_ref, b1_ref, cw_ref, w2_ref, b2_ref, wa2_ref, ba2_ref,
                   h_ref, ya_ref, q_ref, k_ref, v_ref, sr_ref, gf_ref, gb_ref, *, q_scale):
    tm, d = x_ref.shape
    qk = q_ref.shape[1]
    h_ref[...] = _modulate(x_ref[...], g1_ref[...], sh_ref[...], sc_ref[...]).astype(BF16)

    col_in_row = lax.broadcasted_iota(jnp.int32, (tm, 1), 0) % GRID_W
    has_left = col_in_row != 0
    has_right = col_in_row != GRID_W - 1

    def proj(w_ref, b_ref, c0, width):
        return _dot(h_ref[...], w_ref[:, c0:c0 + width]) + b_ref[:, c0:c0 + width]

    def decay_stage():
        lr = proj(w2_ref, b2_ref, 2 * qk + 2 * d, LR_PAD)

        def epilogue():
            g = _log_decays(lr, wa2_ref, ba2_ref)
            gf_ref[...] = g[:, 0:qk]
            gb_ref[...] = g[:, qk:2 * qk]
        return epilogue

    def conv_stage(j):
        cs = slice(j, j + COL_BLOCK)
        xa = proj(w1_ref, b1_ref, j, COL_BLOCK)
        ba = proj(w1_ref, b1_ref, d + j, COL_BLOCK)
        ca = proj(w1_ref, b1_ref, 2 * d + j, COL_BLOCK)

        def epilogue():
            u = ca * xa
            left = jnp.where(has_left, pltpu.roll(u, 1, 0), 0.0)
            right = jnp.where(has_right, pltpu.roll(u, tm - 1, 0), 0.0)
            y = left * cw_ref[0:1, cs] + u * cw_ref[1:2, cs] + right * cw_ref[2:3, cs]
            ya_ref[:, cs] = (ba * y).astype(BF16)
        return epilogue

    def swish_stage(j):
        r = proj(w2_ref, b2_ref, 2 * qk + d + j, COL_BLOCK)

        def epilogue():
            sr_ref[:, j:j + COL_BLOCK] = (r * _sigmoid(r)).astype(BF16)
        return epilogue

    def qk_stage():
        q = proj(w2_ref, b2_ref, 0, qk)
        k = proj(w2_ref, b2_ref, qk, qk)

        def epilogue():
            q_ref[...] = (q * q_scale).astype(BF16)
            k_ref[...] = k.astype(BF16)
        return epilogue

    def value_stage(j):
        v = proj(w2_ref, b2_ref, 2 * qk + j, COL_BLOCK)

        def epilogue():
            v_ref[:, j:j + COL_BLOCK] = v.astype(BF16)
        return epilogue

    _run_skewed([decay_stage] + [functools.partial(conv_stage, j) for j in range(0, d, COL_BLOCK)]
                + [functools.partial(swish_stage, j) for j in range(0, d, COL_BLOCK)] + [qk_stage]
                + [functools.partial(value_stage, j) for j in range(0, d, COL_BLOCK)])


def _inproj_ctx_kernel(x_ref, sh_ref, sc_ref, g1_ref, w_ref, b_ref, wa2_ref, ba2_ref,
                       k_ref, v_ref, gf_ref, gb_ref):
    d = x_ref.shape[1]
    qk = k_ref.shape[1]
    h = _modulate(x_ref[...], g1_ref[...], sh_ref[...], sc_ref[...]).astype(BF16)
    k_ref[...] = (_dot(h, w_ref[:, 0:qk]) + b_ref[:, 0:qk]).astype(BF16)
    for j in range(0, d, COL_BLOCK):
        c0 = qk + j
        v_ref[:, j:j + COL_BLOCK] = (_dot(h, w_ref[:, c0:c0 + COL_BLOCK]) + b_ref[:, c0:c0 + COL_BLOCK]).astype(BF16)
    c0 = qk + d
    lr = _dot(h, w_ref[:, c0:c0 + LR_PAD]) + b_ref[:, c0:c0 + LR_PAD]
    g = _log_decays(lr, wa2_ref, ba2_ref)
    gf_ref[...] = g[:, 0:qk]
    gb_ref[...] = g[:, qk:2 * qk]


def _resident():
    return pl.BlockSpec(memory_space=pltpu.VMEM)


def _inproj_call(x, mod3, g1, w1, b1, cw, w2, b2, wa2, ba2, qk, dk):
    bsz, t, d = x.shape
    tm = INPROJ_TM
    tok = lambda n: pl.BlockSpec((None, tm, n), lambda b, i: (b, i, 0))
    modrow = lambda col: pl.BlockSpec((None, 1, d), lambda b, i: (b, 0, col))
    out_shapes = (
        jax.ShapeDtypeStruct((bsz, t, d), BF16),
        jax.ShapeDtypeStruct((bsz, t, d), BF16),
        jax.ShapeDtypeStruct((bsz, t, qk), BF16),
        jax.ShapeDtypeStruct((bsz, t, qk), BF16),
        jax.ShapeDtypeStruct((bsz, t, d), BF16),
        jax.ShapeDtypeStruct((bsz, t, d), BF16),
        jax.ShapeDtypeStruct((bsz, t, qk), F32),
        jax.ShapeDtypeStruct((bsz, t, qk), F32),
    )
    pipelined = (_nbytes((tm, d), F32) + 4 * _nbytes((tm, d), BF16) + 2 * _nbytes((tm, qk), BF16)
                 + 2 * _nbytes((tm, qk), F32) + 2 * _nbytes((1, d), F32))
    resident = sum(_nbytes(a.shape, a.dtype) for a in (g1, w1, b1, cw, w2, b2, wa2, ba2))
    temps = 24 * _nbytes((tm, COL_BLOCK), F32) + 3 * _nbytes((tm, 2 * qk), F32)
    return pl.pallas_call(
        functools.partial(_inproj_kernel, q_scale=dk ** -0.5),
        out_shape=out_shapes,
        grid=(bsz, t // tm),
        in_specs=[tok(d), modrow(0), modrow(1)] + [_resident()] * 8,
        out_specs=(tok(d), tok(d), tok(qk), tok(qk), tok(d), tok(d), tok(qk), tok(qk)),
        compiler_params=pltpu.CompilerParams(
            dimension_semantics=("parallel", "parallel"),
            vmem_limit_bytes=_vmem_limit(pipelined, resident, temps),
        ),
        name="inproj",
    )(x, mod3, mod3, g1, w1, b1, cw, w2, b2, wa2, ba2)


def _inproj_ctx_call(ctx, mod3, ctx_row, g1, w, b, wa2, ba2, qk):
    bsz, tc, d = ctx.shape
    tok = lambda n: pl.BlockSpec((None, tc, n), lambda b: (b, 0, 0))
    modrow = lambda col: pl.BlockSpec((None, 1, d), lambda b: (ctx_row, 0, col))
    out_shapes = (
        jax.ShapeDtypeStruct((bsz, tc, qk), BF16),
        jax.ShapeDtypeStruct((bsz, tc, d), BF16),
        jax.ShapeDtypeStruct((bsz, tc, qk), F32),
        jax.ShapeDtypeStruct((bsz, tc, qk), F32),
    )
    pipelined = (_nbytes((tc, d), F32) + _nbytes((tc, d), BF16) + _nbytes((tc, qk), BF16)
                 + 2 * _nbytes((tc, qk), F32) + 2 * _nbytes((1, d), F32))
    resident = sum(_nbytes(a.shape, a.dtype) for a in (g1, w, b, wa2, ba2))
    temps = _nbytes((tc, d), BF16) + 6 * _nbytes((tc, COL_BLOCK), F32) + 3 * _nbytes((tc, 2 * qk), F32)
    return pl.pallas_call(
        _inproj_ctx_kernel,
        out_shape=out_shapes,
        grid=(bsz,),
        in_specs=[tok(d), modrow(0), modrow(1)] + [_resident()] * 5,
        out_specs=(tok(qk), tok(d), tok(qk), tok(qk)),
        compiler_params=pltpu.CompilerParams(
            dimension_semantics=("parallel",),
            vmem_limit_bytes=_vmem_limit(pipelined, resident, temps),
        ),
        name="inproj_ctx",
    )(ctx, mod3, mod3, g1, w, b, wa2, ba2)


class _ScanDir(NamedTuple):
    fwd: bool
    g_ref: object
    gc_ref: object
    chunk_sum: object
    diag_sum: object
    split_masks: tuple
    chunk_mask: object
    diag_mask: object
    kt_scr: object
    qb_scr: object
    dm_scr: object
    s_scr: object


def _block_sums(sum_matrix, g):
    dk = g.shape[1]
    g_hi = g.astype(BF16)
    g_lo = (g - g_hi.astype(F32)).astype(BF16)
    bb = _dot(sum_matrix, jnp.concatenate([g_hi, g_lo], axis=1))
    return bb[:, :dk] + bb[:, dk:]


def _tile_decays(g, sum_matrix, fwd, n_chunks):
    rows, dk = g.shape
    c = rows // n_chunks
    b = _block_sums(sum_matrix, g).reshape(n_chunks, c, dk)
    return b, (b[:, c - 1:c, :] if fwd else b[:, 0:1, :])


def _block_row(x, block, row):
    rows, n = x.shape
    x3 = x.reshape(rows // block, block, n)
    return jnp.broadcast_to(x3[:, row:row + 1, :], x3.shape).reshape(rows, n)


def _pair_offsets(tot, fwd):
    n_chunks = tot.shape[0]
    zero = jnp.zeros_like(tot[0:1])
    on_odd = jnp.concatenate([t for p in range(0, n_chunks, GLA_PAIR) for t in (zero, tot[p:p + 1])], axis=0)
    on_even = jnp.concatenate([t for p in range(0, n_chunks, GLA_PAIR) for t in (tot[p + 1:p + 2], zero)], axis=0)
    pair_tot = jnp.concatenate([tot[p:p + 1] + tot[p + 1:p + 2] for p in range(0, n_chunks, GLA_PAIR)], axis=0)
    return (on_odd, on_even, pair_tot) if fwd else (on_even, on_odd, pair_tot)


def _lane_broadcast_column(row):
    n = row.shape[1]
    return jnp.broadcast_to(row, (n, n)).T


def _gla_kernel(q_ref, k_ref, v_ref, sr_ref, gf_ref, gb_ref, kc_ref, vc_ref, gfc_ref, gbc_ref, gn_ref,
                out_ref, a_scr, o_scr, ktf_scr, ktb_scr, qbf_scr, qbb_scr, dmf_scr, dmb_scr, sf_scr, sb_scr):
    c = GLA_KERNEL_CHUNK
    grp = GLA_GROUP
    tile = c * grp
    pair = c * GLA_PAIR
    pairs_per_tile = grp // GLA_PAIR
    diag = GLA_DIAG_BLOCK
    t, dk = q_ref.shape
    dv = v_ref.shape[1]
    n_pairs = t // pair
    row = lax.broadcasted_iota(jnp.int32, (tile, tile), 0)
    col = lax.broadcasted_iota(jnp.int32, (tile, tile), 1)
    halves = []
    half = pair // 2
    while half >= diag:
        halves.append(half)
        half //= 2

    def same_block(size):
        return (row // size) == (col // size)

    def scan_dir(fwd, g_ref, gc_ref, kt_scr, qb_scr, dm_scr, s_scr):
        seen = (row >= col) if fwd else (row <= col)
        split_masks = []
        for h in halves:
            later, earlier = (row % (2 * h) >= h), (col % (2 * h) < h)
            if not fwd:
                later, earlier = (row % (2 * h) < h), (col % (2 * h) >= h)
            split_masks.append((h, (same_block(2 * h) & later & earlier).astype(BF16)))
        return _ScanDir(fwd, g_ref, gc_ref, (same_block(c) & seen).astype(BF16),
                        (same_block(diag) & seen).astype(BF16), tuple(split_masks),
                        same_block(c) & seen, same_block(diag) & seen, kt_scr, qb_scr, dm_scr, s_scr)

    dirs = (scan_dir(True, gf_ref, gfc_ref, ktf_scr, qbf_scr, dmf_scr, sf_scr),
            scan_dir(False, gb_ref, gbc_ref, ktb_scr, qbb_scr, dmb_scr, sb_scr))

    def key_side(k3, b, tot, k_off):
        kt = k3 * jnp.exp2(tot - b)
        return kt, kt * jnp.exp2(k_off)

    ctx_tiles = kc_ref.shape[0] // tile
    ctx_items = [(d, ti) for d in dirs for ti in range(ctx_tiles)]
    ctx_cums = [_tile_decays(d.gc_ref[ti * tile:(ti + 1) * tile, :], d.chunk_sum, d.fwd, grp)
                for d, ti in ctx_items]
    ctx_terms = {}
    for (d, ti), (b, tot) in zip(ctx_items, ctx_cums):
        _, k_off, pair_tot = _pair_offsets(tot, d.fwd)
        kc3 = kc_ref[ti * tile:(ti + 1) * tile, :].astype(F32).reshape(grp, c, dk)
        _, kt_pair = key_side(kc3, b, tot, k_off)
        kt_t = kt_pair.reshape(tile, dk).T.astype(BF16)
        decay = jnp.exp2(pair_tot)
        for p in range(pairs_per_tile):
            r0 = ti * tile + p * pair
            ctx_terms[(d.fwd, ti * pairs_per_tile + p)] = (
                _dot(kt_t[:, p * pair:(p + 1) * pair], vc_ref[r0:r0 + pair, :]),
                _lane_broadcast_column(decay[p]))
    for d in dirs:
        order = range(ctx_tiles * pairs_per_tile)
        s = jnp.zeros((dk, dv), F32)
        for p in (order if d.fwd else reversed(order)):
            inc, dm = ctx_terms[(d.fwd, p)]
            s = s * jnp.concatenate([dm] * (dv // dk), axis=1) + inc
        d.s_scr[...] = s

    def local_body(i, carry):
        tiles = [i * GLA_LOCAL_UNROLL + u for u in range(GLA_LOCAL_UNROLL)]
        rows = [pl.ds(pl.multiple_of(ti * tile, tile), tile) for ti in tiles]
        items = [(u, d) for u in range(GLA_LOCAL_UNROLL) for d in dirs]
        g_tiles = [d.g_ref[rows[u], :] for u, d in items]
        cums = [_tile_decays(g, d.chunk_sum, d.fwd, grp) for (u, d), g in zip(items, g_tiles)]
        q2 = [q_ref[sl, :].astype(F32) for sl in rows]
        k2 = [k_ref[sl, :].astype(F32) for sl in rows]
        pair_terms = []
        for (u, d), (b, tot) in zip(items, cums):
            q_off, k_off, pair_tot = _pair_offsets(tot, d.fwd)
            q3, k3 = q2[u].reshape(grp, c, dk), k2[u].reshape(grp, c, dk)
            qb = q3 * jnp.exp2(b)
            kt, kt_pair = key_side(k3, b, tot, k_off)
            pair_terms.append(_nt_dot(qb.reshape(tile, dk).astype(BF16), kt.reshape(tile, dk).astype(BF16)))
            d.qb_scr[rows[u], :] = (qb * jnp.exp2(q_off)).reshape(tile, dk).astype(BF16)
            kt_t = kt_pair.reshape(tile, dk).T.astype(BF16)
            decay = jnp.exp2(pair_tot)
            for p in range(pairs_per_tile):
                d.kt_scr[tiles[u] * pairs_per_tile + p] = kt_t[:, p * pair:(p + 1) * pair]
                d.dm_scr[tiles[u] * pairs_per_tile + p] = _lane_broadcast_column(decay[p])

        def store_scores(per_item):
            for u in range(GLA_LOCAL_UNROLL):
                scores = functools.reduce(lambda x, y: x + y, [s for (iu, _), s in zip(items, per_item) if iu == u])
                for p in range(pairs_per_tile):
                    a_scr[pl.ds(pl.multiple_of(tiles[u] * tile + p * pair, pair), pair), :] = (
                        scores[p * pair:(p + 1) * pair, p * pair:(p + 1) * pair])

        def chunk_scores_mild():
            products = []
            for (u, d), (b, _) in zip(items, cums):
                flat = b.reshape(tile, dk)
                mid = _block_row(flat, c, c // 2 - 1 if d.fwd else c // 2)
                products.append(_nt_dot((q2[u] * jnp.exp2(flat - mid)).astype(BF16),
                                        (k2[u] * jnp.exp2(mid - flat)).astype(BF16)))
            store_scores([jnp.where(d.chunk_mask, inner, 0.0).astype(BF16)
                          + outer.astype(BF16) * d.split_masks[0][1]
                          for (_, d), inner, outer in zip(items, products, pair_terms)])

        def chunk_scores_any():
            diag_cums = [_block_sums(d.diag_sum, jnp.maximum(g, GLA_DIAG_DECAY_FLOOR))
                         for (u, d), g in zip(items, g_tiles)]
            products = []
            for (u, d), (b, _), bd in zip(items, cums, diag_cums):
                flat = b.reshape(tile, dk)
                split = []
                for h, mask in d.split_masks[1:]:
                    e = jnp.exp2(-jnp.abs(flat - _block_row(flat, 2 * h, h - 1 if d.fwd else h)))
                    split.append((_nt_dot((q2[u] * e).astype(BF16), (k2[u] * e).astype(BF16)), mask))
                mid = _block_row(bd, diag, diag // 2 - 1 if d.fwd else diag // 2)
                inner = _nt_dot((q2[u] * jnp.exp2(bd - mid)).astype(BF16),
                                (k2[u] * jnp.exp2(mid - bd)).astype(BF16))
                products.append((split, inner))
            per_item = []
            for (_, d), (split, inner), outer in zip(items, products, pair_terms):
                part = jnp.where(d.diag_mask, inner, 0.0).astype(BF16) + outer.astype(BF16) * d.split_masks[0][1]
                for product, mask in split:
                    part = part + product.astype(BF16) * mask
                per_item.append(part)
            store_scores(per_item)

        lowest = jnp.min(functools.reduce(jnp.minimum, g_tiles))
        pl.when(lowest >= GLA_MILD_DECAY_FLOOR)(chunk_scores_mild)
        pl.when(lowest < GLA_MILD_DECAY_FLOOR)(chunk_scores_any)
        return carry

    lax.fori_loop(0, t // (tile * GLA_LOCAL_UNROLL), local_body, 0)

    def scan_body(i, carry, finish):
        steps = []
        for u in range(GLA_SCAN_UNROLL):
            for d in dirs:
                step = i * GLA_SCAN_UNROLL + u
                j = step if d.fwd else n_pairs - 1 - step
                steps.append((d, j, pl.ds(pl.multiple_of(j * pair, pair), pair)))
        increments = [_dot(d.kt_scr[j], v_ref[sl, :]) for d, j, sl in steps]
        local = [None if finish else _dot(a_scr[sl, :], v_ref[sl, :]) for _, _, sl in steps]
        outs = []
        for (d, j, sl), inc in zip(steps, increments):
            qb_scr, dm_scr, s_scr = d.qb_scr, d.dm_scr, d.s_scr
            s = s_scr[...]
            outs.append(_dot(qb_scr[sl, :], s.astype(BF16)))
            s_scr[...] = s * jnp.concatenate([dm_scr[j]] * (dv // dk), axis=1) + inc
        for (_, _, sl), o, loc in zip(steps, outs, local):
            if finish:
                y = _rmsnorm(o + o_scr[sl, :], gn_ref[...])
                out_ref[sl, :] = (y * sr_ref[sl, :].astype(F32)).astype(BF16)
            else:
                o_scr[sl, :] = o + loc
        return carry

    half = n_pairs // (2 * GLA_SCAN_UNROLL)
    lax.fori_loop(0, half, functools.partial(scan_body, finish=False), 0)
    lax.fori_loop(half, 2 * half, functools.partial(scan_body, finish=True), 0)


def _gla_call(q, k, v, sr, gf, gb, kc, vc, gfc, gbc, gn):
    bsz, t, qk = q.shape
    d = v.shape[2]
    tc = kc.shape[1]
    dk = qk // GLA_HEADS
    dv = d // GLA_HEADS
    pair = GLA_KERNEL_CHUNK * GLA_PAIR
    n_pairs = t // pair
    tile = GLA_KERNEL_CHUNK * GLA_GROUP
    seq = lambda rows, width: pl.BlockSpec((None, rows, width), lambda b, h: (b, 0, h))
    scratch_shapes = [
        pltpu.VMEM((t, pair), BF16),
        pltpu.VMEM((t, dv), F32),
        pltpu.VMEM((n_pairs, dk, pair), BF16), pltpu.VMEM((n_pairs, dk, pair), BF16),
        pltpu.VMEM((t, dk), BF16), pltpu.VMEM((t, dk), BF16),
        pltpu.VMEM((n_pairs, dk, dk), F32), pltpu.VMEM((n_pairs, dk, dk), F32),
        pltpu.VMEM((dk, dv), F32), pltpu.VMEM((dk, dv), F32),
    ]
    pipelined = (2 * _nbytes((t, dk), BF16) + 3 * _nbytes((t, dv), BF16) + 2 * _nbytes((t, dk), F32)
                 + _nbytes((tc, dk), BF16) + _nbytes((tc, dv), BF16) + 2 * _nbytes((tc, dk), F32)
                 + _nbytes((1, dv), F32))
    scratch = (_nbytes((t, pair), BF16) + _nbytes((t, dv), F32) + 2 * _nbytes((n_pairs, dk, pair), BF16)
               + 2 * _nbytes((t, dk), BF16) + 2 * _nbytes((n_pairs, dk, dk), F32) + 2 * _nbytes((dk, dv), F32))
    temps = 2 * GLA_LOCAL_UNROLL * (8 * _nbytes((tile, dk), F32) + 2 * _nbytes((tile, tile), F32))
    return pl.pallas_call(
        _gla_kernel,
        out_shape=jax.ShapeDtypeStruct((bsz, t, d), BF16),
        grid=(bsz, GLA_HEADS),
        in_specs=[seq(t, dk), seq(t, dk), seq(t, dv), seq(t, dv), seq(t, dk), seq(t, dk),
                  seq(tc, dk), seq(tc, dv), seq(tc, dk), seq(tc, dk),
                  pl.BlockSpec((1, dv), lambda b, h: (0, h))],
        out_specs=seq(t, dv),
        scratch_shapes=scratch_shapes,
        compiler_params=pltpu.CompilerParams(
            dimension_semantics=("parallel", "parallel"),
            vmem_limit_bytes=_vmem_limit(pipelined, scratch, temps),
        ),
        name="gla",
    )(q, k, v, sr, gf, gb, kc, vc, gfc, gbc, gn)


def _out_kernel(x_ref, h1_ref, ya_ref, yb_ref, ga1_ref, sh2_ref, sc2_ref, ga2_ref, g2_ref, gfin_ref,
                wg_ref, bg_ref, wco_ref, wgo_ref, wo_ref, wup_ref, wdn_ref, o_ref, y_scr, h2_scr):
    tm, d = x_ref.shape
    d_ff = wup_ref.shape[1]

    def merge_stage(j):
        cs = slice(j, j + COL_BLOCK)
        h1 = h1_ref[...]
        za = _dot(h1, wg_ref[:, j:j + COL_BLOCK]) + bg_ref[:, j:j + COL_BLOCK]
        zb = _dot(h1, wg_ref[:, d + j:d + j + COL_BLOCK]) + bg_ref[:, d + j:d + j + COL_BLOCK]
        pa = _dot(ya_ref[...], wco_ref[:, cs])
        pb = _dot(yb_ref[...], wgo_ref[:, cs])

        def epilogue():
            y_scr[:, cs] = (_sigmoid(za) * pa + _sigmoid(zb) * pb).astype(BF16)
        return epilogue

    _run_skewed([functools.partial(merge_stage, j) for j in range(0, d, COL_BLOCK)])
    o_ref[...] = x_ref[...] + ga1_ref[...] * _dot(y_scr[...], wo_ref[...])
    h2_scr[...] = _modulate(o_ref[...], g2_ref[...], sh2_ref[...], sc2_ref[...]).astype(BF16)

    parts = []

    def mlp_stage(j):
        u = _dot(h2_scr[...], wup_ref[:, j:j + FF_BLOCK])

        def epilogue():
            a = jnp.maximum(u, 0.0)
            parts.append(_dot((a * a).astype(BF16), wdn_ref[j:j + FF_BLOCK, :]))
        return epilogue

    _run_skewed([functools.partial(mlp_stage, j) for j in range(0, d_ff, FF_BLOCK)])
    x2 = o_ref[...] + ga2_ref[...] * functools.reduce(lambda a, b: a + b, parts)
    o_ref[...] = _rmsnorm(x2, gfin_ref[...])


def _out_call(x, h1, ya, yb, mod3, g2, gfin, wg, bg, wco, wgo, wo, wup, wdn):
    bsz, t, d = x.shape
    tm = OUT_TM
    tok = lambda: pl.BlockSpec((None, tm, d), lambda b, i: (b, i, 0))
    modrow = lambda col: pl.BlockSpec((None, 1, d), lambda b, i: (b, 0, col))
    pipelined = 2 * _nbytes((tm, d), F32) + 3 * _nbytes((tm, d), BF16) + 4 * _nbytes((1, d), F32)
    resident = sum(_nbytes(a.shape, a.dtype) for a in (g2, gfin, wg, bg, wco, wgo, wo, wup, wdn))
    scratch = 2 * _nbytes((tm, d), BF16)
    temps = 10 * _nbytes((tm, d), F32)
    return pl.pallas_call(
        _out_kernel,
        out_shape=jax.ShapeDtypeStruct((bsz, t, d), F32),
        grid=(bsz, t // tm),
        in_specs=[tok(), tok(), tok(), tok()] + [modrow(cidx) for cidx in (2, 3, 4, 5)] + [_resident()] * 9,
        out_specs=tok(),
        scratch_shapes=[pltpu.VMEM((tm, d), BF16), pltpu.VMEM((tm, d), BF16)],
        compiler_params=pltpu.CompilerParams(
            dimension_semantics=("parallel", "parallel"),
            vmem_limit_bytes=_vmem_limit(pipelined, resident + scratch, temps),
        ),
        name="out",
    )(x, h1, ya, yb, mod3, mod3, mod3, mod3, g2, gfin, wg, bg, wco, wgo, wo, wup, wdn)


def kernel(x, c, ctx, c_ctx, w_ada, b_ada, g_norm1, w_in, b_in, conv_w, w_conv_out, w_a2_f, b_a_f,
           w_a2_b, b_a_b, g_gla_norm, w_gla_out, w_o, g_norm2, w_up, w_down, g_final):
    depth = w_ada.shape[0]
    assert depth == 1, "only the single-layer block is implemented"
    bsz, t, d = x.shape
    qk = w_a2_f.shape[2]
    dk = qk // GLA_HEADS
    rank = w_a2_f.shape[1]
    gla_tile = GLA_GROUP * GLA_KERNEL_CHUNK
    assert t % INPROJ_TM == 0 and t % OUT_TM == 0 and INPROJ_TM % GRID_W == 0
    assert GLA_PAIR == 2 and GLA_GROUP % GLA_PAIR == 0 and GLA_PAIR * GLA_KERNEL_CHUNK == dk
    gla_pair = GLA_PAIR * GLA_KERNEL_CHUNK
    assert t % (GLA_LOCAL_UNROLL * gla_tile) == 0 and ctx.shape[1] % gla_tile == 0
    assert t % (2 * GLA_SCAN_UNROLL * gla_pair) == 0
    assert bsz + 1 <= MOD_ROWS and 2 * rank <= LR_PAD

    cc = jnp.zeros((MOD_ROWS, d), F32).at[:bsz].set(c).at[bsz].set(c_ctx)
    mod = _ada_call(cc, w_ada[0], b_ada[0][None, :])
    mod3 = mod.reshape(MOD_ROWS, 1, N_MOD * d)

    w = w_in[0]
    bias = b_in[0][None, :]
    o_q = 3 * d
    o_k = o_q + qk
    o_v = o_k + qk
    o_r = o_v + d
    o_lr = o_r + d
    o_g = o_lr + 2 * rank
    lr_pad = LR_PAD - 2 * rank
    w1 = w[:, :o_q].astype(BF16)
    b1 = bias[:, :o_q]
    w_lr = jnp.pad(w[:, o_lr:o_g], ((0, 0), (0, lr_pad)))
    b_lr = jnp.pad(bias[:, o_lr:o_g], ((0, 0), (0, lr_pad)))
    w2 = jnp.concatenate([w[:, o_q:o_lr], w_lr], axis=1).astype(BF16)
    b2 = jnp.concatenate([bias[:, o_q:o_lr], b_lr], axis=1)
    w2c = jnp.concatenate([w[:, o_k:o_r], w_lr], axis=1).astype(BF16)
    b2c = jnp.concatenate([bias[:, o_k:o_r], b_lr], axis=1)
    wg = w[:, o_g:].astype(BF16)
    bg = bias[:, o_g:]
    wa2 = jnp.zeros((LR_PAD, 2 * qk), F32)
    wa2 = wa2.at[:rank, :qk].set(w_a2_f[0]).at[rank:2 * rank, qk:].set(w_a2_b[0]).astype(BF16)
    ba2 = jnp.concatenate([b_a_f[0], b_a_b[0]])[None, :]
    g1 = g_norm1[0][None, :]

    h1, ya, q, k, v, sr, gf, gb = _inproj_call(x, mod3, g1, w1, b1, conv_w[0], w2, b2, wa2, ba2, qk, dk)
    kc, vc, gfc, gbc = _inproj_ctx_call(ctx, mod3, bsz, g1, w2c, b2c, wa2, ba2, qk)
    yb = _gla_call(q, k, v, sr, gf, gb, kc, vc, gfc, gbc, g_gla_norm[0][None, :])
    return _out_call(x, h1, ya, yb, mod3, g_norm2[0][None, :], g_final[None, :], wg, bg,
                     w_conv_out[0].astype(BF16), w_gla_out[0].astype(BF16), w_o[0].astype(BF16),
                     w_up[0].astype(BF16), w_down[0].astype(BF16))
```

```python
import functools
from typing import NamedTuple

import jax
import jax.numpy as jnp
from jax import lax
from jax.experimental import pallas as pl
from jax.experimental.pallas import tpu as pltpu

F32 = jnp.float32
BF16 = jnp.bfloat16
HIGHEST = lax.Precision.HIGHEST

GLA_HEADS = 4
GLA_TAU = 16.0
GRID_W = 64
N_MOD = 6
RMS_EPS = 1e-6
GLA_RANK = 16
LOG2_E = 1.4426950408889634

LR_PAD = 128
GLA_KERNEL_CHUNK = 64
GLA_DIAG_BLOCK = 8
GLA_DIAG_DECAY_FLOOR = -31.0
GLA_MILD_DECAY_FLOOR = -126.0 / (GLA_KERNEL_CHUNK // 2)
GLA_PAIR = 2
GLA_GROUP = 4
GLA_LOCAL_UNROLL = 4
GLA_SCAN_UNROLL = 8
INPROJ_TM = 1024
OUT_TM = 512
COL_BLOCK = 256
FF_BLOCK = 1024
MOD_ROWS = 16
V7X_VMEM_LIMIT_BYTES = 60000 * 1024


def _vmem_limit(pipelined_bytes, resident_bytes, temp_bytes):
    need = 2 * pipelined_bytes + resident_bytes + temp_bytes
    return int(min(V7X_VMEM_LIMIT_BYTES, need))


def _nbytes(shape, dtype):
    n = 1
    for s in shape:
        n *= s
    return n * jnp.dtype(dtype).itemsize


def _dot(a, b):
    return jnp.dot(a, b, preferred_element_type=F32)


def _nt_dot(a, b):
    return lax.dot_general(a, b, (((1,), (1,)), ((), ())), preferred_element_type=F32)


def _run_skewed(stages):
    pending = None
    for stage in stages:
        epilogue = stage()
        if pending is not None:
            pending()
        pending = epilogue
    if pending is not None:
        pending()


def _sigmoid(x):
    return 1.0 / (1.0 + jnp.exp(-x))


def _log_sigmoid(x):
    return jnp.minimum(x, 0.0) - jnp.log(1.0 + jnp.exp(-jnp.abs(x)))


def _rmsnorm(x, g):
    return x * lax.rsqrt(jnp.mean(x * x, axis=-1, keepdims=True) + RMS_EPS) * g


def _modulate(x, g, shift, scale):
    return _rmsnorm(x, g) * (1.0 + scale) + shift


def _ada_kernel(c_ref, w_ref, b_ref, o_ref):
    c = c_ref[...]
    s = c * _sigmoid(c)
    o_ref[...] = jnp.dot(s, w_ref[...], precision=HIGHEST, preferred_element_type=F32) + b_ref[...]


def _ada_call(cc, w_ada, b_ada):
    d = cc.shape[1]
    n_out = w_ada.shape[1]
    return pl.pallas_call(
        _ada_kernel,
        out_shape=jax.ShapeDtypeStruct((MOD_ROWS, n_out), F32),
        grid=(n_out // d,),
        in_specs=[
            pl.BlockSpec((MOD_ROWS, d), lambda j: (0, 0)),
            pl.BlockSpec((d, d), lambda j: (0, j)),
            pl.BlockSpec((1, d), lambda j: (0, j)),
        ],
        out_specs=pl.BlockSpec((MOD_ROWS, d), lambda j: (0, j)),
        compiler_params=pltpu.CompilerParams(
            dimension_semantics=("arbitrary",),
            vmem_limit_bytes=_vmem_limit(_nbytes((d, d), F32) + _nbytes((MOD_ROWS, 2 * d), F32),
                                         0, 4 * _nbytes((d, d), F32)),
        ),
        name="ada",
    )(cc, w_ada, b_ada)


def _log_decays(lr, wa2_ref, ba2_ref):
    xg = _dot(lr.astype(BF16), wa2_ref[...]) + ba2_ref[...]
    return _log_sigmoid(xg) * (LOG2_E / GLA_TAU)


def _inproj_kernel(x_ref, sh_ref, sc_ref, g1_ref, w_ref, b_ref, cw_ref, wa2_ref, ba2_ref,
                   h_ref, ya_ref, q_ref, k_ref, v_ref, sr_ref, gf_ref, gb_ref, *, q_scale):
    tm, d = x_ref.shape
    qk = q_ref.shape[1]
    h_ref[...] = _modulate(x_ref[...], g1_ref[...], sh_ref[...], sc_ref[...]).astype(BF16)

    col_in_row = lax.broadcasted_iota(jnp.int32, (tm, 1), 0) % GRID_W
    has_left = col_in_row != 0
    has_right = col_in_row != GRID_W - 1

    def proj(c0, width):
        return _dot(h_ref[...], w_ref[:, c0:c0 + width]) + b_ref[:, c0:c0 + width]

    o_q = 3 * d
    o_v = o_q + 2 * qk
    o_r = o_v + d
    o_lr = o_r + d

    def decay_stage():
        lr = proj(o_lr, LR_PAD)

        def epilogue():
            g = _log_decays(lr, wa2_ref, ba2_ref)
            gf_ref[...] = g[:, 0:qk]
            gb_ref[...] = g[:, qk:2 * qk]
        return epilogue

    def conv_stage(j):
        cs = slice(j, j + COL_BLOCK)
        xa = proj(j, COL_BLOCK)
        ba = proj(d + j, COL_BLOCK)
        ca = proj(2 * d + j, COL_BLOCK)

        def epilogue():
            u = ca * xa
            left = jnp.where(has_left, pltpu.roll(u, 1, 0), 0.0)
            right = jnp.where(has_right, pltpu.roll(u, tm - 1, 0), 0.0)
            y = left * cw_ref[0:1, cs] + u * cw_ref[1:2, cs] + right * cw_ref[2:3, cs]
            ya_ref[:, cs] = (ba * y).astype(BF16)
        return epilogue

    def swish_stage(j):
        r = proj(o_r + j, COL_BLOCK)

        def epilogue():
            sr_ref[:, j:j + COL_BLOCK] = (r * _sigmoid(r)).astype(BF16)
        return epilogue

    def qk_stage():
        q = proj(o_q, qk)
        k = proj(o_q + qk, qk)

        def epilogue():
            q_ref[...] = (q * q_scale).astype(BF16)
            k_ref[...] = k.astype(BF16)
        return epilogue

    def value_stage(j):
        v = proj(o_v + j, COL_BLOCK)

        def epilogue():
            v_ref[:, j:j + COL_BLOCK] = v.astype(BF16)
        return epilogue

    _run_skewed([decay_stage] + [functools.partial(conv_stage, j) for j in range(0, d, COL_BLOCK)]
                + [functools.partial(swish_stage, j) for j in range(0, d, COL_BLOCK)] + [qk_stage]
                + [functools.partial(value_stage, j) for j in range(0, d, COL_BLOCK)])


def _inproj_ctx_kernel(x_ref, sh_ref, sc_ref, g1_ref, w_ref, b_ref, wa2_ref, ba2_ref,
                       k_ref, v_ref, gf_ref, gb_ref):
    d = x_ref.shape[1]
    qk = k_ref.shape[1]
    h = _modulate(x_ref[...], g1_ref[...], sh_ref[...], sc_ref[...]).astype(BF16)
    o_k = 3 * d + qk
    k_ref[...] = (_dot(h, w_ref[:, o_k:o_k + qk]) + b_ref[:, o_k:o_k + qk]).astype(BF16)
    for j in range(0, d, COL_BLOCK):
        c0 = o_k + qk + j
        v_ref[:, j:j + COL_BLOCK] = (_dot(h, w_ref[:, c0:c0 + COL_BLOCK]) + b_ref[:, c0:c0 + COL_BLOCK]).astype(BF16)
    c0 = o_k + qk + 2 * d
    lr = _dot(h, w_ref[:, c0:c0 + LR_PAD]) + b_ref[:, c0:c0 + LR_PAD]
    g = _log_decays(lr, wa2_ref, ba2_ref)
    gf_ref[...] = g[:, 0:qk]
    gb_ref[...] = g[:, qk:2 * qk]


def _resident():
    return pl.BlockSpec(memory_space=pltpu.VMEM)


def _inproj_call(x, mod3, g1, w, b, cw, wa2, ba2, qk, dk):
    bsz, t, d = x.shape
    tm = INPROJ_TM
    tok = lambda n: pl.BlockSpec((None, tm, n), lambda b, i: (b, i, 0))
    modrow = lambda col: pl.BlockSpec((None, 1, d), lambda b, i: (b, 0, col))
    out_shapes = (
        jax.ShapeDtypeStruct((bsz, t, d), BF16),
        jax.ShapeDtypeStruct((bsz, t, d), BF16),
        jax.ShapeDtypeStruct((bsz, t, qk), BF16),
        jax.ShapeDtypeStruct((bsz, t, qk), BF16),
        jax.ShapeDtypeStruct((bsz, t, d), BF16),
        jax.ShapeDtypeStruct((bsz, t, d), BF16),
        jax.ShapeDtypeStruct((bsz, t, qk), F32),
        jax.ShapeDtypeStruct((bsz, t, qk), F32),
    )
    pipelined = (_nbytes((tm, d), F32) + 4 * _nbytes((tm, d), BF16) + 2 * _nbytes((tm, qk), BF16)
                 + 2 * _nbytes((tm, qk), F32) + 2 * _nbytes((1, d), F32))
    resident = sum(_nbytes(a.shape, a.dtype) for a in (g1, w, b, cw, wa2, ba2))
    temps = 24 * _nbytes((tm, COL_BLOCK), F32) + 3 * _nbytes((tm, 2 * qk), F32)
    return pl.pallas_call(
        functools.partial(_inproj_kernel, q_scale=dk ** -0.5),
        out_shape=out_shapes,
        grid=(bsz, t // tm),
        in_specs=[tok(d), modrow(0), modrow(1)] + [_resident()] * 6,
        out_specs=(tok(d), tok(d), tok(qk), tok(qk), tok(d), tok(d), tok(qk), tok(qk)),
        compiler_params=pltpu.CompilerParams(
            dimension_semantics=("parallel", "parallel"),
            vmem_limit_bytes=_vmem_limit(pipelined, resident, temps),
        ),
        name="inproj",
    )(x, mod3, mod3, g1, w, b, cw, wa2, ba2)


def _inproj_ctx_call(ctx, mod3, ctx_row, g1, w, b, wa2, ba2, qk):
    bsz, tc, d = ctx.shape
    tok = lambda n: pl.BlockSpec((None, tc, n), lambda b: (b, 0, 0))
    modrow = lambda col: pl.BlockSpec((None, 1, d), lambda b: (ctx_row, 0, col))
    out_shapes = (
        jax.ShapeDtypeStruct((bsz, tc, qk), BF16),
        jax.ShapeDtypeStruct((bsz, tc, d), BF16),
        jax.ShapeDtypeStruct((bsz, tc, qk), F32),
        jax.ShapeDtypeStruct((bsz, tc, qk), F32),
    )
    pipelined = (_nbytes((tc, d), F32) + _nbytes((tc, d), BF16) + _nbytes((tc, qk), BF16)
                 + 2 * _nbytes((tc, qk), F32) + 2 * _nbytes((1, d), F32))
    resident = sum(_nbytes(a.shape, a.dtype) for a in (g1, w, b, wa2, ba2))
    temps = _nbytes((tc, d), BF16) + 6 * _nbytes((tc, COL_BLOCK), F32) + 3 * _nbytes((tc, 2 * qk), F32)
    return pl.pallas_call(
        _inproj_ctx_kernel,
        out_shape=out_shapes,
        grid=(bsz,),
        in_specs=[tok(d), modrow(0), modrow(1)] + [_resident()] * 5,
        out_specs=(tok(qk), tok(d), tok(qk), tok(qk)),
        compiler_params=pltpu.CompilerParams(
            dimension_semantics=("parallel",),
            vmem_limit_bytes=_vmem_limit(pipelined, resident, temps),
        ),
        name="inproj_ctx",
    )(ctx, mod3, mod3, g1, w, b, wa2, ba2)


class _ScanDir(NamedTuple):
    fwd: bool
    g_ref: object
    gc_ref: object
    chunk_sum: object
    diag_sum: object
    split_masks: tuple
    chunk_mask: object
    diag_mask: object
    kt_scr: object
    qb_scr: object
    dm_scr: object
    s_scr: object


def _block_sums(sum_matrix, g):
    dk = g.shape[1]
    g_hi = g.astype(BF16)
    g_lo = (g - g_hi.astype(F32)).astype(BF16)
    bb = _dot(sum_matrix, jnp.concatenate([g_hi, g_lo], axis=1))
    return bb[:, :dk] + bb[:, dk:]


def _tile_decays(g, sum_matrix, fwd, n_chunks):
    rows, dk = g.shape
    c = rows // n_chunks
    b = _block_sums(sum_matrix, g).reshape(n_chunks, c, dk)
    return b, (b[:, c - 1:c, :] if fwd else b[:, 0:1, :])


def _block_row(x, block, row):
    rows, n = x.shape
    x3 = x.reshape(rows // block, block, n)
    return jnp.broadcast_to(x3[:, row:row + 1, :], x3.shape).reshape(rows, n)


def _pair_offsets(tot, fwd):
    n_chunks = tot.shape[0]
    zero = jnp.zeros_like(tot[0:1])
    on_odd = jnp.concatenate([t for p in range(0, n_chunks, GLA_PAIR) for t in (zero, tot[p:p + 1])], axis=0)
    on_even = jnp.concatenate([t for p in range(0, n_chunks, GLA_PAIR) for t in (tot[p + 1:p + 2], zero)], axis=0)
    pair_tot = jnp.concatenate([tot[p:p + 1] + tot[p + 1:p + 2] for p in range(0, n_chunks, GLA_PAIR)], axis=0)
    return (on_odd, on_even, pair_tot) if fwd else (on_even, on_odd, pair_tot)


def _lane_broadcast_column(row):
    n = row.shape[1]
    return jnp.broadcast_to(row, (n, n)).T


def _gla_kernel(q_ref, k_ref, v_ref, sr_ref, gf_ref, gb_ref, kc_ref, vc_ref, gfc_ref, gbc_ref, gn_ref,
                out_ref, a_scr, o_scr, ktf_scr, ktb_scr, qbf_scr, qbb_scr, dmf_scr, dmb_scr, sf_scr, sb_scr):
    c = GLA_KERNEL_CHUNK
    grp = GLA_GROUP
    tile = c * grp
    pair = c * GLA_PAIR
    pairs_per_tile = grp // GLA_PAIR
    diag = GLA_DIAG_BLOCK
    t, dk = q_ref.shape
    dv = v_ref.shape[1]
    n_pairs = t // pair
    row = lax.broadcasted_iota(jnp.int32, (tile, tile), 0)
    col = lax.broadcasted_iota(jnp.int32, (tile, tile), 1)
    halves = []
    half = pair // 2
    while half >= diag:
        halves.append(half)
        half //= 2

    def same_block(size):
        return (row // size) == (col // size)

    def scan_dir(fwd, g_ref, gc_ref, kt_scr, qb_scr, dm_scr, s_scr):
        seen = (row >= col) if fwd else (row <= col)
        split_masks = []
        for h in halves:
            later, earlier = (row % (2 * h) >= h), (col % (2 * h) < h)
            if not fwd:
                later, earlier = (row % (2 * h) < h), (col % (2 * h) >= h)
            split_masks.append((h, (same_block(2 * h) & later & earlier).astype(BF16)))
        return _ScanDir(fwd, g_ref, gc_ref, (same_block(c) & seen).astype(BF16),
                        (same_block(diag) & seen).astype(BF16), tuple(split_masks),
                        same_block(c) & seen, same_block(diag) & seen, kt_scr, qb_scr, dm_scr, s_scr)

    dirs = (scan_dir(True, gf_ref, gfc_ref, ktf_scr, qbf_scr, dmf_scr, sf_scr),
            scan_dir(False, gb_ref, gbc_ref, ktb_scr, qbb_scr, dmb_scr, sb_scr))

    def key_side(k3, b, tot, k_off):
        kt = k3 * jnp.exp2(tot - b)
        return kt, kt * jnp.exp2(k_off)

    ctx_tiles = kc_ref.shape[0] // tile
    ctx_items = [(d, ti) for d in dirs for ti in range(ctx_tiles)]
    ctx_cums = [_tile_decays(d.gc_ref[ti * tile:(ti + 1) * tile, :], d.chunk_sum, d.fwd, grp)
                for d, ti in ctx_items]
    ctx_terms = {}
    for (d, ti), (b, tot) in zip(ctx_items, ctx_cums):
        _, k_off, pair_tot = _pair_offsets(tot, d.fwd)
        kc3 = kc_ref[ti * tile:(ti + 1) * tile, :].astype(F32).reshape(grp, c, dk)
        _, kt_pair = key_side(kc3, b, tot, k_off)
        kt_t = kt_pair.reshape(tile, dk).T.astype(BF16)
        decay = jnp.exp2(pair_tot)
        for p in range(pairs_per_tile):
            r0 = ti * tile + p * pair
            ctx_terms[(d.fwd, ti * pairs_per_tile + p)] = (
                _dot(kt_t[:, p * pair:(p + 1) * pair], vc_ref[r0:r0 + pair, :]),
                _lane_broadcast_column(decay[p]))
    for d in dirs:
        order = range(ctx_tiles * pairs_per_tile)
        s = jnp.zeros((dk, dv), F32)
        for p in (order if d.fwd else reversed(order)):
            inc, dm = ctx_terms[(d.fwd, p)]
            s = s * jnp.concatenate([dm] * (dv // dk), axis=1) + inc
        d.s_scr[...] = s

    def local_body(i, carry):
        tiles = [i * GLA_LOCAL_UNROLL + u for u in range(GLA_LOCAL_UNROLL)]
        rows = [pl.ds(pl.multiple_of(ti * tile, tile), tile) for ti in tiles]
        items = [(u, d) for u in range(GLA_LOCAL_UNROLL) for d in dirs]
        g_tiles = [d.g_ref[rows[u], :] for u, d in items]
        cums = [_tile_decays(g, d.chunk_sum, d.fwd, grp) for (u, d), g in zip(items, g_tiles)]
        q2 = [q_ref[sl, :].astype(F32) for sl in rows]
        k2 = [k_ref[sl, :].astype(F32) for sl in rows]
        pair_terms = []
        for (u, d), (b, tot) in zip(items, cums):
            q_off, k_off, pair_tot = _pair_offsets(tot, d.fwd)
            q3, k3 = q2[u].reshape(grp, c, dk), k2[u].reshape(grp, c, dk)
            qb = q3 * jnp.exp2(b)
            kt, kt_pair = key_side(k3, b, tot, k_off)
            pair_terms.append(_nt_dot(qb.reshape(tile, dk).astype(BF16), kt.reshape(tile, dk).astype(BF16)))
            d.qb_scr[rows[u], :] = (qb * jnp.exp2(q_off)).reshape(tile, dk).astype(BF16)
            kt_t = kt_pair.reshape(tile, dk).T.astype(BF16)
            decay = jnp.exp2(pair_tot)
            for p in range(pairs_per_tile):
                d.kt_scr[tiles[u] * pairs_per_tile + p] = kt_t[:, p * pair:(p + 1) * pair]
                d.dm_scr[tiles[u] * pairs_per_tile + p] = _lane_broadcast_column(decay[p])

        def store_scores(per_item):
            for u in range(GLA_LOCAL_UNROLL):
                scores = functools.reduce(lambda x, y: x + y, [s for (iu, _), s in zip(items, per_item) if iu == u])
                for p in range(pairs_per_tile):
                    a_scr[pl.ds(pl.multiple_of(tiles[u] * tile + p * pair, pair), pair), :] = (
                        scores[p * pair:(p + 1) * pair, p * pair:(p + 1) * pair])

        def chunk_scores_mild():
            products = []
            for (u, d), (b, _) in zip(items, cums):
                flat = b.reshape(tile, dk)
                mid = _block_row(flat, c, c // 2 - 1 if d.fwd else c // 2)
                products.append(_nt_dot((q2[u] * jnp.exp2(flat - mid)).astype(BF16),
                                        (k2[u] * jnp.exp2(mid - flat)).astype(BF16)))
            store_scores([jnp.where(d.chunk_mask, inner, 0.0).astype(BF16)
                          + outer.astype(BF16) * d.split_masks[0][1]
                          for (_, d), inner, outer in zip(items, products, pair_terms)])

        def chunk_scores_any():
            diag_cums = [_block_sums(d.diag_sum, jnp.maximum(g, GLA_DIAG_DECAY_FLOOR))
                         for (u, d), g in zip(items, g_tiles)]
            products = []
            for (u, d), (b, _), bd in zip(items, cums, diag_cums):
                flat = b.reshape(tile, dk)
                split = []
                for h, mask in d.split_masks[1:]:
                    e = jnp.exp2(-jnp.abs(flat - _block_row(flat, 2 * h, h - 1 if d.fwd else h)))
                    split.append((_nt_dot((q2[u] * e).astype(BF16), (k2[u] * e).astype(BF16)), mask))
                mid = _block_row(bd, diag, diag // 2 - 1 if d.fwd else diag // 2)
                inner = _nt_dot((q2[u] * jnp.exp2(bd - mid)).astype(BF16),
                                (k2[u] * jnp.exp2(mid - bd)).astype(BF16))
                products.append((split, inner))
            per_item = []
            for (_, d), (split, inner), outer in zip(items, products, pair_terms):
                part = jnp.where(d.diag_mask, inner, 0.0).astype(BF16) + outer.astype(BF16) * d.split_masks[0][1]
                for product, mask in split:
                    part = part + product.astype(BF16) * mask
                per_item.append(part)
            store_scores(per_item)

        lowest = jnp.min(functools.reduce(jnp.minimum, g_tiles))
        pl.when(lowest >= GLA_MILD_DECAY_FLOOR)(chunk_scores_mild)
        pl.when(lowest < GLA_MILD_DECAY_FLOOR)(chunk_scores_any)
        return carry

    lax.fori_loop(0, t // (tile * GLA_LOCAL_UNROLL), local_body, 0)

    def scan_body(i, carry, finish):
        steps = []
        for u in range(GLA_SCAN_UNROLL):
            for d in dirs:
                step = i * GLA_SCAN_UNROLL + u
                j = step if d.fwd else n_pairs - 1 - step
                steps.append((d, j, pl.ds(pl.multiple_of(j * pair, pair), pair)))
        increments = [_dot(d.kt_scr[j], v_ref[sl, :]) for d, j, sl in steps]
        local = [None if finish else _dot(a_scr[sl, :], v_ref[sl, :]) for _, _, sl in steps]
        outs = []
        for (d, j, sl), inc in zip(steps, increments):
            qb_scr, dm_scr, s_scr = d.qb_scr, d.dm_scr, d.s_scr
            s = s_scr[...]
            outs.append(_dot(qb_scr[sl, :], s.astype(BF16)))
            s_scr[...] = s * jnp.concatenate([dm_scr[j]] * (dv // dk), axis=1) + inc
        for (_, _, sl), o, loc in zip(steps, outs, local):
            if finish:
                y = _rmsnorm(o + o_scr[sl, :], gn_ref[...])
                out_ref[sl, :] = (y * sr_ref[sl, :].astype(F32)).astype(BF16)
            else:
                o_scr[sl, :] = o + loc
        return carry

    half = n_pairs // (2 * GLA_SCAN_UNROLL)
    lax.fori_loop(0, half, functools.partial(scan_body, finish=False), 0)
    lax.fori_loop(half, 2 * half, functools.partial(scan_body, finish=True), 0)


def _gla_call(q, k, v, sr, gf, gb, kc, vc, gfc, gbc, gn):
    bsz, t, qk = q.shape
    d = v.shape[2]
    tc = kc.shape[1]
    dk = qk // GLA_HEADS
    dv = d // GLA_HEADS
    pair = GLA_KERNEL_CHUNK * GLA_PAIR
    n_pairs = t // pair
    tile = GLA_KERNEL_CHUNK * GLA_GROUP
    seq = lambda rows, width: pl.BlockSpec((None, rows, width), lambda b, h: (b, 0, h))
    scratch_shapes = [
        pltpu.VMEM((t, pair), BF16),
        pltpu.VMEM((t, dv), F32),
        pltpu.VMEM((n_pairs, dk, pair), BF16), pltpu.VMEM((n_pairs, dk, pair), BF16),
        pltpu.VMEM((t, dk), BF16), pltpu.VMEM((t, dk), BF16),
        pltpu.VMEM((n_pairs, dk, dk), F32), pltpu.VMEM((n_pairs, dk, dk), F32),
        pltpu.VMEM((dk, dv), F32), pltpu.VMEM((dk, dv), F32),
    ]
    pipelined = (2 * _nbytes((t, dk), BF16) + 3 * _nbytes((t, dv), BF16) + 2 * _nbytes((t, dk), F32)
                 + _nbytes((tc, dk), BF16) + _nbytes((tc, dv), BF16) + 2 * _nbytes((tc, dk), F32)
                 + _nbytes((1, dv), F32))
    scratch = (_nbytes((t, pair), BF16) + _nbytes((t, dv), F32) + 2 * _nbytes((n_pairs, dk, pair), BF16)
               + 2 * _nbytes((t, dk), BF16) + 2 * _nbytes((n_pairs, dk, dk), F32) + 2 * _nbytes((dk, dv), F32))
    temps = 2 * GLA_LOCAL_UNROLL * (8 * _nbytes((tile, dk), F32) + 2 * _nbytes((tile, tile), F32))
    return pl.pallas_call(
        _gla_kernel,
        out_shape=jax.ShapeDtypeStruct((bsz, t, d), BF16),
        grid=(bsz, GLA_HEADS),
        in_specs=[seq(t, dk), seq(t, dk), seq(t, dv), seq(t, dv), seq(t, dk), seq(t, dk),
                  seq(tc, dk), seq(tc, dv), seq(tc, dk), seq(tc, dk),
                  pl.BlockSpec((1, dv), lambda b, h: (0, h))],
        out_specs=seq(t, dv),
        scratch_shapes=scratch_shapes,
        compiler_params=pltpu.CompilerParams(
            dimension_semantics=("parallel", "parallel"),
            vmem_limit_bytes=_vmem_limit(pipelined, scratch, temps),
        ),
        name="gla",
    )(q, k, v, sr, gf, gb, kc, vc, gfc, gbc, gn)


def _out_kernel(x_ref, h1_ref, ya_ref, yb_ref, ga1_ref, sh2_ref, sc2_ref, ga2_ref, g2_ref, gfin_ref,
                wg_ref, bg_ref, wco_ref, wgo_ref, wo_ref, wup_ref, wdn_ref, o_ref, y_scr, h2_scr):
    tm, d = x_ref.shape
    d_ff = wup_ref.shape[1]

    def merge_stage(j):
        cs = slice(j, j + COL_BLOCK)
        h1 = h1_ref[...]
        za = _dot(h1, wg_ref[:, j:j + COL_BLOCK]) + bg_ref[:, j:j + COL_BLOCK]
        zb = _dot(h1, wg_ref[:, d + j:d + j + COL_BLOCK]) + bg_ref[:, d + j:d + j + COL_BLOCK]
        pa = _dot(ya_ref[...], wco_ref[:, cs])
        pb = _dot(yb_ref[...], wgo_ref[:, cs])

        def epilogue():
            y_scr[:, cs] = (_sigmoid(za) * pa + _sigmoid(zb) * pb).astype(BF16)
        return epilogue

    _run_skewed([functools.partial(merge_stage, j) for j in range(0, d, COL_BLOCK)])
    o_ref[...] = x_ref[...] + ga1_ref[...] * _dot(y_scr[...], wo_ref[...])
    h2_scr[...] = _modulate(o_ref[...], g2_ref[...], sh2_ref[...], sc2_ref[...]).astype(BF16)

    parts = []

    def mlp_stage(j):
        u = _dot(h2_scr[...], wup_ref[:, j:j + FF_BLOCK])

        def epilogue():
            a = jnp.maximum(u, 0.0)
            parts.append(_dot((a * a).astype(BF16), wdn_ref[j:j + FF_BLOCK, :]))
        return epilogue

    _run_skewed([functools.partial(mlp_stage, j) for j in range(0, d_ff, FF_BLOCK)])
    x2 = o_ref[...] + ga2_ref[...] * functools.reduce(lambda a, b: a + b, parts)
    o_ref[...] = _rmsnorm(x2, gfin_ref[...])


def _out_call(x, h1, ya, yb, mod3, g2, gfin, wg, bg, wco, wgo, wo, wup, wdn):
    bsz, t, d = x.shape
    tm = OUT_TM
    tok = lambda: pl.BlockSpec((None, tm, d), lambda b, i: (b, i, 0))
    modrow = lambda col: pl.BlockSpec((None, 1, d), lambda b, i: (b, 0, col))
    pipelined = 2 * _nbytes((tm, d), F32) + 3 * _nbytes((tm, d), BF16) + 4 * _nbytes((1, d), F32)
    resident = sum(_nbytes(a.shape, a.dtype) for a in (g2, gfin, wg, bg, wco, wgo, wo, wup, wdn))
    scratch = 2 * _nbytes((tm, d), BF16)
    temps = 10 * _nbytes((tm, d), F32)
    return pl.pallas_call(
        _out_kernel,
        out_shape=jax.ShapeDtypeStruct((bsz, t, d), F32),
        grid=(bsz, t // tm),
        in_specs=[tok(), tok(), tok(), tok()] + [modrow(cidx) for cidx in (2, 3, 4, 5)] + [_resident()] * 9,
        out_specs=tok(),
        scratch_shapes=[pltpu.VMEM((tm, d), BF16), pltpu.VMEM((tm, d), BF16)],
        compiler_params=pltpu.CompilerParams(
            dimension_semantics=("parallel", "parallel"),
            vmem_limit_bytes=_vmem_limit(pipelined, resident + scratch, temps),
        ),
        name="out",
    )(x, h1, ya, yb, mod3, mod3, mod3, mod3, g2, gfin, wg, bg, wco, wgo, wo, wup, wdn)


def kernel(x, c, ctx, c_ctx, w_ada, b_ada, g_norm1, w_in, b_in, conv_w, w_conv_out, w_a2_f, b_a_f,
           w_a2_b, b_a_b, g_gla_norm, w_gla_out, w_o, g_norm2, w_up, w_down, g_final):
    depth = w_ada.shape[0]
    assert depth == 1, "only the single-layer block is implemented"
    bsz, t, d = x.shape
    qk = w_a2_f.shape[2]
    dk = qk // GLA_HEADS
    rank = w_a2_f.shape[1]
    gla_tile = GLA_GROUP * GLA_KERNEL_CHUNK
    assert t % INPROJ_TM == 0 and t % OUT_TM == 0 and INPROJ_TM % GRID_W == 0
    assert GLA_PAIR == 2 and GLA_GROUP % GLA_PAIR == 0 and GLA_PAIR * GLA_KERNEL_CHUNK == dk
    gla_pair = GLA_PAIR * GLA_KERNEL_CHUNK
    assert t % (GLA_LOCAL_UNROLL * gla_tile) == 0 and ctx.shape[1] % gla_tile == 0
    assert t % (2 * GLA_SCAN_UNROLL * gla_pair) == 0
    assert bsz + 1 <= MOD_ROWS and 2 * rank <= LR_PAD and 5 * d + 2 * qk + LR_PAD <= w_in.shape[2]

    cc = jnp.zeros((MOD_ROWS, d), F32).at[:bsz].set(c).at[bsz].set(c_ctx)
    mod = _ada_call(cc, w_ada[0], b_ada[0][None, :])
    mod3 = mod.reshape(MOD_ROWS, 1, N_MOD * d)

    o_lr = 5 * d + 2 * qk
    o_g = o_lr + 2 * rank
    w_main = w_in[0][:, :o_lr + LR_PAD].astype(BF16)
    b_main = b_in[0][None, :o_lr + LR_PAD]
    wg = w_in[0][:, o_g:].astype(BF16)
    bg = b_in[0][None, o_g:]
    wa2 = jnp.zeros((LR_PAD, 2 * qk), F32)
    wa2 = wa2.at[:rank, :qk].set(w_a2_f[0]).at[rank:2 * rank, qk:].set(w_a2_b[0]).astype(BF16)
    ba2 = jnp.concatenate([b_a_f[0], b_a_b[0]])[None, :]
    g1 = g_norm1[0][None, :]

    h1, ya, q, k, v, sr, gf, gb = _inproj_call(x, mod3, g1, w_main, b_main, conv_w[0], wa2, ba2, qk, dk)
    kc, vc, gfc, gbc = _inproj_ctx_call(ctx, mod3, bsz, g1, w_main, b_main, wa2, ba2, qk)
    yb = _gla_call(q, k, v, sr, gf, gb, kc, vc, gfc, gbc, g_gla_norm[0][None, :])
    return _out_call(x, h1, ya, yb, mod3, g_norm2[0][None, :], g_final[None, :], wg, bg,
                     w_conv_out[0].astype(BF16), w_gla_out[0].astype(BF16), w_o[0].astype(BF16),
                     w_up[0].astype(BF16), w_down[0].astype(BF16))
```

```python
import functools
from typing import NamedTuple

import jax
import jax.numpy as jnp
from jax import lax
from jax.experimental import pallas as pl
from jax.experimental.pallas import tpu as pltpu

F32 = jnp.float32
BF16 = jnp.bfloat16
HIGHEST = lax.Precision.HIGHEST

GLA_HEADS = 4
GLA_TAU = 16.0
GRID_W = 64
N_MOD = 6
RMS_EPS = 1e-6
LOG2_E = 1.4426950408889634

V7X_LANES = 128
V7X_VMEM_LIMIT_BYTES = 60000 * 1024

LR_PAD = V7X_LANES
GLA_KERNEL_CHUNK = 64
GLA_DIAG_BLOCK = 8
GLA_MAX_EXPONENT = 96.0
GLA_DIAG_DECAY_FLOOR = -GLA_MAX_EXPONENT / (GLA_DIAG_BLOCK // 2)
GLA_MILD_DECAY_FLOOR = -GLA_MAX_EXPONENT / (GLA_KERNEL_CHUNK // 2)
GLA_PAIR = 2
GLA_GROUP = 4
GLA_LOCAL_UNROLL = 4
GLA_SCAN_UNROLL = 8
INPROJ_TM = 1024
OUT_TM = 512
COL_BLOCK = 256
FF_BLOCK = 1024
MOD_ROWS = 16


def _vmem_limit(pipelined_bytes, resident_bytes, temp_bytes):
    need = 2 * pipelined_bytes + resident_bytes + temp_bytes
    return int(min(V7X_VMEM_LIMIT_BYTES, need))


def _nbytes(shape, dtype):
    n = 1
    for s in shape:
        n *= s
    return n * jnp.dtype(dtype).itemsize


def _dot(a, b):
    return jnp.dot(a, b, preferred_element_type=F32)


def _nt_dot(a, b):
    return lax.dot_general(a, b, (((1,), (1,)), ((), ())), preferred_element_type=F32)


def _run_skewed(stages):
    pending = None
    for stage in stages:
        epilogue = stage()
        if pending is not None:
            pending()
        pending = epilogue
    if pending is not None:
        pending()


def _sigmoid(x):
    return 1.0 / (1.0 + jnp.exp(-x))


def _log_sigmoid(x):
    return jnp.minimum(x, 0.0) - jnp.log(1.0 + jnp.exp(-jnp.abs(x)))


def _rmsnorm(x, g):
    return x * lax.rsqrt(jnp.mean(x * x, axis=-1, keepdims=True) + RMS_EPS) * g


def _modulate(x, g, shift, scale):
    return _rmsnorm(x, g) * (1.0 + scale) + shift


def _ada_kernel(c_ref, w_ref, b_ref, o_ref):
    c = c_ref[...]
    s = c * _sigmoid(c)
    o_ref[...] = jnp.dot(s, w_ref[...], precision=HIGHEST, preferred_element_type=F32) + b_ref[...]


def _ada_call(cc, w_ada, b_ada):
    d = cc.shape[1]
    n_out = w_ada.shape[1]
    return pl.pallas_call(
        _ada_kernel,
        out_shape=jax.ShapeDtypeStruct((MOD_ROWS, n_out), F32),
        grid=(n_out // d,),
        in_specs=[
            pl.BlockSpec((MOD_ROWS, d), lambda j: (0, 0)),
            pl.BlockSpec((d, d), lambda j: (0, j)),
            pl.BlockSpec((1, d), lambda j: (0, j)),
        ],
        out_specs=pl.BlockSpec((MOD_ROWS, d), lambda j: (0, j)),
        compiler_params=pltpu.CompilerParams(
            dimension_semantics=("arbitrary",),
            vmem_limit_bytes=_vmem_limit(_nbytes((d, d), F32) + _nbytes((MOD_ROWS, 2 * d), F32),
                                         0, 4 * _nbytes((d, d), F32)),
        ),
        name="ada",
    )(cc, w_ada, b_ada)


def _log_decays(lr, wa2_ref, ba2_ref):
    xg = _dot(lr.astype(BF16), wa2_ref[...]) + ba2_ref[...]
    return _log_sigmoid(xg) * (LOG2_E / GLA_TAU)


def _inproj_kernel(x_ref, sh_ref, sc_ref, g1_ref, w_ref, b_ref, cw_ref, wa2_ref, ba2_ref,
                   h_ref, ya_ref, q_ref, k_ref, v_ref, sr_ref, gf_ref, gb_ref, *, q_scale):
    tm, d = x_ref.shape
    qk = q_ref.shape[1]
    h_ref[...] = _modulate(x_ref[...], g1_ref[...], sh_ref[...], sc_ref[...]).astype(BF16)

    col_in_row = lax.broadcasted_iota(jnp.int32, (tm, 1), 0) % GRID_W
    has_left = col_in_row != 0
    has_right = col_in_row != GRID_W - 1

    def proj(c0, width):
        return _dot(h_ref[...], w_ref[:, c0:c0 + width]) + b_ref[:, c0:c0 + width]

    o_q = 3 * d
    o_v = o_q + 2 * qk
    o_r = o_v + d
    o_lr = o_r + d

    def decay_stage():
        lr = proj(o_lr, LR_PAD)

        def epilogue():
            g = _log_decays(lr, wa2_ref, ba2_ref)
            gf_ref[...] = g[:, 0:qk]
            gb_ref[...] = g[:, qk:2 * qk]
        return epilogue

    def conv_stage(j):
        cs = slice(j, j + COL_BLOCK)
        xa = proj(j, COL_BLOCK)
        ba = proj(d + j, COL_BLOCK)
        ca = proj(2 * d + j, COL_BLOCK)

        def epilogue():
            u = ca * xa
            left = jnp.where(has_left, pltpu.roll(u, 1, 0), 0.0)
            right = jnp.where(has_right, pltpu.roll(u, tm - 1, 0), 0.0)
            y = left * cw_ref[0:1, cs] + u * cw_ref[1:2, cs] + right * cw_ref[2:3, cs]
            ya_ref[:, cs] = (ba * y).astype(BF16)
        return epilogue

    def swish_stage(j):
        r = proj(o_r + j, COL_BLOCK)

        def epilogue():
            sr_ref[:, j:j + COL_BLOCK] = (r * _sigmoid(r)).astype(BF16)
        return epilogue

    def qk_stage():
        q = proj(o_q, qk)
        k = proj(o_q + qk, qk)

        def epilogue():
            q_ref[...] = (q * q_scale).astype(BF16)
            k_ref[...] = k.astype(BF16)
        return epilogue

    def value_stage(j):
        v = proj(o_v + j, COL_BLOCK)

        def epilogue():
            v_ref[:, j:j + COL_BLOCK] = v.astype(BF16)
        return epilogue

    _run_skewed([decay_stage] + [functools.partial(conv_stage, j) for j in range(0, d, COL_BLOCK)]
                + [functools.partial(swish_stage, j) for j in range(0, d, COL_BLOCK)] + [qk_stage]
                + [functools.partial(value_stage, j) for j in range(0, d, COL_BLOCK)])


def _inproj_ctx_kernel(x_ref, sh_ref, sc_ref, g1_ref, w_ref, b_ref, wa2_ref, ba2_ref,
                       k_ref, v_ref, gf_ref, gb_ref):
    d = x_ref.shape[1]
    qk = k_ref.shape[1]
    h = _modulate(x_ref[...], g1_ref[...], sh_ref[...], sc_ref[...]).astype(BF16)
    o_k = 3 * d + qk
    k_ref[...] = (_dot(h, w_ref[:, o_k:o_k + qk]) + b_ref[:, o_k:o_k + qk]).astype(BF16)
    for j in range(0, d, COL_BLOCK):
        c0 = o_k + qk + j
        v_ref[:, j:j + COL_BLOCK] = (_dot(h, w_ref[:, c0:c0 + COL_BLOCK]) + b_ref[:, c0:c0 + COL_BLOCK]).astype(BF16)
    c0 = o_k + qk + 2 * d
    lr = _dot(h, w_ref[:, c0:c0 + LR_PAD]) + b_ref[:, c0:c0 + LR_PAD]
    g = _log_decays(lr, wa2_ref, ba2_ref)
    gf_ref[...] = g[:, 0:qk]
    gb_ref[...] = g[:, qk:2 * qk]


def _resident():
    return pl.BlockSpec(memory_space=pltpu.VMEM)


def _inproj_call(x, mod3, g1, w, b, cw, wa2, ba2, qk, dk):
    bsz, t, d = x.shape
    tm = INPROJ_TM
    tok = lambda n: pl.BlockSpec((None, tm, n), lambda b, i: (b, i, 0))
    modrow = lambda col: pl.BlockSpec((None, 1, d), lambda b, i: (b, 0, col))
    out_shapes = (
        jax.ShapeDtypeStruct((bsz, t, d), BF16),
        jax.ShapeDtypeStruct((bsz, t, d), BF16),
        jax.ShapeDtypeStruct((bsz, t, qk), BF16),
        jax.ShapeDtypeStruct((bsz, t, qk), BF16),
        jax.ShapeDtypeStruct((bsz, t, d), BF16),
        jax.ShapeDtypeStruct((bsz, t, d), BF16),
        jax.ShapeDtypeStruct((bsz, t, qk), F32),
        jax.ShapeDtypeStruct((bsz, t, qk), F32),
    )
    pipelined = (_nbytes((tm, d), F32) + 4 * _nbytes((tm, d), BF16) + 2 * _nbytes((tm, qk), BF16)
                 + 2 * _nbytes((tm, qk), F32) + 2 * _nbytes((1, d), F32))
    resident = sum(_nbytes(a.shape, a.dtype) for a in (g1, w, b, cw, wa2, ba2))
    temps = 24 * _nbytes((tm, COL_BLOCK), F32) + 3 * _nbytes((tm, 2 * qk), F32)
    return pl.pallas_call(
        functools.partial(_inproj_kernel, q_scale=dk ** -0.5),
        out_shape=out_shapes,
        grid=(bsz, t // tm),
        in_specs=[tok(d), modrow(0), modrow(1)] + [_resident()] * 6,
        out_specs=(tok(d), tok(d), tok(qk), tok(qk), tok(d), tok(d), tok(qk), tok(qk)),
        compiler_params=pltpu.CompilerParams(
            dimension_semantics=("parallel", "parallel"),
            vmem_limit_bytes=_vmem_limit(pipelined, resident, temps),
        ),
        name="inproj",
    )(x, mod3, mod3, g1, w, b, cw, wa2, ba2)


def _inproj_ctx_call(ctx, mod3, ctx_row, g1, w, b, wa2, ba2, qk):
    bsz, tc, d = ctx.shape
    tok = lambda n: pl.BlockSpec((None, tc, n), lambda b: (b, 0, 0))
    modrow = lambda col: pl.BlockSpec((None, 1, d), lambda b: (ctx_row, 0, col))
    out_shapes = (
        jax.ShapeDtypeStruct((bsz, tc, qk), BF16),
        jax.ShapeDtypeStruct((bsz, tc, d), BF16),
        jax.ShapeDtypeStruct((bsz, tc, qk), F32),
        jax.ShapeDtypeStruct((bsz, tc, qk), F32),
    )
    pipelined = (_nbytes((tc, d), F32) + _nbytes((tc, d), BF16) + _nbytes((tc, qk), BF16)
                 + 2 * _nbytes((tc, qk), F32) + 2 * _nbytes((1, d), F32))
    resident = sum(_nbytes(a.shape, a.dtype) for a in (g1, w, b, wa2, ba2))
    temps = _nbytes((tc, d), BF16) + 6 * _nbytes((tc, COL_BLOCK), F32) + 3 * _nbytes((tc, 2 * qk), F32)
    return pl.pallas_call(
        _inproj_ctx_kernel,
        out_shape=out_shapes,
        grid=(bsz,),
        in_specs=[tok(d), modrow(0), modrow(1)] + [_resident()] * 5,
        out_specs=(tok(qk), tok(d), tok(qk), tok(qk)),
        compiler_params=pltpu.CompilerParams(
            dimension_semantics=("parallel",),
            vmem_limit_bytes=_vmem_limit(pipelined, resident, temps),
        ),
        name="inproj_ctx",
    )(ctx, mod3, mod3, g1, w, b, wa2, ba2)


class _ScanDir(NamedTuple):
    fwd: bool
    g_ref: object
    gc_ref: object
    chunk_sum: object
    diag_sum: object
    split_masks: tuple
    chunk_mask: object
    diag_mask: object
    kt_scr: object
    qb_scr: object
    dm_scr: object
    s_scr: object


def _block_sums(sum_matrix, g):
    dk = g.shape[1]
    g_hi = g.astype(BF16)
    g_lo = (g - g_hi.astype(F32)).astype(BF16)
    bb = _dot(sum_matrix, jnp.concatenate([g_hi, g_lo], axis=1))
    return bb[:, :dk] + bb[:, dk:]


def _tile_decays(g, sum_matrix, fwd, n_chunks):
    rows, dk = g.shape
    c = rows // n_chunks
    b = _block_sums(sum_matrix, g).reshape(n_chunks, c, dk)
    return b, (b[:, c - 1:c, :] if fwd else b[:, 0:1, :])


def _block_row(x, block, row):
    rows, n = x.shape
    x3 = x.reshape(rows // block, block, n)
    return jnp.broadcast_to(x3[:, row:row + 1, :], x3.shape).reshape(rows, n)


def _pair_offsets(tot, fwd):
    n_chunks = tot.shape[0]
    zero = jnp.zeros_like(tot[0:1])
    on_odd = jnp.concatenate([t for p in range(0, n_chunks, GLA_PAIR) for t in (zero, tot[p:p + 1])], axis=0)
    on_even = jnp.concatenate([t for p in range(0, n_chunks, GLA_PAIR) for t in (tot[p + 1:p + 2], zero)], axis=0)
    pair_tot = jnp.concatenate([tot[p:p + 1] + tot[p + 1:p + 2] for p in range(0, n_chunks, GLA_PAIR)], axis=0)
    return (on_odd, on_even, pair_tot) if fwd else (on_even, on_odd, pair_tot)


def _lane_broadcast_column(row):
    n = row.shape[1]
    return jnp.broadcast_to(row, (n, n)).T


def _gla_kernel(q_ref, k_ref, v_ref, sr_ref, gf_ref, gb_ref, kc_ref, vc_ref, gfc_ref, gbc_ref, gn_ref,
                out_ref, a_scr, o_scr, ktf_scr, ktb_scr, qbf_scr, qbb_scr, dmf_scr, dmb_scr, sf_scr, sb_scr):
    c = GLA_KERNEL_CHUNK
    grp = GLA_GROUP
    tile = c * grp
    pair = c * GLA_PAIR
    pairs_per_tile = grp // GLA_PAIR
    diag = GLA_DIAG_BLOCK
    t, dk = q_ref.shape
    dv = v_ref.shape[1]
    n_pairs = t // pair
    row = lax.broadcasted_iota(jnp.int32, (tile, tile), 0)
    col = lax.broadcasted_iota(jnp.int32, (tile, tile), 1)
    halves = []
    half = pair // 2
    while half >= diag:
        halves.append(half)
        half //= 2

    def same_block(size):
        return (row // size) == (col // size)

    def scan_dir(fwd, g_ref, gc_ref, kt_scr, qb_scr, dm_scr, s_scr):
        seen = (row >= col) if fwd else (row <= col)
        split_masks = []
        for h in halves:
            later, earlier = (row % (2 * h) >= h), (col % (2 * h) < h)
            if not fwd:
                later, earlier = (row % (2 * h) < h), (col % (2 * h) >= h)
            split_masks.append((h, (same_block(2 * h) & later & earlier).astype(BF16)))
        return _ScanDir(fwd, g_ref, gc_ref, (same_block(c) & seen).astype(BF16),
                        (same_block(diag) & seen).astype(BF16), tuple(split_masks),
                        same_block(c) & seen, same_block(diag) & seen, kt_scr, qb_scr, dm_scr, s_scr)

    dirs = (scan_dir(True, gf_ref, gfc_ref, ktf_scr, qbf_scr, dmf_scr, sf_scr),
            scan_dir(False, gb_ref, gbc_ref, ktb_scr, qbb_scr, dmb_scr, sb_scr))

    def key_side(k3, b, tot, k_off):
        kt = k3 * jnp.exp2(tot - b)
        return kt, kt * jnp.exp2(k_off)

    ctx_tiles = kc_ref.shape[0] // tile
    ctx_items = [(d, ti) for d in dirs for ti in range(ctx_tiles)]
    ctx_cums = [_tile_decays(d.gc_ref[ti * tile:(ti + 1) * tile, :], d.chunk_sum, d.fwd, grp)
                for d, ti in ctx_items]
    ctx_terms = {}
    for (d, ti), (b, tot) in zip(ctx_items, ctx_cums):
        _, k_off, pair_tot = _pair_offsets(tot, d.fwd)
        kc3 = kc_ref[ti * tile:(ti + 1) * tile, :].astype(F32).reshape(grp, c, dk)
        _, kt_pair = key_side(kc3, b, tot, k_off)
        kt_t = kt_pair.reshape(tile, dk).T.astype(BF16)
        decay = jnp.exp2(pair_tot)
        for p in range(pairs_per_tile):
            r0 = ti * tile + p * pair
            ctx_terms[(d.fwd, ti * pairs_per_tile + p)] = (
                _dot(kt_t[:, p * pair:(p + 1) * pair], vc_ref[r0:r0 + pair, :]),
                _lane_broadcast_column(decay[p]))
    for d in dirs:
        order = range(ctx_tiles * pairs_per_tile)
        s = jnp.zeros((dk, dv), F32)
        for p in (order if d.fwd else reversed(order)):
            inc, dm = ctx_terms[(d.fwd, p)]
            s = s * jnp.concatenate([dm] * (dv // dk), axis=1) + inc
        d.s_scr[...] = s

    def local_body(i, carry):
        tiles = [i * GLA_LOCAL_UNROLL + u for u in range(GLA_LOCAL_UNROLL)]
        rows = [pl.ds(pl.multiple_of(ti * tile, tile), tile) for ti in tiles]
        items = [(u, d) for u in range(GLA_LOCAL_UNROLL) for d in dirs]
        g_tiles = [d.g_ref[rows[u], :] for u, d in items]
        cums = [_tile_decays(g, d.chunk_sum, d.fwd, grp) for (u, d), g in zip(items, g_tiles)]
        q2 = [q_ref[sl, :].astype(F32) for sl in rows]
        k2 = [k_ref[sl, :].astype(F32) for sl in rows]
        pair_terms = []
        for (u, d), (b, tot) in zip(items, cums):
            q_off, k_off, pair_tot = _pair_offsets(tot, d.fwd)
            q3, k3 = q2[u].reshape(grp, c, dk), k2[u].reshape(grp, c, dk)
            qb = q3 * jnp.exp2(b)
            kt, kt_pair = key_side(k3, b, tot, k_off)
            pair_terms.append(_nt_dot(qb.reshape(tile, dk).astype(BF16), kt.reshape(tile, dk).astype(BF16)))
            d.qb_scr[rows[u], :] = (qb * jnp.exp2(q_off)).reshape(tile, dk).astype(BF16)
            kt_t = kt_pair.reshape(tile, dk).T.astype(BF16)
            decay = jnp.exp2(pair_tot)
            for p in range(pairs_per_tile):
                d.kt_scr[tiles[u] * pairs_per_tile + p] = kt_t[:, p * pair:(p + 1) * pair]
                d.dm_scr[tiles[u] * pairs_per_tile + p] = _lane_broadcast_column(decay[p])

        def store_scores(per_item):
            for u in range(GLA_LOCAL_UNROLL):
                scores = functools.reduce(lambda x, y: x + y, [s for (iu, _), s in zip(items, per_item) if iu == u])
                for p in range(pairs_per_tile):
                    a_scr[pl.ds(pl.multiple_of(tiles[u] * tile + p * pair, pair), pair), :] = (
                        scores[p * pair:(p + 1) * pair, p * pair:(p + 1) * pair])

        def chunk_scores_mild():
            products = []
            for (u, d), (b, _) in zip(items, cums):
                flat = b.reshape(tile, dk)
                mid = _block_row(flat, c, c // 2 - 1 if d.fwd else c // 2)
                products.append(_nt_dot((q2[u] * jnp.exp2(flat - mid)).astype(BF16),
                                        (k2[u] * jnp.exp2(mid - flat)).astype(BF16)))
            store_scores([jnp.where(d.chunk_mask, inner, 0.0).astype(BF16)
                          + outer.astype(BF16) * d.split_masks[0][1]
                          for (_, d), inner, outer in zip(items, products, pair_terms)])

        def chunk_scores_any():
            diag_cums = [_block_sums(d.diag_sum, jnp.maximum(g, GLA_DIAG_DECAY_FLOOR))
                         for (u, d), g in zip(items, g_tiles)]
            products = []
            for (u, d), (b, _), bd in zip(items, cums, diag_cums):
                flat = b.reshape(tile, dk)
                split = []
                for h, mask in d.split_masks[1:]:
                    e = jnp.exp2(-jnp.abs(flat - _block_row(flat, 2 * h, h - 1 if d.fwd else h)))
                    split.append((_nt_dot((q2[u] * e).astype(BF16), (k2[u] * e).astype(BF16)), mask))
                mid = _block_row(bd, diag, diag // 2 - 1 if d.fwd else diag // 2)
                inner = _nt_dot((q2[u] * jnp.exp2(bd - mid)).astype(BF16),
                                (k2[u] * jnp.exp2(mid - bd)).astype(BF16))
                products.append((split, inner))
            per_item = []
            for (_, d), (split, inner), outer in zip(items, products, pair_terms):
                part = jnp.where(d.diag_mask, inner, 0.0).astype(BF16) + outer.astype(BF16) * d.split_masks[0][1]
                for product, mask in split:
                    part = part + product.astype(BF16) * mask
                per_item.append(part)
            store_scores(per_item)

        lowest = jnp.min(functools.reduce(jnp.minimum, g_tiles))
        pl.when(lowest >= GLA_MILD_DECAY_FLOOR)(chunk_scores_mild)
        pl.when(lowest < GLA_MILD_DECAY_FLOOR)(chunk_scores_any)
        return carry

    lax.fori_loop(0, t // (tile * GLA_LOCAL_UNROLL), local_body, 0)

    def scan_body(i, carry, finish):
        steps = []
        for u in range(GLA_SCAN_UNROLL):
            for d in dirs:
                step = i * GLA_SCAN_UNROLL + u
                j = step if d.fwd else n_pairs - 1 - step
                steps.append((d, j, pl.ds(pl.multiple_of(j * pair, pair), pair)))
        increments = [_dot(d.kt_scr[j], v_ref[sl, :]) for d, j, sl in steps]
        local = [None if finish else _dot(a_scr[sl, :], v_ref[sl, :]) for _, _, sl in steps]
        outs = []
        for (d, j, sl), inc in zip(steps, increments):
            qb_scr, dm_scr, s_scr = d.qb_scr, d.dm_scr, d.s_scr
            s = s_scr[...]
            outs.append(_dot(qb_scr[sl, :], s.astype(BF16)))
            s_scr[...] = s * jnp.concatenate([dm_scr[j]] * (dv // dk), axis=1) + inc
        for (_, _, sl), o, loc in zip(steps, outs, local):
            if finish:
                y = _rmsnorm(o + o_scr[sl, :], gn_ref[...])
                out_ref[sl, :] = (y * sr_ref[sl, :].astype(F32)).astype(BF16)
            else:
                o_scr[sl, :] = o + loc
        return carry

    half = n_pairs // (2 * GLA_SCAN_UNROLL)
    lax.fori_loop(0, half, functools.partial(scan_body, finish=False), 0)
    lax.fori_loop(half, 2 * half, functools.partial(scan_body, finish=True), 0)


def _gla_call(q, k, v, sr, gf, gb, kc, vc, gfc, gbc, gn):
    bsz, t, qk = q.shape
    d = v.shape[2]
    tc = kc.shape[1]
    dk = qk // GLA_HEADS
    dv = d // GLA_HEADS
    pair = GLA_KERNEL_CHUNK * GLA_PAIR
    n_pairs = t // pair
    tile = GLA_KERNEL_CHUNK * GLA_GROUP
    seq = lambda rows, width: pl.BlockSpec((None, rows, width), lambda b, h: (b, 0, h))
    scratch_shapes = [
        pltpu.VMEM((t, pair), BF16),
        pltpu.VMEM((t, dv), F32),
        pltpu.VMEM((n_pairs, dk, pair), BF16), pltpu.VMEM((n_pairs, dk, pair), BF16),
        pltpu.VMEM((t, dk), BF16), pltpu.VMEM((t, dk), BF16),
        pltpu.VMEM((n_pairs, dk, dk), F32), pltpu.VMEM((n_pairs, dk, dk), F32),
        pltpu.VMEM((dk, dv), F32), pltpu.VMEM((dk, dv), F32),
    ]
    pipelined = (2 * _nbytes((t, dk), BF16) + 3 * _nbytes((t, dv), BF16) + 2 * _nbytes((t, dk), F32)
                 + _nbytes((tc, dk), BF16) + _nbytes((tc, dv), BF16) + 2 * _nbytes((tc, dk), F32)
                 + _nbytes((1, dv), F32))
    scratch = (_nbytes((t, pair), BF16) + _nbytes((t, dv), F32) + 2 * _nbytes((n_pairs, dk, pair), BF16)
               + 2 * _nbytes((t, dk), BF16) + 2 * _nbytes((n_pairs, dk, dk), F32) + 2 * _nbytes((dk, dv), F32))
    temps = 2 * GLA_LOCAL_UNROLL * (8 * _nbytes((tile, dk), F32) + 2 * _nbytes((tile, tile), F32))
    return pl.pallas_call(
        _gla_kernel,
        out_shape=jax.ShapeDtypeStruct((bsz, t, d), BF16),
        grid=(bsz, GLA_HEADS),
        in_specs=[seq(t, dk), seq(t, dk), seq(t, dv), seq(t, dv), seq(t, dk), seq(t, dk),
                  seq(tc, dk), seq(tc, dv), seq(tc, dk), seq(tc, dk),
                  pl.BlockSpec((1, dv), lambda b, h: (0, h))],
        out_specs=seq(t, dv),
        scratch_shapes=scratch_shapes,
        compiler_params=pltpu.CompilerParams(
            dimension_semantics=("parallel", "parallel"),
            vmem_limit_bytes=_vmem_limit(pipelined, scratch, temps),
        ),
        name="gla",
    )(q, k, v, sr, gf, gb, kc, vc, gfc, gbc, gn)


def _out_kernel(x_ref, h1_ref, ya_ref, yb_ref, ga1_ref, sh2_ref, sc2_ref, ga2_ref, g2_ref, gfin_ref,
                wg_ref, bg_ref, wco_ref, wgo_ref, wo_ref, wup_ref, wdn_ref, o_ref, y_scr, h2_scr):
    tm, d = x_ref.shape
    d_ff = wup_ref.shape[1]

    def merge_stage(j):
        cs = slice(j, j + COL_BLOCK)
        h1 = h1_ref[...]
        za = _dot(h1, wg_ref[:, j:j + COL_BLOCK]) + bg_ref[:, j:j + COL_BLOCK]
        zb = _dot(h1, wg_ref[:, d + j:d + j + COL_BLOCK]) + bg_ref[:, d + j:d + j + COL_BLOCK]
        pa = _dot(ya_ref[...], wco_ref[:, cs])
        pb = _dot(yb_ref[...], wgo_ref[:, cs])

        def epilogue():
            y_scr[:, cs] = (_sigmoid(za) * pa + _sigmoid(zb) * pb).astype(BF16)
        return epilogue

    _run_skewed([functools.partial(merge_stage, j) for j in range(0, d, COL_BLOCK)])
    o_ref[...] = x_ref[...] + ga1_ref[...] * _dot(y_scr[...], wo_ref[...])
    h2_scr[...] = _modulate(o_ref[...], g2_ref[...], sh2_ref[...], sc2_ref[...]).astype(BF16)

    parts = []

    def mlp_stage(j):
        u = _dot(h2_scr[...], wup_ref[:, j:j + FF_BLOCK])

        def epilogue():
            a = jnp.maximum(u, 0.0)
            parts.append(_dot((a * a).astype(BF16), wdn_ref[j:j + FF_BLOCK, :]))
        return epilogue

    _run_skewed([functools.partial(mlp_stage, j) for j in range(0, d_ff, FF_BLOCK)])
    x2 = o_ref[...] + ga2_ref[...] * functools.reduce(lambda a, b: a + b, parts)
    o_ref[...] = _rmsnorm(x2, gfin_ref[...])


def _out_call(x, h1, ya, yb, mod3, g2, gfin, wg, bg, wco, wgo, wo, wup, wdn):
    bsz, t, d = x.shape
    tm = OUT_TM
    tok = lambda: pl.BlockSpec((None, tm, d), lambda b, i: (b, i, 0))
    modrow = lambda col: pl.BlockSpec((None, 1, d), lambda b, i: (b, 0, col))
    pipelined = 2 * _nbytes((tm, d), F32) + 3 * _nbytes((tm, d), BF16) + 4 * _nbytes((1, d), F32)
    resident = sum(_nbytes(a.shape, a.dtype) for a in (g2, gfin, wg, bg, wco, wgo, wo, wup, wdn))
    scratch = 2 * _nbytes((tm, d), BF16)
    temps = 10 * _nbytes((tm, d), F32)
    return pl.pallas_call(
        _out_kernel,
        out_shape=jax.ShapeDtypeStruct((bsz, t, d), F32),
        grid=(bsz, t // tm),
        in_specs=[tok(), tok(), tok(), tok()] + [modrow(cidx) for cidx in (2, 3, 4, 5)] + [_resident()] * 9,
        out_specs=tok(),
        scratch_shapes=[pltpu.VMEM((tm, d), BF16), pltpu.VMEM((tm, d), BF16)],
        compiler_params=pltpu.CompilerParams(
            dimension_semantics=("parallel", "parallel"),
            vmem_limit_bytes=_vmem_limit(pipelined, resident + scratch, temps),
        ),
        name="out",
    )(x, h1, ya, yb, mod3, mod3, mod3, mod3, g2, gfin, wg, bg, wco, wgo, wo, wup, wdn)


def kernel(x, c, ctx, c_ctx, w_ada, b_ada, g_norm1, w_in, b_in, conv_w, w_conv_out, w_a2_f, b_a_f,
           w_a2_b, b_a_b, g_gla_norm, w_gla_out, w_o, g_norm2, w_up, w_down, g_final):
    depth = w_ada.shape[0]
    assert depth == 1, "only the single-layer block is implemented"
    bsz, t, d = x.shape
    qk = w_a2_f.shape[2]
    dk = qk // GLA_HEADS
    rank = w_a2_f.shape[1]
    gla_tile = GLA_GROUP * GLA_KERNEL_CHUNK
    assert t % INPROJ_TM == 0 and t % OUT_TM == 0 and INPROJ_TM % GRID_W == 0
    assert GLA_PAIR == 2 and GLA_GROUP % GLA_PAIR == 0 and GLA_PAIR * GLA_KERNEL_CHUNK == dk
    gla_pair = GLA_PAIR * GLA_KERNEL_CHUNK
    assert t % (GLA_LOCAL_UNROLL * gla_tile) == 0 and ctx.shape[1] % gla_tile == 0
    assert t % (2 * GLA_SCAN_UNROLL * gla_pair) == 0
    assert bsz + 1 <= MOD_ROWS and 2 * rank <= LR_PAD and 5 * d + 2 * qk + LR_PAD <= w_in.shape[2]

    cc = jnp.zeros((MOD_ROWS, d), F32).at[:bsz].set(c).at[bsz].set(c_ctx)
    mod = _ada_call(cc, w_ada[0], b_ada[0][None, :])
    mod3 = mod.reshape(MOD_ROWS, 1, N_MOD * d)

    o_lr = 5 * d + 2 * qk
    o_g = o_lr + 2 * rank
    w_main = w_in[0][:, :o_lr + LR_PAD].astype(BF16)
    b_main = b_in[0][None, :o_lr + LR_PAD]
    wg = w_in[0][:, o_g:].astype(BF16)
    bg = b_in[0][None, o_g:]
    wa2 = jnp.zeros((LR_PAD, 2 * qk), F32)
    wa2 = wa2.at[:rank, :qk].set(w_a2_f[0]).at[rank:2 * rank, qk:].set(w_a2_b[0]).astype(BF16)
    ba2 = jnp.concatenate([b_a_f[0], b_a_b[0]])[None, :]
    g1 = g_norm1[0][None, :]

    h1, ya, q, k, v, sr, gf, gb = _inproj_call(x, mod3, g1, w_main, b_main, conv_w[0], wa2, ba2, qk, dk)
    kc, vc, gfc, gbc = _inproj_ctx_call(ctx, mod3, bsz, g1, w_main, b_main, wa2, ba2, qk)
    yb = _gla_call(q, k, v, sr, gf, gb, kc, vc, gfc, gbc, g_gla_norm[0][None, :])
    return _out_call(x, h1, ya, yb, mod3, g_norm2[0][None, :], g_final[None, :], wg, bg,
                     w_conv_out[0].astype(BF16), w_gla_out[0].astype(BF16), w_o[0].astype(BF16),
                     w_up[0].astype(BF16), w_down[0].astype(BF16))
```

```python
import functools
from typing import NamedTuple

import jax
import jax.numpy as jnp
from jax import lax
from jax.experimental import pallas as pl
from jax.experimental.pallas import tpu as pltpu

F32 = jnp.float32
BF16 = jnp.bfloat16

GLA_HEADS = 4
GLA_TAU = 16.0
GRID_W = 64
N_MOD = 6
RMS_EPS = 1e-6
LOG2_E = 1.4426950408889634

V7X_LANES = 128
V7X_VMEM_LIMIT_BYTES = 60000 * 1024

LR_PAD = V7X_LANES
GLA_KERNEL_CHUNK = 64
GLA_DIAG_BLOCK = 8
GLA_MAX_EXPONENT = 96.0
GLA_DIAG_DECAY_FLOOR = -GLA_MAX_EXPONENT / (GLA_DIAG_BLOCK // 2)
GLA_MILD_DECAY_FLOOR = -GLA_MAX_EXPONENT / (GLA_KERNEL_CHUNK // 2)
GLA_PAIR = 2
GLA_GROUP = 4
GLA_LOCAL_UNROLL = 4
GLA_SCAN_UNROLL = 8
INPROJ_TM = 1024
OUT_TM = 512
COL_BLOCK = 256
FF_BLOCK = 1024
MOD_ROWS = 16
WEIGHT_CAST_ROWS = 128


def _vmem_limit(pipelined_bytes, resident_bytes, temp_bytes):
    need = 2 * pipelined_bytes + resident_bytes + temp_bytes
    return int(min(V7X_VMEM_LIMIT_BYTES, need))


def _nbytes(shape, dtype):
    n = 1
    for s in shape:
        n *= s
    return n * jnp.dtype(dtype).itemsize


def _dot(a, b):
    return jnp.dot(a, b, preferred_element_type=F32)


def _nt_dot(a, b):
    return lax.dot_general(a, b, (((1,), (1,)), ((), ())), preferred_element_type=F32)


def _run_skewed(stages):
    pending = None
    for stage in stages:
        epilogue = stage()
        if pending is not None:
            pending()
        pending = epilogue
    if pending is not None:
        pending()


def _sigmoid(x):
    return 1.0 / (1.0 + jnp.exp(-x))


def _log_sigmoid(x):
    return jnp.minimum(x, 0.0) - jnp.log(1.0 + jnp.exp(-jnp.abs(x)))


def _rmsnorm(x, g):
    return x * lax.rsqrt(jnp.mean(x * x, axis=-1, keepdims=True) + RMS_EPS) * g


def _modulate(x, g, shift, scale):
    return _rmsnorm(x, g) * (1.0 + scale) + shift


def _ada_kernel(c_ref, w_ref, b_ref, o_ref):
    c = c_ref[...]
    s = c * _sigmoid(c)
    o_ref[...] = _dot(s.astype(BF16), w_ref[...].astype(BF16)) + b_ref[...]


def _ada_call(cc, w_ada, b_ada):
    d = cc.shape[1]
    n_out = w_ada.shape[1]
    return pl.pallas_call(
        _ada_kernel,
        out_shape=jax.ShapeDtypeStruct((MOD_ROWS, n_out), F32),
        grid=(n_out // d,),
        in_specs=[
            pl.BlockSpec((MOD_ROWS, d), lambda j: (0, 0)),
            pl.BlockSpec((d, d), lambda j: (0, j)),
            pl.BlockSpec((1, d), lambda j: (0, j)),
        ],
        out_specs=pl.BlockSpec((MOD_ROWS, d), lambda j: (0, j)),
        compiler_params=pltpu.CompilerParams(
            dimension_semantics=("arbitrary",),
            vmem_limit_bytes=_vmem_limit(_nbytes((d, d), F32) + _nbytes((MOD_ROWS, 2 * d), F32),
                                         0, 4 * _nbytes((d, d), F32)),
        ),
        name="ada",
    )(cc, w_ada, b_ada)


def _split_weight_kernel(w_ref, main_ref, gate_ref):
    main_ref[...] = w_ref[:, :main_ref.shape[1]].astype(BF16)
    gate_ref[...] = w_ref[:, w_ref.shape[1] - gate_ref.shape[1]:].astype(BF16)


def _split_weight_call(w, n_main, n_gate):
    rows, cols = w.shape
    tr = WEIGHT_CAST_ROWS
    return pl.pallas_call(
        _split_weight_kernel,
        out_shape=(jax.ShapeDtypeStruct((rows, n_main), BF16), jax.ShapeDtypeStruct((rows, n_gate), BF16)),
        grid=(rows // tr,),
        in_specs=[pl.BlockSpec((tr, cols), lambda i: (i, 0))],
        out_specs=(pl.BlockSpec((tr, n_main), lambda i: (i, 0)), pl.BlockSpec((tr, n_gate), lambda i: (i, 0))),
        compiler_params=pltpu.CompilerParams(
            dimension_semantics=("parallel",),
            vmem_limit_bytes=_vmem_limit(_nbytes((tr, cols), F32) + _nbytes((tr, n_main + n_gate), BF16), 0,
                                         2 * _nbytes((tr, cols), F32)),
        ),
        name="split_weight",
    )(w)


def _log_decays(lr, wa2_ref, ba2_ref):
    xg = _dot(lr.astype(BF16), wa2_ref[...]) + ba2_ref[...]
    return _log_sigmoid(xg) * (LOG2_E / GLA_TAU)


def _inproj_kernel(x_ref, sh_ref, sc_ref, g1_ref, w_ref, b_ref, cw_ref, wa2_ref, ba2_ref,
                   h_ref, ya_ref, q_ref, k_ref, v_ref, sr_ref, gf_ref, gb_ref, *, q_scale):
    tm, d = x_ref.shape
    qk = q_ref.shape[1]
    h_ref[...] = _modulate(x_ref[...], g1_ref[...], sh_ref[...], sc_ref[...]).astype(BF16)

    col_in_row = lax.broadcasted_iota(jnp.int32, (tm, 1), 0) % GRID_W
    has_left = col_in_row != 0
    has_right = col_in_row != GRID_W - 1

    def proj(c0, width):
        return _dot(h_ref[...], w_ref[:, c0:c0 + width]) + b_ref[:, c0:c0 + width]

    o_q = 3 * d
    o_v = o_q + 2 * qk
    o_r = o_v + d
    o_lr = o_r + d

    def decay_stage():
        lr = proj(o_lr, LR_PAD)

        def epilogue():
            g = _log_decays(lr, wa2_ref, ba2_ref)
            gf_ref[...] = g[:, 0:qk]
            gb_ref[...] = g[:, qk:2 * qk]
        return epilogue

    def conv_stage(j):
        cs = slice(j, j + COL_BLOCK)
        xa = proj(j, COL_BLOCK)
        ba = proj(d + j, COL_BLOCK)
        ca = proj(2 * d + j, COL_BLOCK)

        def epilogue():
            u = ca * xa
            left = jnp.where(has_left, pltpu.roll(u, 1, 0), 0.0)
            right = jnp.where(has_right, pltpu.roll(u, tm - 1, 0), 0.0)
            y = left * cw_ref[0:1, cs] + u * cw_ref[1:2, cs] + right * cw_ref[2:3, cs]
            ya_ref[:, cs] = (ba * y).astype(BF16)
        return epilogue

    def swish_stage(j):
        r = proj(o_r + j, COL_BLOCK)

        def epilogue():
            sr_ref[:, j:j + COL_BLOCK] = (r * _sigmoid(r)).astype(BF16)
        return epilogue

    def qk_stage():
        q = proj(o_q, qk)
        k = proj(o_q + qk, qk)

        def epilogue():
            q_ref[...] = (q * q_scale).astype(BF16)
            k_ref[...] = k.astype(BF16)
        return epilogue

    def value_stage(j):
        v = proj(o_v + j, COL_BLOCK)

        def epilogue():
            v_ref[:, j:j + COL_BLOCK] = v.astype(BF16)
        return epilogue

    _run_skewed([decay_stage] + [functools.partial(conv_stage, j) for j in range(0, d, COL_BLOCK)]
                + [functools.partial(swish_stage, j) for j in range(0, d, COL_BLOCK)] + [qk_stage]
                + [functools.partial(value_stage, j) for j in range(0, d, COL_BLOCK)])


def _inproj_ctx_kernel(x_ref, sh_ref, sc_ref, g1_ref, w_ref, b_ref, wa2_ref, ba2_ref,
                       k_ref, v_ref, gf_ref, gb_ref):
    d = x_ref.shape[1]
    qk = k_ref.shape[1]
    h = _modulate(x_ref[...], g1_ref[...], sh_ref[...], sc_ref[...]).astype(BF16)
    o_k = 3 * d + qk
    k_ref[...] = (_dot(h, w_ref[:, o_k:o_k + qk]) + b_ref[:, o_k:o_k + qk]).astype(BF16)
    for j in range(0, d, COL_BLOCK):
        c0 = o_k + qk + j
        v_ref[:, j:j + COL_BLOCK] = (_dot(h, w_ref[:, c0:c0 + COL_BLOCK]) + b_ref[:, c0:c0 + COL_BLOCK]).astype(BF16)
    c0 = o_k + qk + 2 * d
    lr = _dot(h, w_ref[:, c0:c0 + LR_PAD]) + b_ref[:, c0:c0 + LR_PAD]
    g = _log_decays(lr, wa2_ref, ba2_ref)
    gf_ref[...] = g[:, 0:qk]
    gb_ref[...] = g[:, qk:2 * qk]


def _resident():
    return pl.BlockSpec(memory_space=pltpu.VMEM)


def _inproj_call(x, mod3, g1, w, b, cw, wa2, ba2, qk, dk):
    bsz, t, d = x.shape
    tm = INPROJ_TM
    tok = lambda n: pl.BlockSpec((None, tm, n), lambda b, i: (b, i, 0))
    modrow = lambda col: pl.BlockSpec((None, 1, d), lambda b, i: (b, 0, col))
    out_shapes = (
        jax.ShapeDtypeStruct((bsz, t, d), BF16),
        jax.ShapeDtypeStruct((bsz, t, d), BF16),
        jax.ShapeDtypeStruct((bsz, t, qk), BF16),
        jax.ShapeDtypeStruct((bsz, t, qk), BF16),
        jax.ShapeDtypeStruct((bsz, t, d), BF16),
        jax.ShapeDtypeStruct((bsz, t, d), BF16),
        jax.ShapeDtypeStruct((bsz, t, qk), F32),
        jax.ShapeDtypeStruct((bsz, t, qk), F32),
    )
    pipelined = (_nbytes((tm, d), F32) + 4 * _nbytes((tm, d), BF16) + 2 * _nbytes((tm, qk), BF16)
                 + 2 * _nbytes((tm, qk), F32) + 2 * _nbytes((1, d), F32))
    resident = sum(_nbytes(a.shape, a.dtype) for a in (g1, w, b, cw, wa2, ba2))
    temps = 24 * _nbytes((tm, COL_BLOCK), F32) + 3 * _nbytes((tm, 2 * qk), F32)
    return pl.pallas_call(
        functools.partial(_inproj_kernel, q_scale=dk ** -0.5),
        out_shape=out_shapes,
        grid=(bsz, t // tm),
        in_specs=[tok(d), modrow(0), modrow(1)] + [_resident()] * 6,
        out_specs=(tok(d), tok(d), tok(qk), tok(qk), tok(d), tok(d), tok(qk), tok(qk)),
        compiler_params=pltpu.CompilerParams(
            dimension_semantics=("parallel", "parallel"),
            vmem_limit_bytes=_vmem_limit(pipelined, resident, temps),
        ),
        name="inproj",
    )(x, mod3, mod3, g1, w, b, cw, wa2, ba2)


def _inproj_ctx_call(ctx, mod3, ctx_row, g1, w, b, wa2, ba2, qk):
    bsz, tc, d = ctx.shape
    tok = lambda n: pl.BlockSpec((None, tc, n), lambda b: (b, 0, 0))
    modrow = lambda col: pl.BlockSpec((None, 1, d), lambda b: (ctx_row, 0, col))
    out_shapes = (
        jax.ShapeDtypeStruct((bsz, tc, qk), BF16),
        jax.ShapeDtypeStruct((bsz, tc, d), BF16),
        jax.ShapeDtypeStruct((bsz, tc, qk), F32),
        jax.ShapeDtypeStruct((bsz, tc, qk), F32),
    )
    pipelined = (_nbytes((tc, d), F32) + _nbytes((tc, d), BF16) + _nbytes((tc, qk), BF16)
                 + 2 * _nbytes((tc, qk), F32) + 2 * _nbytes((1, d), F32))
    resident = sum(_nbytes(a.shape, a.dtype) for a in (g1, w, b, wa2, ba2))
    temps = _nbytes((tc, d), BF16) + 6 * _nbytes((tc, COL_BLOCK), F32) + 3 * _nbytes((tc, 2 * qk), F32)
    return pl.pallas_call(
        _inproj_ctx_kernel,
        out_shape=out_shapes,
        grid=(bsz,),
        in_specs=[tok(d), modrow(0), modrow(1)] + [_resident()] * 5,
        out_specs=(tok(qk), tok(d), tok(qk), tok(qk)),
        compiler_params=pltpu.CompilerParams(
            dimension_semantics=("parallel",),
            vmem_limit_bytes=_vmem_limit(pipelined, resident, temps),
        ),
        name="inproj_ctx",
    )(ctx, mod3, mod3, g1, w, b, wa2, ba2)


class _ScanDir(NamedTuple):
    fwd: bool
    g_ref: object
    gc_ref: object
    chunk_sum: object
    diag_sum: object
    split_masks: tuple
    chunk_mask: object
    diag_mask: object
    kt_scr: object
    qb_scr: object
    dm_scr: object
    s_scr: object


def _block_sums(sum_matrix, g):
    dk = g.shape[1]
    g_hi = g.astype(BF16)
    g_lo = (g - g_hi.astype(F32)).astype(BF16)
    bb = _dot(sum_matrix, jnp.concatenate([g_hi, g_lo], axis=1))
    return bb[:, :dk] + bb[:, dk:]


def _tile_decays(g, sum_matrix, fwd, n_chunks):
    rows, dk = g.shape
    c = rows // n_chunks
    b = _block_sums(sum_matrix, g).reshape(n_chunks, c, dk)
    return b, (b[:, c - 1:c, :] if fwd else b[:, 0:1, :])


def _block_row(x, block, row):
    rows, n = x.shape
    x3 = x.reshape(rows // block, block, n)
    return jnp.broadcast_to(x3[:, row:row + 1, :], x3.shape).reshape(rows, n)


def _pair_offsets(tot, fwd):
    n_chunks = tot.shape[0]
    zero = jnp.zeros_like(tot[0:1])
    on_odd = jnp.concatenate([t for p in range(0, n_chunks, GLA_PAIR) for t in (zero, tot[p:p + 1])], axis=0)
    on_even = jnp.concatenate([t for p in range(0, n_chunks, GLA_PAIR) for t in (tot[p + 1:p + 2], zero)], axis=0)
    pair_tot = jnp.concatenate([tot[p:p + 1] + tot[p + 1:p + 2] for p in range(0, n_chunks, GLA_PAIR)], axis=0)
    return (on_odd, on_even, pair_tot) if fwd else (on_even, on_odd, pair_tot)


def _lane_broadcast_column(row):
    n = row.shape[1]
    return jnp.broadcast_to(row, (n, n)).T


def _gla_kernel(q_ref, k_ref, v_ref, sr_ref, gf_ref, gb_ref, kc_ref, vc_ref, gfc_ref, gbc_ref, gn_ref,
                out_ref, a_scr, o_scr, ktf_scr, ktb_scr, qbf_scr, qbb_scr, dmf_scr, dmb_scr, sf_scr, sb_scr):
    c = GLA_KERNEL_CHUNK
    grp = GLA_GROUP
    tile = c * grp
    pair = c * GLA_PAIR
    pairs_per_tile = grp // GLA_PAIR
    diag = GLA_DIAG_BLOCK
    t, dk = q_ref.shape
    dv = v_ref.shape[1]
    n_pairs = t // pair
    row = lax.broadcasted_iota(jnp.int32, (tile, tile), 0)
    col = lax.broadcasted_iota(jnp.int32, (tile, tile), 1)
    halves = []
    half = pair // 2
    while half >= diag:
        halves.append(half)
        half //= 2

    def same_block(size):
        return (row // size) == (col // size)

    def scan_dir(fwd, g_ref, gc_ref, kt_scr, qb_scr, dm_scr, s_scr):
        seen = (row >= col) if fwd else (row <= col)
        split_masks = []
        for h in halves:
            later, earlier = (row % (2 * h) >= h), (col % (2 * h) < h)
            if not fwd:
                later, earlier = (row % (2 * h) < h), (col % (2 * h) >= h)
            split_masks.append((h, (same_block(2 * h) & later & earlier).astype(BF16)))
        return _ScanDir(fwd, g_ref, gc_ref, (same_block(c) & seen).astype(BF16),
                        (same_block(diag) & seen).astype(BF16), tuple(split_masks),
                        same_block(c) & seen, same_block(diag) & seen, kt_scr, qb_scr, dm_scr, s_scr)

    dirs = (scan_dir(True, gf_ref, gfc_ref, ktf_scr, qbf_scr, dmf_scr, sf_scr),
            scan_dir(False, gb_ref, gbc_ref, ktb_scr, qbb_scr, dmb_scr, sb_scr))

    def key_side(k3, b, tot, k_off):
        kt = k3 * jnp.exp2(tot - b)
        return kt, kt * jnp.exp2(k_off)

    ctx_tiles = kc_ref.shape[0] // tile
    ctx_items = [(d, ti) for d in dirs for ti in range(ctx_tiles)]
    ctx_cums = [_tile_decays(d.gc_ref[ti * tile:(ti + 1) * tile, :], d.chunk_sum, d.fwd, grp)
                for d, ti in ctx_items]
    ctx_terms = {}
    for (d, ti), (b, tot) in zip(ctx_items, ctx_cums):
        _, k_off, pair_tot = _pair_offsets(tot, d.fwd)
        kc3 = kc_ref[ti * tile:(ti + 1) * tile, :].astype(F32).reshape(grp, c, dk)
        _, kt_pair = key_side(kc3, b, tot, k_off)
        kt_t = kt_pair.reshape(tile, dk).T.astype(BF16)
        decay = jnp.exp2(pair_tot)
        for p in range(pairs_per_tile):
            r0 = ti * tile + p * pair
            ctx_terms[(d.fwd, ti * pairs_per_tile + p)] = (
                _dot(kt_t[:, p * pair:(p + 1) * pair], vc_ref[r0:r0 + pair, :]),
                _lane_broadcast_column(decay[p]))
    for d in dirs:
        order = range(ctx_tiles * pairs_per_tile)
        s = jnp.zeros((dk, dv), F32)
        for p in (order if d.fwd else reversed(order)):
            inc, dm = ctx_terms[(d.fwd, p)]
            s = s * jnp.concatenate([dm] * (dv // dk), axis=1) + inc
        d.s_scr[...] = s

    def local_body(i, carry):
        tiles = [i * GLA_LOCAL_UNROLL + u for u in range(GLA_LOCAL_UNROLL)]
        rows = [pl.ds(pl.multiple_of(ti * tile, tile), tile) for ti in tiles]
        items = [(u, d) for u in range(GLA_LOCAL_UNROLL) for d in dirs]
        g_tiles = [d.g_ref[rows[u], :] for u, d in items]
        cums = [_tile_decays(g, d.chunk_sum, d.fwd, grp) for (u, d), g in zip(items, g_tiles)]
        q2 = [q_ref[sl, :].astype(F32) for sl in rows]
        k2 = [k_ref[sl, :].astype(F32) for sl in rows]
        pair_terms = []
        for (u, d), (b, tot) in zip(items, cums):
            q_off, k_off, pair_tot = _pair_offsets(tot, d.fwd)
            q3, k3 = q2[u].reshape(grp, c, dk), k2[u].reshape(grp, c, dk)
            qb = q3 * jnp.exp2(b)
            kt, kt_pair = key_side(k3, b, tot, k_off)
            pair_terms.append(_nt_dot(qb.reshape(tile, dk).astype(BF16), kt.reshape(tile, dk).astype(BF16)))
            d.qb_scr[rows[u], :] = (qb * jnp.exp2(q_off)).reshape(tile, dk).astype(BF16)
            kt_t = kt_pair.reshape(tile, dk).T.astype(BF16)
            decay = jnp.exp2(pair_tot)
            for p in range(pairs_per_tile):
                d.kt_scr[tiles[u] * pairs_per_tile + p] = kt_t[:, p * pair:(p + 1) * pair]
                d.dm_scr[tiles[u] * pairs_per_tile + p] = _lane_broadcast_column(decay[p])

        def store_scores(per_item):
            for u in range(GLA_LOCAL_UNROLL):
                scores = functools.reduce(lambda x, y: x + y, [s for (iu, _), s in zip(items, per_item) if iu == u])
                for p in range(pairs_per_tile):
                    a_scr[pl.ds(pl.multiple_of(tiles[u] * tile + p * pair, pair), pair), :] = (
                        scores[p * pair:(p + 1) * pair, p * pair:(p + 1) * pair])

        def chunk_scores_mild():
            products = []
            for (u, d), (b, _) in zip(items, cums):
                flat = b.reshape(tile, dk)
                mid = _block_row(flat, c, c // 2 - 1 if d.fwd else c // 2)
                products.append(_nt_dot((q2[u] * jnp.exp2(flat - mid)).astype(BF16),
                                        (k2[u] * jnp.exp2(mid - flat)).astype(BF16)))
            store_scores([jnp.where(d.chunk_mask, inner, 0.0).astype(BF16)
                          + outer.astype(BF16) * d.split_masks[0][1]
                          for (_, d), inner, outer in zip(items, products, pair_terms)])

        def chunk_scores_any():
            diag_cums = [_block_sums(d.diag_sum, jnp.maximum(g, GLA_DIAG_DECAY_FLOOR))
                         for (u, d), g in zip(items, g_tiles)]
            products = []
            for (u, d), (b, _), bd in zip(items, cums, diag_cums):
                flat = b.reshape(tile, dk)
                split = []
                for h, mask in d.split_masks[1:]:
                    e = jnp.exp2(-jnp.abs(flat - _block_row(flat, 2 * h, h - 1 if d.fwd else h)))
                    split.append((_nt_dot((q2[u] * e).astype(BF16), (k2[u] * e).astype(BF16)), mask))
                mid = _block_row(bd, diag, diag // 2 - 1 if d.fwd else diag // 2)
                inner = _nt_dot((q2[u] * jnp.exp2(bd - mid)).astype(BF16),
                                (k2[u] * jnp.exp2(mid - bd)).astype(BF16))
                products.append((split, inner))
            per_item = []
            for (_, d), (split, inner), outer in zip(items, products, pair_terms):
                part = jnp.where(d.diag_mask, inner, 0.0).astype(BF16) + outer.astype(BF16) * d.split_masks[0][1]
                for product, mask in split:
                    part = part + product.astype(BF16) * mask
                per_item.append(part)
            store_scores(per_item)

        lowest = jnp.min(functools.reduce(jnp.minimum, g_tiles))
        pl.when(lowest >= GLA_MILD_DECAY_FLOOR)(chunk_scores_mild)
        pl.when(lowest < GLA_MILD_DECAY_FLOOR)(chunk_scores_any)
        return carry

    lax.fori_loop(0, t // (tile * GLA_LOCAL_UNROLL), local_body, 0)

    def scan_body(i, carry, finish):
        steps = []
        for u in range(GLA_SCAN_UNROLL):
            for d in dirs:
                step = i * GLA_SCAN_UNROLL + u
                j = step if d.fwd else n_pairs - 1 - step
                steps.append((d, j, pl.ds(pl.multiple_of(j * pair, pair), pair)))
        increments = [_dot(d.kt_scr[j], v_ref[sl, :]) for d, j, sl in steps]
        local = [None if finish else _dot(a_scr[sl, :], v_ref[sl, :]) for _, _, sl in steps]
        outs = []
        for (d, j, sl), inc in zip(steps, increments):
            qb_scr, dm_scr, s_scr = d.qb_scr, d.dm_scr, d.s_scr
            s = s_scr[...]
            outs.append(_dot(qb_scr[sl, :], s.astype(BF16)))
            s_scr[...] = s * jnp.concatenate([dm_scr[j]] * (dv // dk), axis=1) + inc
        for (_, _, sl), o, loc in zip(steps, outs, local):
            if finish:
                y = _rmsnorm(o + o_scr[sl, :], gn_ref[...])
                out_ref[sl, :] = (y * sr_ref[sl, :].astype(F32)).astype(BF16)
            else:
                o_scr[sl, :] = o + loc
        return carry

    half = n_pairs // (2 * GLA_SCAN_UNROLL)
    lax.fori_loop(0, half, functools.partial(scan_body, finish=False), 0)
    lax.fori_loop(half, 2 * half, functools.partial(scan_body, finish=True), 0)


def _gla_call(q, k, v, sr, gf, gb, kc, vc, gfc, gbc, gn):
    bsz, t, qk = q.shape
    d = v.shape[2]
    tc = kc.shape[1]
    dk = qk // GLA_HEADS
    dv = d // GLA_HEADS
    pair = GLA_KERNEL_CHUNK * GLA_PAIR
    n_pairs = t // pair
    tile = GLA_KERNEL_CHUNK * GLA_GROUP
    seq = lambda rows, width: pl.BlockSpec((None, rows, width), lambda b, h: (b, 0, h))
    scratch_shapes = [
        pltpu.VMEM((t, pair), BF16),
        pltpu.VMEM((t, dv), F32),
        pltpu.VMEM((n_pairs, dk, pair), BF16), pltpu.VMEM((n_pairs, dk, pair), BF16),
        pltpu.VMEM((t, dk), BF16), pltpu.VMEM((t, dk), BF16),
        pltpu.VMEM((n_pairs, dk, dk), F32), pltpu.VMEM((n_pairs, dk, dk), F32),
        pltpu.VMEM((dk, dv), F32), pltpu.VMEM((dk, dv), F32),
    ]
    pipelined = (2 * _nbytes((t, dk), BF16) + 3 * _nbytes((t, dv), BF16) + 2 * _nbytes((t, dk), F32)
                 + _nbytes((tc, dk), BF16) + _nbytes((tc, dv), BF16) + 2 * _nbytes((tc, dk), F32)
                 + _nbytes((1, dv), F32))
    scratch = (_nbytes((t, pair), BF16) + _nbytes((t, dv), F32) + 2 * _nbytes((n_pairs, dk, pair), BF16)
               + 2 * _nbytes((t, dk), BF16) + 2 * _nbytes((n_pairs, dk, dk), F32) + 2 * _nbytes((dk, dv), F32))
    temps = 2 * GLA_LOCAL_UNROLL * (8 * _nbytes((tile, dk), F32) + 2 * _nbytes((tile, tile), F32))
    return pl.pallas_call(
        _gla_kernel,
        out_shape=jax.ShapeDtypeStruct((bsz, t, d), BF16),
        grid=(bsz, GLA_HEADS),
        in_specs=[seq(t, dk), seq(t, dk), seq(t, dv), seq(t, dv), seq(t, dk), seq(t, dk),
                  seq(tc, dk), seq(tc, dv), seq(tc, dk), seq(tc, dk),
                  pl.BlockSpec((1, dv), lambda b, h: (0, h))],
        out_specs=seq(t, dv),
        scratch_shapes=scratch_shapes,
        compiler_params=pltpu.CompilerParams(
            dimension_semantics=("parallel", "parallel"),
            vmem_limit_bytes=_vmem_limit(pipelined, scratch, temps),
        ),
        name="gla",
    )(q, k, v, sr, gf, gb, kc, vc, gfc, gbc, gn)


def _out_kernel(x_ref, h1_ref, ya_ref, yb_ref, ga1_ref, sh2_ref, sc2_ref, ga2_ref, g2_ref, gfin_ref,
                wg_ref, bg_ref, wco_ref, wgo_ref, wo_ref, wup_ref, wdn_ref, o_ref, y_scr, h2_scr):
    tm, d = x_ref.shape
    d_ff = wup_ref.shape[1]

    def merge_stage(j):
        cs = slice(j, j + COL_BLOCK)
        h1 = h1_ref[...]
        za = _dot(h1, wg_ref[:, j:j + COL_BLOCK]) + bg_ref[:, j:j + COL_BLOCK]
        zb = _dot(h1, wg_ref[:, d + j:d + j + COL_BLOCK]) + bg_ref[:, d + j:d + j + COL_BLOCK]
        pa = _dot(ya_ref[...], wco_ref[:, cs])
        pb = _dot(yb_ref[...], wgo_ref[:, cs])

        def epilogue():
            y_scr[:, cs] = (_sigmoid(za) * pa + _sigmoid(zb) * pb).astype(BF16)
        return epilogue

    _run_skewed([functools.partial(merge_stage, j) for j in range(0, d, COL_BLOCK)])
    o_ref[...] = x_ref[...] + ga1_ref[...] * _dot(y_scr[...], wo_ref[...])
    h2_scr[...] = _modulate(o_ref[...], g2_ref[...], sh2_ref[...], sc2_ref[...]).astype(BF16)

    parts = []

    def mlp_stage(j):
        u = _dot(h2_scr[...], wup_ref[:, j:j + FF_BLOCK])

        def epilogue():
            a = jnp.maximum(u, 0.0)
            parts.append(_dot((a * a).astype(BF16), wdn_ref[j:j + FF_BLOCK, :]))
        return epilogue

    _run_skewed([functools.partial(mlp_stage, j) for j in range(0, d_ff, FF_BLOCK)])
    x2 = o_ref[...] + ga2_ref[...] * functools.reduce(lambda a, b: a + b, parts)
    o_ref[...] = _rmsnorm(x2, gfin_ref[...])


def _out_call(x, h1, ya, yb, mod3, g2, gfin, wg, bg, wco, wgo, wo, wup, wdn):
    bsz, t, d = x.shape
    tm = OUT_TM
    tok = lambda: pl.BlockSpec((None, tm, d), lambda b, i: (b, i, 0))
    modrow = lambda col: pl.BlockSpec((None, 1, d), lambda b, i: (b, 0, col))
    pipelined = 2 * _nbytes((tm, d), F32) + 3 * _nbytes((tm, d), BF16) + 4 * _nbytes((1, d), F32)
    resident = sum(_nbytes(a.shape, a.dtype) for a in (g2, gfin, wg, bg, wco, wgo, wo, wup, wdn))
    scratch = 2 * _nbytes((tm, d), BF16)
    temps = 10 * _nbytes((tm, d), F32)
    return pl.pallas_call(
        _out_kernel,
        out_shape=jax.ShapeDtypeStruct((bsz, t, d), F32),
        grid=(bsz, t // tm),
        in_specs=[tok(), tok(), tok(), tok()] + [modrow(cidx) for cidx in (2, 3, 4, 5)] + [_resident()] * 9,
        out_specs=tok(),
        scratch_shapes=[pltpu.VMEM((tm, d), BF16), pltpu.VMEM((tm, d), BF16)],
        compiler_params=pltpu.CompilerParams(
            dimension_semantics=("parallel", "parallel"),
            vmem_limit_bytes=_vmem_limit(pipelined, resident + scratch, temps),
        ),
        name="out",
    )(x, h1, ya, yb, mod3, mod3, mod3, mod3, g2, gfin, wg, bg, wco, wgo, wo, wup, wdn)


def kernel(x, c, ctx, c_ctx, w_ada, b_ada, g_norm1, w_in, b_in, conv_w, w_conv_out, w_a2_f, b_a_f,
           w_a2_b, b_a_b, g_gla_norm, w_gla_out, w_o, g_norm2, w_up, w_down, g_final):
    depth = w_ada.shape[0]
    assert depth == 1, "only the single-layer block is implemented"
    bsz, t, d = x.shape
    qk = w_a2_f.shape[2]
    dk = qk // GLA_HEADS
    rank = w_a2_f.shape[1]
    gla_tile = GLA_GROUP * GLA_KERNEL_CHUNK
    assert t % INPROJ_TM == 0 and t % OUT_TM == 0 and INPROJ_TM % GRID_W == 0
    assert GLA_PAIR == 2 and GLA_GROUP % GLA_PAIR == 0 and GLA_PAIR * GLA_KERNEL_CHUNK == dk
    gla_pair = GLA_PAIR * GLA_KERNEL_CHUNK
    assert t % (GLA_LOCAL_UNROLL * gla_tile) == 0 and ctx.shape[1] % gla_tile == 0
    assert t % (2 * GLA_SCAN_UNROLL * gla_pair) == 0
    assert bsz + 1 <= MOD_ROWS and 2 * rank <= LR_PAD and 5 * d + 2 * qk + LR_PAD <= w_in.shape[2]

    cc = jnp.zeros((MOD_ROWS, d), F32).at[:bsz].set(c).at[bsz].set(c_ctx)
    mod = _ada_call(cc, w_ada[0], b_ada[0][None, :])
    mod3 = mod.reshape(MOD_ROWS, 1, N_MOD * d)

    o_lr = 5 * d + 2 * qk
    o_g = o_lr + 2 * rank
    w_main, wg = _split_weight_call(w_in[0], o_lr + LR_PAD, w_in.shape[2] - o_g)
    b_main = b_in[0][None, :o_lr + LR_PAD]
    bg = b_in[0][None, o_g:]
    wa2 = jnp.zeros((LR_PAD, 2 * qk), F32)
    wa2 = wa2.at[:rank, :qk].set(w_a2_f[0]).at[rank:2 * rank, qk:].set(w_a2_b[0]).astype(BF16)
    ba2 = jnp.concatenate([b_a_f[0], b_a_b[0]])[None, :]
    g1 = g_norm1[0][None, :]

    h1, ya, q, k, v, sr, gf, gb = _inproj_call(x, mod3, g1, w_main, b_main, conv_w[0], wa2, ba2, qk, dk)
    kc, vc, gfc, gbc = _inproj_ctx_call(ctx, mod3, bsz, g1, w_main, b_main, wa2, ba2, qk)
    yb = _gla_call(q, k, v, sr, gf, gb, kc, vc, gfc, gbc, g_gla_norm[0][None, :])
    return _out_call(x, h1, ya, yb, mod3, g_norm2[0][None, :], g_final[None, :], wg, bg,
                     w_conv_out[0].astype(BF16), w_gla_out[0].astype(BF16), w_o[0].astype(BF16),
                     w_up[0].astype(BF16), w_down[0].astype(BF16))
```

```python
import functools
from typing import NamedTuple

import jax
import jax.numpy as jnp
from jax import lax
from jax.experimental import pallas as pl
from jax.experimental.pallas import tpu as pltpu

F32 = jnp.float32
BF16 = jnp.bfloat16

GLA_HEADS = 4
GLA_TAU = 16.0
GRID_W = 64
N_MOD = 6
RMS_EPS = 1e-6
LOG2_E = 1.4426950408889634

V7X_LANES = 128
V7X_VMEM_LIMIT_BYTES = 60000 * 1024

LR_PAD = V7X_LANES
GLA_KERNEL_CHUNK = 64
GLA_DIAG_BLOCK = 8
GLA_MAX_EXPONENT = 96.0
GLA_DIAG_DECAY_FLOOR = -GLA_MAX_EXPONENT / (GLA_DIAG_BLOCK // 2)
GLA_MILD_DECAY_FLOOR = -GLA_MAX_EXPONENT / (GLA_KERNEL_CHUNK // 2)
GLA_PAIR = 2
GLA_GROUP = 4
GLA_LOCAL_UNROLL = 4
GLA_SCAN_UNROLL = 8
INPROJ_TM = 1024
OUT_TM = 512
COL_BLOCK = 256
FF_BLOCK = 1024
MOD_ROWS = 16


def _vmem_limit(pipelined_bytes, resident_bytes, temp_bytes):
    need = 2 * pipelined_bytes + resident_bytes + temp_bytes
    return int(min(V7X_VMEM_LIMIT_BYTES, need))


def _nbytes(shape, dtype):
    n = 1
    for s in shape:
        n *= s
    return n * jnp.dtype(dtype).itemsize


def _dot(a, b):
    return jnp.dot(a, b, preferred_element_type=F32)


def _nt_dot(a, b):
    return lax.dot_general(a, b, (((1,), (1,)), ((), ())), preferred_element_type=F32)


def _run_skewed(stages):
    pending = None
    for stage in stages:
        epilogue = stage()
        if pending is not None:
            pending()
        pending = epilogue
    if pending is not None:
        pending()


def _sigmoid(x):
    return 1.0 / (1.0 + jnp.exp(-x))


def _log_sigmoid(x):
    return jnp.minimum(x, 0.0) - jnp.log(1.0 + jnp.exp(-jnp.abs(x)))


def _rmsnorm(x, g):
    return x * lax.rsqrt(jnp.mean(x * x, axis=-1, keepdims=True) + RMS_EPS) * g


def _modulate(x, g, shift, scale):
    return _rmsnorm(x, g) * (1.0 + scale) + shift


def _ada_kernel(c_ref, w_ref, b_ref, o_ref):
    c = c_ref[...]
    s = c * _sigmoid(c)
    o_ref[...] = _dot(s.astype(BF16), w_ref[...].astype(BF16)) + b_ref[...]


def _ada_call(cc, w_ada, b_ada):
    d = cc.shape[1]
    n_out = w_ada.shape[1]
    return pl.pallas_call(
        _ada_kernel,
        out_shape=jax.ShapeDtypeStruct((MOD_ROWS, n_out), F32),
        grid=(n_out // d,),
        in_specs=[
            pl.BlockSpec((MOD_ROWS, d), lambda j: (0, 0)),
            pl.BlockSpec((d, d), lambda j: (0, j)),
            pl.BlockSpec((1, d), lambda j: (0, j)),
        ],
        out_specs=pl.BlockSpec((MOD_ROWS, d), lambda j: (0, j)),
        compiler_params=pltpu.CompilerParams(
            dimension_semantics=("arbitrary",),
            vmem_limit_bytes=_vmem_limit(_nbytes((d, d), F32) + _nbytes((MOD_ROWS, 2 * d), F32),
                                         0, 4 * _nbytes((d, d), F32)),
        ),
        name="ada",
    )(cc, w_ada, b_ada)


def _log_decays(lr, wa2_ref, ba2_ref):
    xg = _dot(lr.astype(BF16), wa2_ref[...]) + ba2_ref[...]
    return _log_sigmoid(xg) * (LOG2_E / GLA_TAU)


def _inproj_kernel(x_ref, sh_ref, sc_ref, g1_ref, w_ref, b_ref, cw_ref, wa2_ref, ba2_ref,
                   h_ref, ya_ref, q_ref, k_ref, v_ref, sr_ref, gf_ref, gb_ref, *, q_scale):
    tm, d = x_ref.shape
    qk = q_ref.shape[1]
    h_ref[...] = _modulate(x_ref[...], g1_ref[...], sh_ref[...], sc_ref[...]).astype(BF16)

    col_in_row = lax.broadcasted_iota(jnp.int32, (tm, 1), 0) % GRID_W
    has_left = col_in_row != 0
    has_right = col_in_row != GRID_W - 1

    def proj(c0, width):
        return _nt_dot(h_ref[...], w_ref[c0:c0 + width, :]) + b_ref[:, c0:c0 + width]

    o_q = 3 * d
    o_v = o_q + 2 * qk
    o_r = o_v + d
    o_lr = o_r + d

    def decay_stage():
        lr = proj(o_lr, LR_PAD)

        def epilogue():
            g = _log_decays(lr, wa2_ref, ba2_ref)
            gf_ref[...] = g[:, 0:qk]
            gb_ref[...] = g[:, qk:2 * qk]
        return epilogue

    def conv_stage(j):
        cs = slice(j, j + COL_BLOCK)
        xa = proj(j, COL_BLOCK)
        ba = proj(d + j, COL_BLOCK)
        ca = proj(2 * d + j, COL_BLOCK)

        def epilogue():
            u = ca * xa
            left = jnp.where(has_left, pltpu.roll(u, 1, 0), 0.0)
            right = jnp.where(has_right, pltpu.roll(u, tm - 1, 0), 0.0)
            y = left * cw_ref[0:1, cs] + u * cw_ref[1:2, cs] + right * cw_ref[2:3, cs]
            ya_ref[:, cs] = (ba * y).astype(BF16)
        return epilogue

    def swish_stage(j):
        r = proj(o_r + j, COL_BLOCK)

        def epilogue():
            sr_ref[:, j:j + COL_BLOCK] = (r * _sigmoid(r)).astype(BF16)
        return epilogue

    def qk_stage():
        q = proj(o_q, qk)
        k = proj(o_q + qk, qk)

        def epilogue():
            q_ref[...] = (q * q_scale).astype(BF16)
            k_ref[...] = k.astype(BF16)
        return epilogue

    def value_stage(j):
        v = proj(o_v + j, COL_BLOCK)

        def epilogue():
            v_ref[:, j:j + COL_BLOCK] = v.astype(BF16)
        return epilogue

    _run_skewed([decay_stage] + [functools.partial(conv_stage, j) for j in range(0, d, COL_BLOCK)]
                + [functools.partial(swish_stage, j) for j in range(0, d, COL_BLOCK)] + [qk_stage]
                + [functools.partial(value_stage, j) for j in range(0, d, COL_BLOCK)])


def _inproj_ctx_kernel(x_ref, sh_ref, sc_ref, g1_ref, w_ref, b_ref, wa2_ref, ba2_ref,
                       k_ref, v_ref, gf_ref, gb_ref):
    d = x_ref.shape[1]
    qk = k_ref.shape[1]
    h = _modulate(x_ref[...], g1_ref[...], sh_ref[...], sc_ref[...]).astype(BF16)
    o_k = 3 * d + qk
    k_ref[...] = (_nt_dot(h, w_ref[o_k:o_k + qk, :]) + b_ref[:, o_k:o_k + qk]).astype(BF16)
    for j in range(0, d, COL_BLOCK):
        c0 = o_k + qk + j
        v_ref[:, j:j + COL_BLOCK] = (_nt_dot(h, w_ref[c0:c0 + COL_BLOCK, :]) + b_ref[:, c0:c0 + COL_BLOCK]).astype(BF16)
    c0 = o_k + qk + 2 * d
    lr = _nt_dot(h, w_ref[c0:c0 + LR_PAD, :]) + b_ref[:, c0:c0 + LR_PAD]
    g = _log_decays(lr, wa2_ref, ba2_ref)
    gf_ref[...] = g[:, 0:qk]
    gb_ref[...] = g[:, qk:2 * qk]


def _resident():
    return pl.BlockSpec(memory_space=pltpu.VMEM)


def _inproj_call(x, mod3, g1, w, b, cw, wa2, ba2, qk, dk):
    bsz, t, d = x.shape
    tm = INPROJ_TM
    tok = lambda n: pl.BlockSpec((None, tm, n), lambda b, i: (b, i, 0))
    modrow = lambda col: pl.BlockSpec((None, 1, d), lambda b, i: (b, 0, col))
    out_shapes = (
        jax.ShapeDtypeStruct((bsz, t, d), BF16),
        jax.ShapeDtypeStruct((bsz, t, d), BF16),
        jax.ShapeDtypeStruct((bsz, t, qk), BF16),
        jax.ShapeDtypeStruct((bsz, t, qk), BF16),
        jax.ShapeDtypeStruct((bsz, t, d), BF16),
        jax.ShapeDtypeStruct((bsz, t, d), BF16),
        jax.ShapeDtypeStruct((bsz, t, qk), F32),
        jax.ShapeDtypeStruct((bsz, t, qk), F32),
    )
    pipelined = (_nbytes((tm, d), F32) + 4 * _nbytes((tm, d), BF16) + 2 * _nbytes((tm, qk), BF16)
                 + 2 * _nbytes((tm, qk), F32) + 2 * _nbytes((1, d), F32))
    resident = sum(_nbytes(a.shape, a.dtype) for a in (g1, w, b, cw, wa2, ba2))
    temps = 24 * _nbytes((tm, COL_BLOCK), F32) + 3 * _nbytes((tm, 2 * qk), F32)
    return pl.pallas_call(
        functools.partial(_inproj_kernel, q_scale=dk ** -0.5),
        out_shape=out_shapes,
        grid=(bsz, t // tm),
        in_specs=[tok(d), modrow(0), modrow(1)] + [_resident()] * 6,
        out_specs=(tok(d), tok(d), tok(qk), tok(qk), tok(d), tok(d), tok(qk), tok(qk)),
        compiler_params=pltpu.CompilerParams(
            dimension_semantics=("parallel", "parallel"),
            vmem_limit_bytes=_vmem_limit(pipelined, resident, temps),
        ),
        name="inproj",
    )(x, mod3, mod3, g1, w, b, cw, wa2, ba2)


def _inproj_ctx_call(ctx, mod3, ctx_row, g1, w, b, wa2, ba2, qk):
    bsz, tc, d = ctx.shape
    tok = lambda n: pl.BlockSpec((None, tc, n), lambda b: (b, 0, 0))
    modrow = lambda col: pl.BlockSpec((None, 1, d), lambda b: (ctx_row, 0, col))
    out_shapes = (
        jax.ShapeDtypeStruct((bsz, tc, qk), BF16),
        jax.ShapeDtypeStruct((bsz, tc, d), BF16),
        jax.ShapeDtypeStruct((bsz, tc, qk), F32),
        jax.ShapeDtypeStruct((bsz, tc, qk), F32),
    )
    pipelined = (_nbytes((tc, d), F32) + _nbytes((tc, d), BF16) + _nbytes((tc, qk), BF16)
                 + 2 * _nbytes((tc, qk), F32) + 2 * _nbytes((1, d), F32))
    resident = sum(_nbytes(a.shape, a.dtype) for a in (g1, w, b, wa2, ba2))
    temps = _nbytes((tc, d), BF16) + 6 * _nbytes((tc, COL_BLOCK), F32) + 3 * _nbytes((tc, 2 * qk), F32)
    return pl.pallas_call(
        _inproj_ctx_kernel,
        out_shape=out_shapes,
        grid=(bsz,),
        in_specs=[tok(d), modrow(0), modrow(1)] + [_resident()] * 5,
        out_specs=(tok(qk), tok(d), tok(qk), tok(qk)),
        compiler_params=pltpu.CompilerParams(
            dimension_semantics=("parallel",),
            vmem_limit_bytes=_vmem_limit(pipelined, resident, temps),
        ),
        name="inproj_ctx",
    )(ctx, mod3, mod3, g1, w, b, wa2, ba2)


class _ScanDir(NamedTuple):
    fwd: bool
    g_ref: object
    gc_ref: object
    chunk_sum: object
    diag_sum: object
    split_masks: tuple
    chunk_mask: object
    diag_mask: object
    kt_scr: object
    qb_scr: object
    dm_scr: object
    s_scr: object


def _block_sums(sum_matrix, g):
    dk = g.shape[1]
    g_hi = g.astype(BF16)
    g_lo = (g - g_hi.astype(F32)).astype(BF16)
    bb = _dot(sum_matrix, jnp.concatenate([g_hi, g_lo], axis=1))
    return bb[:, :dk] + bb[:, dk:]


def _tile_decays(g, sum_matrix, fwd, n_chunks):
    rows, dk = g.shape
    c = rows // n_chunks
    b = _block_sums(sum_matrix, g).reshape(n_chunks, c, dk)
    return b, (b[:, c - 1:c, :] if fwd else b[:, 0:1, :])


def _block_row(x, block, row):
    rows, n = x.shape
    x3 = x.reshape(rows // block, block, n)
    return jnp.broadcast_to(x3[:, row:row + 1, :], x3.shape).reshape(rows, n)


def _pair_offsets(tot, fwd):
    n_chunks = tot.shape[0]
    zero = jnp.zeros_like(tot[0:1])
    on_odd = jnp.concatenate([t for p in range(0, n_chunks, GLA_PAIR) for t in (zero, tot[p:p + 1])], axis=0)
    on_even = jnp.concatenate([t for p in range(0, n_chunks, GLA_PAIR) for t in (tot[p + 1:p + 2], zero)], axis=0)
    pair_tot = jnp.concatenate([tot[p:p + 1] + tot[p + 1:p + 2] for p in range(0, n_chunks, GLA_PAIR)], axis=0)
    return (on_odd, on_even, pair_tot) if fwd else (on_even, on_odd, pair_tot)


def _lane_broadcast_column(row):
    n = row.shape[1]
    return jnp.broadcast_to(row, (n, n)).T


def _gla_kernel(q_ref, k_ref, v_ref, sr_ref, gf_ref, gb_ref, kc_ref, vc_ref, gfc_ref, gbc_ref, gn_ref,
                out_ref, a_scr, o_scr, ktf_scr, ktb_scr, qbf_scr, qbb_scr, dmf_scr, dmb_scr, sf_scr, sb_scr):
    c = GLA_KERNEL_CHUNK
    grp = GLA_GROUP
    tile = c * grp
    pair = c * GLA_PAIR
    pairs_per_tile = grp // GLA_PAIR
    diag = GLA_DIAG_BLOCK
    t, dk = q_ref.shape
    dv = v_ref.shape[1]
    n_pairs = t // pair
    row = lax.broadcasted_iota(jnp.int32, (tile, tile), 0)
    col = lax.broadcasted_iota(jnp.int32, (tile, tile), 1)
    halves = []
    half = pair // 2
    while half >= diag:
        halves.append(half)
        half //= 2

    def same_block(size):
        return (row // size) == (col // size)

    def scan_dir(fwd, g_ref, gc_ref, kt_scr, qb_scr, dm_scr, s_scr):
        seen = (row >= col) if fwd else (row <= col)
        split_masks = []
        for h in halves:
            later, earlier = (row % (2 * h) >= h), (col % (2 * h) < h)
            if not fwd:
                later, earlier = (row % (2 * h) < h), (col % (2 * h) >= h)
            split_masks.append((h, (same_block(2 * h) & later & earlier).astype(BF16)))
        return _ScanDir(fwd, g_ref, gc_ref, (same_block(c) & seen).astype(BF16),
                        (same_block(diag) & seen).astype(BF16), tuple(split_masks),
                        same_block(c) & seen, same_block(diag) & seen, kt_scr, qb_scr, dm_scr, s_scr)

    dirs = (scan_dir(True, gf_ref, gfc_ref, ktf_scr, qbf_scr, dmf_scr, sf_scr),
            scan_dir(False, gb_ref, gbc_ref, ktb_scr, qbb_scr, dmb_scr, sb_scr))

    def key_side(k3, b, tot, k_off):
        kt = k3 * jnp.exp2(tot - b)
        return kt, kt * jnp.exp2(k_off)

    ctx_tiles = kc_ref.shape[0] // tile
    ctx_items = [(d, ti) for d in dirs for ti in range(ctx_tiles)]
    ctx_cums = [_tile_decays(d.gc_ref[ti * tile:(ti + 1) * tile, :], d.chunk_sum, d.fwd, grp)
                for d, ti in ctx_items]
    ctx_terms = {}
    for (d, ti), (b, tot) in zip(ctx_items, ctx_cums):
        _, k_off, pair_tot = _pair_offsets(tot, d.fwd)
        kc3 = kc_ref[ti * tile:(ti + 1) * tile, :].astype(F32).reshape(grp, c, dk)
        _, kt_pair = key_side(kc3, b, tot, k_off)
        kt_t = kt_pair.reshape(tile, dk).T.astype(BF16)
        decay = jnp.exp2(pair_tot)
        for p in range(pairs_per_tile):
            r0 = ti * tile + p * pair
            ctx_terms[(d.fwd, ti * pairs_per_tile + p)] = (
                _dot(kt_t[:, p * pair:(p + 1) * pair], vc_ref[r0:r0 + pair, :]),
                _lane_broadcast_column(decay[p]))
    for d in dirs:
        order = range(ctx_tiles * pairs_per_tile)
        s = jnp.zeros((dk, dv), F32)
        for p in (order if d.fwd else reversed(order)):
            inc, dm = ctx_terms[(d.fwd, p)]
            s = s * jnp.concatenate([dm] * (dv // dk), axis=1) + inc
        d.s_scr[...] = s

    def local_body(i, carry):
        tiles = [i * GLA_LOCAL_UNROLL + u for u in range(GLA_LOCAL_UNROLL)]
        rows = [pl.ds(pl.multiple_of(ti * tile, tile), tile) for ti in tiles]
        items = [(u, d) for u in range(GLA_LOCAL_UNROLL) for d in dirs]
        g_tiles = [d.g_ref[rows[u], :] for u, d in items]
        cums = [_tile_decays(g, d.chunk_sum, d.fwd, grp) for (u, d), g in zip(items, g_tiles)]
        q2 = [q_ref[sl, :].astype(F32) for sl in rows]
        k2 = [k_ref[sl, :].astype(F32) for sl in rows]
        pair_terms = []
        for (u, d), (b, tot) in zip(items, cums):
            q_off, k_off, pair_tot = _pair_offsets(tot, d.fwd)
            q3, k3 = q2[u].reshape(grp, c, dk), k2[u].reshape(grp, c, dk)
            qb = q3 * jnp.exp2(b)
            kt, kt_pair = key_side(k3, b, tot, k_off)
            pair_terms.append(_nt_dot(qb.reshape(tile, dk).astype(BF16), kt.reshape(tile, dk).astype(BF16)))
            d.qb_scr[rows[u], :] = (qb * jnp.exp2(q_off)).reshape(tile, dk).astype(BF16)
            kt_t = kt_pair.reshape(tile, dk).T.astype(BF16)
            decay = jnp.exp2(pair_tot)
            for p in range(pairs_per_tile):
                d.kt_scr[tiles[u] * pairs_per_tile + p] = kt_t[:, p * pair:(p + 1) * pair]
                d.dm_scr[tiles[u] * pairs_per_tile + p] = _lane_broadcast_column(decay[p])

        def store_scores(per_item):
            for u in range(GLA_LOCAL_UNROLL):
                scores = functools.reduce(lambda x, y: x + y, [s for (iu, _), s in zip(items, per_item) if iu == u])
                for p in range(pairs_per_tile):
                    a_scr[pl.ds(pl.multiple_of(tiles[u] * tile + p * pair, pair), pair), :] = (
                        scores[p * pair:(p + 1) * pair, p * pair:(p + 1) * pair])

        def chunk_scores_mild():
            products = []
            for (u, d), (b, _) in zip(items, cums):
                flat = b.reshape(tile, dk)
                mid = _block_row(flat, c, c // 2 - 1 if d.fwd else c // 2)
                products.append(_nt_dot((q2[u] * jnp.exp2(flat - mid)).astype(BF16),
                                        (k2[u] * jnp.exp2(mid - flat)).astype(BF16)))
            store_scores([jnp.where(d.chunk_mask, inner, 0.0).astype(BF16)
                          + outer.astype(BF16) * d.split_masks[0][1]
                          for (_, d), inner, outer in zip(items, products, pair_terms)])

        def chunk_scores_any():
            diag_cums = [_block_sums(d.diag_sum, jnp.maximum(g, GLA_DIAG_DECAY_FLOOR))
                         for (u, d), g in zip(items, g_tiles)]
            products = []
            for (u, d), (b, _), bd in zip(items, cums, diag_cums):
                flat = b.reshape(tile, dk)
                split = []
                for h, mask in d.split_masks[1:]:
                    e = jnp.exp2(-jnp.abs(flat - _block_row(flat, 2 * h, h - 1 if d.fwd else h)))
                    split.append((_nt_dot((q2[u] * e).astype(BF16), (k2[u] * e).astype(BF16)), mask))
                mid = _block_row(bd, diag, diag // 2 - 1 if d.fwd else diag // 2)
                inner = _nt_dot((q2[u] * jnp.exp2(bd - mid)).astype(BF16),
                                (k2[u] * jnp.exp2(mid - bd)).astype(BF16))
                products.append((split, inner))
            per_item = []
            for (_, d), (split, inner), outer in zip(items, products, pair_terms):
                part = jnp.where(d.diag_mask, inner, 0.0).astype(BF16) + outer.astype(BF16) * d.split_masks[0][1]
                for product, mask in split:
                    part = part + product.astype(BF16) * mask
                per_item.append(part)
            store_scores(per_item)

        lowest = jnp.min(functools.reduce(jnp.minimum, g_tiles))
        pl.when(lowest >= GLA_MILD_DECAY_FLOOR)(chunk_scores_mild)
        pl.when(lowest < GLA_MILD_DECAY_FLOOR)(chunk_scores_any)
        return carry

    lax.fori_loop(0, t // (tile * GLA_LOCAL_UNROLL), local_body, 0)

    def scan_body(i, carry, finish):
        steps = []
        for u in range(GLA_SCAN_UNROLL):
            for d in dirs:
                step = i * GLA_SCAN_UNROLL + u
                j = step if d.fwd else n_pairs - 1 - step
                steps.append((d, j, pl.ds(pl.multiple_of(j * pair, pair), pair)))
        increments = [_dot(d.kt_scr[j], v_ref[sl, :]) for d, j, sl in steps]
        local = [None if finish else _dot(a_scr[sl, :], v_ref[sl, :]) for _, _, sl in steps]
        outs = []
        for (d, j, sl), inc in zip(steps, increments):
            qb_scr, dm_scr, s_scr = d.qb_scr, d.dm_scr, d.s_scr
            s = s_scr[...]
            outs.append(_dot(qb_scr[sl, :], s.astype(BF16)))
            s_scr[...] = s * jnp.concatenate([dm_scr[j]] * (dv // dk), axis=1) + inc
        for (_, _, sl), o, loc in zip(steps, outs, local):
            if finish:
                y = _rmsnorm(o + o_scr[sl, :], gn_ref[...])
                out_ref[sl, :] = (y * sr_ref[sl, :].astype(F32)).astype(BF16)
            else:
                o_scr[sl, :] = o + loc
        return carry

    half = n_pairs // (2 * GLA_SCAN_UNROLL)
    lax.fori_loop(0, half, functools.partial(scan_body, finish=False), 0)
    lax.fori_loop(half, 2 * half, functools.partial(scan_body, finish=True), 0)


def _gla_call(q, k, v, sr, gf, gb, kc, vc, gfc, gbc, gn):
    bsz, t, qk = q.shape
    d = v.shape[2]
    tc = kc.shape[1]
    dk = qk // GLA_HEADS
    dv = d // GLA_HEADS
    pair = GLA_KERNEL_CHUNK * GLA_PAIR
    n_pairs = t // pair
    tile = GLA_KERNEL_CHUNK * GLA_GROUP
    seq = lambda rows, width: pl.BlockSpec((None, rows, width), lambda b, h: (b, 0, h))
    scratch_shapes = [
        pltpu.VMEM((t, pair), BF16),
        pltpu.VMEM((t, dv), F32),
        pltpu.VMEM((n_pairs, dk, pair), BF16), pltpu.VMEM((n_pairs, dk, pair), BF16),
        pltpu.VMEM((t, dk), BF16), pltpu.VMEM((t, dk), BF16),
        pltpu.VMEM((n_pairs, dk, dk), F32), pltpu.VMEM((n_pairs, dk, dk), F32),
        pltpu.VMEM((dk, dv), F32), pltpu.VMEM((dk, dv), F32),
    ]
    pipelined = (2 * _nbytes((t, dk), BF16) + 3 * _nbytes((t, dv), BF16) + 2 * _nbytes((t, dk), F32)
                 + _nbytes((tc, dk), BF16) + _nbytes((tc, dv), BF16) + 2 * _nbytes((tc, dk), F32)
                 + _nbytes((1, dv), F32))
    scratch = (_nbytes((t, pair), BF16) + _nbytes((t, dv), F32) + 2 * _nbytes((n_pairs, dk, pair), BF16)
               + 2 * _nbytes((t, dk), BF16) + 2 * _nbytes((n_pairs, dk, dk), F32) + 2 * _nbytes((dk, dv), F32))
    temps = 2 * GLA_LOCAL_UNROLL * (8 * _nbytes((tile, dk), F32) + 2 * _nbytes((tile, tile), F32))
    return pl.pallas_call(
        _gla_kernel,
        out_shape=jax.ShapeDtypeStruct((bsz, t, d), BF16),
        grid=(bsz, GLA_HEADS),
        in_specs=[seq(t, dk), seq(t, dk), seq(t, dv), seq(t, dv), seq(t, dk), seq(t, dk),
                  seq(tc, dk), seq(tc, dv), seq(tc, dk), seq(tc, dk),
                  pl.BlockSpec((1, dv), lambda b, h: (0, h))],
        out_specs=seq(t, dv),
        scratch_shapes=scratch_shapes,
        compiler_params=pltpu.CompilerParams(
            dimension_semantics=("parallel", "parallel"),
            vmem_limit_bytes=_vmem_limit(pipelined, scratch, temps),
        ),
        name="gla",
    )(q, k, v, sr, gf, gb, kc, vc, gfc, gbc, gn)


def _out_kernel(x_ref, h1_ref, ya_ref, yb_ref, ga1_ref, sh2_ref, sc2_ref, ga2_ref, g2_ref, gfin_ref,
                wg_ref, bg_ref, wco_ref, wgo_ref, wo_ref, wup_ref, wdn_ref, o_ref, y_scr, h2_scr):
    tm, d = x_ref.shape
    d_ff = wup_ref.shape[1]

    def merge_stage(j):
        cs = slice(j, j + COL_BLOCK)
        h1 = h1_ref[...]
        za = _nt_dot(h1, wg_ref[j:j + COL_BLOCK, :]) + bg_ref[:, j:j + COL_BLOCK]
        zb = _nt_dot(h1, wg_ref[d + j:d + j + COL_BLOCK, :]) + bg_ref[:, d + j:d + j + COL_BLOCK]
        pa = _dot(ya_ref[...], wco_ref[:, cs])
        pb = _dot(yb_ref[...], wgo_ref[:, cs])

        def epilogue():
            y_scr[:, cs] = (_sigmoid(za) * pa + _sigmoid(zb) * pb).astype(BF16)
        return epilogue

    _run_skewed([functools.partial(merge_stage, j) for j in range(0, d, COL_BLOCK)])
    o_ref[...] = x_ref[...] + ga1_ref[...] * _dot(y_scr[...], wo_ref[...])
    h2_scr[...] = _modulate(o_ref[...], g2_ref[...], sh2_ref[...], sc2_ref[...]).astype(BF16)

    parts = []

    def mlp_stage(j):
        u = _dot(h2_scr[...], wup_ref[:, j:j + FF_BLOCK])

        def epilogue():
            a = jnp.maximum(u, 0.0)
            parts.append(_dot((a * a).astype(BF16), wdn_ref[j:j + FF_BLOCK, :]))
        return epilogue

    _run_skewed([functools.partial(mlp_stage, j) for j in range(0, d_ff, FF_BLOCK)])
    x2 = o_ref[...] + ga2_ref[...] * functools.reduce(lambda a, b: a + b, parts)
    o_ref[...] = _rmsnorm(x2, gfin_ref[...])


def _out_call(x, h1, ya, yb, mod3, g2, gfin, wg, bg, wco, wgo, wo, wup, wdn):
    bsz, t, d = x.shape
    tm = OUT_TM
    tok = lambda: pl.BlockSpec((None, tm, d), lambda b, i: (b, i, 0))
    modrow = lambda col: pl.BlockSpec((None, 1, d), lambda b, i: (b, 0, col))
    pipelined = 2 * _nbytes((tm, d), F32) + 3 * _nbytes((tm, d), BF16) + 4 * _nbytes((1, d), F32)
    resident = sum(_nbytes(a.shape, a.dtype) for a in (g2, gfin, wg, bg, wco, wgo, wo, wup, wdn))
    scratch = 2 * _nbytes((tm, d), BF16)
    temps = 10 * _nbytes((tm, d), F32)
    return pl.pallas_call(
        _out_kernel,
        out_shape=jax.ShapeDtypeStruct((bsz, t, d), F32),
        grid=(bsz, t // tm),
        in_specs=[tok(), tok(), tok(), tok()] + [modrow(cidx) for cidx in (2, 3, 4, 5)] + [_resident()] * 9,
        out_specs=tok(),
        scratch_shapes=[pltpu.VMEM((tm, d), BF16), pltpu.VMEM((tm, d), BF16)],
        compiler_params=pltpu.CompilerParams(
            dimension_semantics=("parallel", "parallel"),
            vmem_limit_bytes=_vmem_limit(pipelined, resident + scratch, temps),
        ),
        name="out",
    )(x, h1, ya, yb, mod3, mod3, mod3, mod3, g2, gfin, wg, bg, wco, wgo, wo, wup, wdn)


def kernel(x, c, ctx, c_ctx, w_ada, b_ada, g_norm1, w_in, b_in, conv_w, w_conv_out, w_a2_f, b_a_f,
           w_a2_b, b_a_b, g_gla_norm, w_gla_out, w_o, g_norm2, w_up, w_down, g_final):
    depth = w_ada.shape[0]
    assert depth == 1, "only the single-layer block is implemented"
    bsz, t, d = x.shape
    qk = w_a2_f.shape[2]
    dk = qk // GLA_HEADS
    rank = w_a2_f.shape[1]
    gla_tile = GLA_GROUP * GLA_KERNEL_CHUNK
    assert t % INPROJ_TM == 0 and t % OUT_TM == 0 and INPROJ_TM % GRID_W == 0
    assert GLA_PAIR == 2 and GLA_GROUP % GLA_PAIR == 0 and GLA_PAIR * GLA_KERNEL_CHUNK == dk
    gla_pair = GLA_PAIR * GLA_KERNEL_CHUNK
    assert t % (GLA_LOCAL_UNROLL * gla_tile) == 0 and ctx.shape[1] % gla_tile == 0
    assert t % (2 * GLA_SCAN_UNROLL * gla_pair) == 0
    assert bsz + 1 <= MOD_ROWS and 2 * rank <= LR_PAD and 5 * d + 2 * qk + LR_PAD <= w_in.shape[2]

    cc = jnp.zeros((MOD_ROWS, d), F32).at[:bsz].set(c).at[bsz].set(c_ctx)
    mod = _ada_call(cc, w_ada[0], b_ada[0][None, :])
    mod3 = mod.reshape(MOD_ROWS, 1, N_MOD * d)

    o_lr = 5 * d + 2 * qk
    o_g = o_lr + 2 * rank
    w_t = jnp.swapaxes(w_in, 1, 2)[0]
    w_main = w_t[:o_lr + LR_PAD].astype(BF16)
    wg = w_t[o_g:].astype(BF16)
    b_main = b_in[0][None, :o_lr + LR_PAD]
    bg = b_in[0][None, o_g:]
    wa2 = jnp.zeros((LR_PAD, 2 * qk), F32)
    wa2 = wa2.at[:rank, :qk].set(w_a2_f[0]).at[rank:2 * rank, qk:].set(w_a2_b[0]).astype(BF16)
    ba2 = jnp.concatenate([b_a_f[0], b_a_b[0]])[None, :]
    g1 = g_norm1[0][None, :]

    h1, ya, q, k, v, sr, gf, gb = _inproj_call(x, mod3, g1, w_main, b_main, conv_w[0], wa2, ba2, qk, dk)
    kc, vc, gfc, gbc = _inproj_ctx_call(ctx, mod3, bsz, g1, w_main, b_main, wa2, ba2, qk)
    yb = _gla_call(q, k, v, sr, gf, gb, kc, vc, gfc, gbc, g_gla_norm[0][None, :])
    return _out_call(x, h1, ya, yb, mod3, g_norm2[0][None, :], g_final[None, :], wg, bg,
                     w_conv_out[0].astype(BF16), w_gla_out[0].astype(BF16), w_o[0].astype(BF16),
                     w_up[0].astype(BF16), w_down[0].astype(BF16))
```

```python
import functools
from typing import NamedTuple

import jax
import jax.numpy as jnp
from jax import lax
from jax.experimental import pallas as pl
from jax.experimental.pallas import tpu as pltpu

F32 = jnp.float32
BF16 = jnp.bfloat16

GLA_HEADS = 4
GLA_TAU = 16.0
GRID_W = 64
N_MOD = 6
RMS_EPS = 1e-6
LOG2_E = 1.4426950408889634

V7X_LANES = 128
V7X_VMEM_LIMIT_BYTES = 60000 * 1024

LR_PAD = V7X_LANES
GLA_KERNEL_CHUNK = 64
GLA_DIAG_BLOCK = 8
GLA_MAX_EXPONENT = 96.0
GLA_DIAG_DECAY_FLOOR = -GLA_MAX_EXPONENT / (GLA_DIAG_BLOCK // 2)
GLA_MILD_DECAY_FLOOR = -GLA_MAX_EXPONENT / (GLA_KERNEL_CHUNK // 2)
GLA_PAIR = 2
GLA_GROUP = 4
GLA_LOCAL_UNROLL = 4
GLA_SCAN_UNROLL = 16
INPROJ_TM = 1024
OUT_TM = 512
COL_BLOCK = 256
FF_BLOCK = 1024
MOD_ROWS = 16


def _vmem_limit(pipelined_bytes, resident_bytes, temp_bytes):
    need = 2 * pipelined_bytes + resident_bytes + temp_bytes
    return int(min(V7X_VMEM_LIMIT_BYTES, need))


def _nbytes(shape, dtype):
    n = 1
    for s in shape:
        n *= s
    return n * jnp.dtype(dtype).itemsize


def _dot(a, b):
    return jnp.dot(a, b, preferred_element_type=F32)


def _nt_dot(a, b):
    return lax.dot_general(a, b, (((1,), (1,)), ((), ())), preferred_element_type=F32)


def _run_skewed(stages):
    pending = None
    for stage in stages:
        epilogue = stage()
        if pending is not None:
            pending()
        pending = epilogue
    if pending is not None:
        pending()


def _sigmoid(x):
    return 1.0 / (1.0 + jnp.exp(-x))


def _log_sigmoid(x):
    return jnp.minimum(x, 0.0) - jnp.log(1.0 + jnp.exp(-jnp.abs(x)))


def _rmsnorm(x, g):
    return x * lax.rsqrt(jnp.mean(x * x, axis=-1, keepdims=True) + RMS_EPS) * g


def _modulate(x, g, shift, scale):
    return _rmsnorm(x, g) * (1.0 + scale) + shift


def _ada_kernel(c_ref, w_ref, b_ref, o_ref):
    c = c_ref[...]
    s = c * _sigmoid(c)
    o_ref[...] = _dot(s.astype(BF16), w_ref[...].astype(BF16)) + b_ref[...]


def _ada_call(cc, w_ada, b_ada):
    d = cc.shape[1]
    n_out = w_ada.shape[1]
    return pl.pallas_call(
        _ada_kernel,
        out_shape=jax.ShapeDtypeStruct((MOD_ROWS, n_out), F32),
        grid=(n_out // d,),
        in_specs=[
            pl.BlockSpec((MOD_ROWS, d), lambda j: (0, 0)),
            pl.BlockSpec((d, d), lambda j: (0, j)),
            pl.BlockSpec((1, d), lambda j: (0, j)),
        ],
        out_specs=pl.BlockSpec((MOD_ROWS, d), lambda j: (0, j)),
        compiler_params=pltpu.CompilerParams(
            dimension_semantics=("arbitrary",),
            vmem_limit_bytes=_vmem_limit(_nbytes((d, d), F32) + _nbytes((MOD_ROWS, 2 * d), F32),
                                         0, 4 * _nbytes((d, d), F32)),
        ),
        name="ada",
    )(cc, w_ada, b_ada)


def _log_decays(lr, wa2_ref, ba2_ref):
    xg = _dot(lr.astype(BF16), wa2_ref[...]) + ba2_ref[...]
    return _log_sigmoid(xg) * (LOG2_E / GLA_TAU)


def _inproj_kernel(x_ref, sh_ref, sc_ref, g1_ref, w_ref, b_ref, cw_ref, wa2_ref, ba2_ref,
                   h_ref, ya_ref, q_ref, k_ref, v_ref, sr_ref, gf_ref, gb_ref, *, q_scale):
    tm, d = x_ref.shape
    qk = q_ref.shape[1]
    h_ref[...] = _modulate(x_ref[...], g1_ref[...], sh_ref[...], sc_ref[...]).astype(BF16)

    col_in_row = lax.broadcasted_iota(jnp.int32, (tm, 1), 0) % GRID_W
    has_left = col_in_row != 0
    has_right = col_in_row != GRID_W - 1

    def proj(c0, width):
        return _dot(h_ref[...], w_ref[:, c0:c0 + width]) + b_ref[:, c0:c0 + width]

    o_q = 3 * d
    o_v = o_q + 2 * qk
    o_r = o_v + d
    o_lr = o_r + d

    def decay_stage():
        lr = proj(o_lr, LR_PAD)

        def epilogue():
            g = _log_decays(lr, wa2_ref, ba2_ref)
            gf_ref[...] = g[:, 0:qk]
            gb_ref[...] = g[:, qk:2 * qk]
        return epilogue

    def conv_stage(j):
        cs = slice(j, j + COL_BLOCK)
        xa = proj(j, COL_BLOCK)
        ba = proj(d + j, COL_BLOCK)
        ca = proj(2 * d + j, COL_BLOCK)

        def epilogue():
            u = ca * xa
            left = jnp.where(has_left, pltpu.roll(u, 1, 0), 0.0)
            right = jnp.where(has_right, pltpu.roll(u, tm - 1, 0), 0.0)
            y = left * cw_ref[0:1, cs] + u * cw_ref[1:2, cs] + right * cw_ref[2:3, cs]
            ya_ref[:, cs] = (ba * y).astype(BF16)
        return epilogue

    def swish_stage(j):
        r = proj(o_r + j, COL_BLOCK)

        def epilogue():
            sr_ref[:, j:j + COL_BLOCK] = (r * _sigmoid(r)).astype(BF16)
        return epilogue

    def qk_stage():
        q = proj(o_q, qk)
        k = proj(o_q + qk, qk)

        def epilogue():
            q_ref[...] = (q * q_scale).astype(BF16)
            k_ref[...] = k.astype(BF16)
        return epilogue

    def value_stage(j):
        v = proj(o_v + j, COL_BLOCK)

        def epilogue():
            v_ref[:, j:j + COL_BLOCK] = v.astype(BF16)
        return epilogue

    _run_skewed([decay_stage] + [functools.partial(conv_stage, j) for j in range(0, d, COL_BLOCK)]
                + [functools.partial(swish_stage, j) for j in range(0, d, COL_BLOCK)] + [qk_stage]
                + [functools.partial(value_stage, j) for j in range(0, d, COL_BLOCK)])


def _inproj_ctx_kernel(x_ref, sh_ref, sc_ref, g1_ref, w_ref, b_ref, wa2_ref, ba2_ref,
                       k_ref, v_ref, gf_ref, gb_ref):
    d = x_ref.shape[1]
    qk = k_ref.shape[1]
    h = _modulate(x_ref[...], g1_ref[...], sh_ref[...], sc_ref[...]).astype(BF16)
    o_k = 3 * d + qk
    k_ref[...] = (_dot(h, w_ref[:, o_k:o_k + qk]) + b_ref[:, o_k:o_k + qk]).astype(BF16)
    for j in range(0, d, COL_BLOCK):
        c0 = o_k + qk + j
        v_ref[:, j:j + COL_BLOCK] = (_dot(h, w_ref[:, c0:c0 + COL_BLOCK]) + b_ref[:, c0:c0 + COL_BLOCK]).astype(BF16)
    c0 = o_k + qk + 2 * d
    lr = _dot(h, w_ref[:, c0:c0 + LR_PAD]) + b_ref[:, c0:c0 + LR_PAD]
    g = _log_decays(lr, wa2_ref, ba2_ref)
    gf_ref[...] = g[:, 0:qk]
    gb_ref[...] = g[:, qk:2 * qk]


def _resident():
    return pl.BlockSpec(memory_space=pltpu.VMEM)


def _inproj_call(x, mod3, g1, w, b, cw, wa2, ba2, qk, dk):
    bsz, t, d = x.shape
    tm = INPROJ_TM
    tok = lambda n: pl.BlockSpec((None, tm, n), lambda b, i: (b, i, 0))
    modrow = lambda col: pl.BlockSpec((None, 1, d), lambda b, i: (b, 0, col))
    out_shapes = (
        jax.ShapeDtypeStruct((bsz, t, d), BF16),
        jax.ShapeDtypeStruct((bsz, t, d), BF16),
        jax.ShapeDtypeStruct((bsz, t, qk), BF16),
        jax.ShapeDtypeStruct((bsz, t, qk), BF16),
        jax.ShapeDtypeStruct((bsz, t, d), BF16),
        jax.ShapeDtypeStruct((bsz, t, d), BF16),
        jax.ShapeDtypeStruct((bsz, t, qk), F32),
        jax.ShapeDtypeStruct((bsz, t, qk), F32),
    )
    pipelined = (_nbytes((tm, d), F32) + 4 * _nbytes((tm, d), BF16) + 2 * _nbytes((tm, qk), BF16)
                 + 2 * _nbytes((tm, qk), F32) + 2 * _nbytes((1, d), F32))
    resident = sum(_nbytes(a.shape, a.dtype) for a in (g1, w, b, cw, wa2, ba2))
    temps = 24 * _nbytes((tm, COL_BLOCK), F32) + 3 * _nbytes((tm, 2 * qk), F32)
    return pl.pallas_call(
        functools.partial(_inproj_kernel, q_scale=dk ** -0.5),
        out_shape=out_shapes,
        grid=(bsz, t // tm),
        in_specs=[tok(d), modrow(0), modrow(1)] + [_resident()] * 6,
        out_specs=(tok(d), tok(d), tok(qk), tok(qk), tok(d), tok(d), tok(qk), tok(qk)),
        compiler_params=pltpu.CompilerParams(
            dimension_semantics=("parallel", "parallel"),
            vmem_limit_bytes=_vmem_limit(pipelined, resident, temps),
        ),
        name="inproj",
    )(x, mod3, mod3, g1, w, b, cw, wa2, ba2)


def _inproj_ctx_call(ctx, mod3, ctx_row, g1, w, b, wa2, ba2, qk):
    bsz, tc, d = ctx.shape
    tok = lambda n: pl.BlockSpec((None, tc, n), lambda b: (b, 0, 0))
    modrow = lambda col: pl.BlockSpec((None, 1, d), lambda b: (ctx_row, 0, col))
    out_shapes = (
        jax.ShapeDtypeStruct((bsz, tc, qk), BF16),
        jax.ShapeDtypeStruct((bsz, tc, d), BF16),
        jax.ShapeDtypeStruct((bsz, tc, qk), F32),
        jax.ShapeDtypeStruct((bsz, tc, qk), F32),
    )
    pipelined = (_nbytes((tc, d), F32) + _nbytes((tc, d), BF16) + _nbytes((tc, qk), BF16)
                 + 2 * _nbytes((tc, qk), F32) + 2 * _nbytes((1, d), F32))
    resident = sum(_nbytes(a.shape, a.dtype) for a in (g1, w, b, wa2, ba2))
    temps = _nbytes((tc, d), BF16) + 6 * _nbytes((tc, COL_BLOCK), F32) + 3 * _nbytes((tc, 2 * qk), F32)
    return pl.pallas_call(
        _inproj_ctx_kernel,
        out_shape=out_shapes,
        grid=(bsz,),
        in_specs=[tok(d), modrow(0), modrow(1)] + [_resident()] * 5,
        out_specs=(tok(qk), tok(d), tok(qk), tok(qk)),
        compiler_params=pltpu.CompilerParams(
            dimension_semantics=("parallel",),
            vmem_limit_bytes=_vmem_limit(pipelined, resident, temps),
        ),
        name="inproj_ctx",
    )(ctx, mod3, mod3, g1, w, b, wa2, ba2)


class _ScanDir(NamedTuple):
    fwd: bool
    g_ref: object
    gc_ref: object
    chunk_sum: object
    diag_sum: object
    split_masks: tuple
    chunk_mask: object
    diag_mask: object
    kt_scr: object
    qb_scr: object
    dm_scr: object
    s_scr: object


def _block_sums(sum_matrix, g):
    dk = g.shape[1]
    g_hi = g.astype(BF16)
    g_lo = (g - g_hi.astype(F32)).astype(BF16)
    bb = _dot(sum_matrix, jnp.concatenate([g_hi, g_lo], axis=1))
    return bb[:, :dk] + bb[:, dk:]


def _tile_decays(g, sum_matrix, fwd, n_chunks):
    rows, dk = g.shape
    c = rows // n_chunks
    b = _block_sums(sum_matrix, g).reshape(n_chunks, c, dk)
    return b, (b[:, c - 1:c, :] if fwd else b[:, 0:1, :])


def _block_row(x, block, row):
    rows, n = x.shape
    x3 = x.reshape(rows // block, block, n)
    return jnp.broadcast_to(x3[:, row:row + 1, :], x3.shape).reshape(rows, n)


def _pair_offsets(tot, fwd):
    n_chunks = tot.shape[0]
    zero = jnp.zeros_like(tot[0:1])
    on_odd = jnp.concatenate([t for p in range(0, n_chunks, GLA_PAIR) for t in (zero, tot[p:p + 1])], axis=0)
    on_even = jnp.concatenate([t for p in range(0, n_chunks, GLA_PAIR) for t in (tot[p + 1:p + 2], zero)], axis=0)
    pair_tot = jnp.concatenate([tot[p:p + 1] + tot[p + 1:p + 2] for p in range(0, n_chunks, GLA_PAIR)], axis=0)
    return (on_odd, on_even, pair_tot) if fwd else (on_even, on_odd, pair_tot)


def _lane_broadcast_column(row):
    n = row.shape[1]
    return jnp.broadcast_to(row, (n, n)).T


def _gla_kernel(q_ref, k_ref, v_ref, sr_ref, gf_ref, gb_ref, kc_ref, vc_ref, gfc_ref, gbc_ref, gn_ref,
                out_ref, a_scr, o_scr, ktf_scr, ktb_scr, qbf_scr, qbb_scr, dmf_scr, dmb_scr, sf_scr, sb_scr):
    c = GLA_KERNEL_CHUNK
    grp = GLA_GROUP
    tile = c * grp
    pair = c * GLA_PAIR
    pairs_per_tile = grp // GLA_PAIR
    diag = GLA_DIAG_BLOCK
    t, dk = q_ref.shape
    dv = v_ref.shape[1]
    n_pairs = t // pair
    row = lax.broadcasted_iota(jnp.int32, (tile, tile), 0)
    col = lax.broadcasted_iota(jnp.int32, (tile, tile), 1)
    halves = []
    half = pair // 2
    while half >= diag:
        halves.append(half)
        half //= 2

    def same_block(size):
        return (row // size) == (col // size)

    def scan_dir(fwd, g_ref, gc_ref, kt_scr, qb_scr, dm_scr, s_scr):
        seen = (row >= col) if fwd else (row <= col)
        split_masks = []
        for h in halves:
            later, earlier = (row % (2 * h) >= h), (col % (2 * h) < h)
            if not fwd:
                later, earlier = (row % (2 * h) < h), (col % (2 * h) >= h)
            split_masks.append((h, (same_block(2 * h) & later & earlier).astype(BF16)))
        return _ScanDir(fwd, g_ref, gc_ref, (same_block(c) & seen).astype(BF16),
                        (same_block(diag) & seen).astype(BF16), tuple(split_masks),
                        same_block(c) & seen, same_block(diag) & seen, kt_scr, qb_scr, dm_scr, s_scr)

    dirs = (scan_dir(True, gf_ref, gfc_ref, ktf_scr, qbf_scr, dmf_scr, sf_scr),
            scan_dir(False, gb_ref, gbc_ref, ktb_scr, qbb_scr, dmb_scr, sb_scr))

    def key_side(k3, b, tot, k_off):
        kt = k3 * jnp.exp2(tot - b)
        return kt, kt * jnp.exp2(k_off)

    ctx_tiles = kc_ref.shape[0] // tile
    ctx_items = [(d, ti) for d in dirs for ti in range(ctx_tiles)]
    ctx_cums = [_tile_decays(d.gc_ref[ti * tile:(ti + 1) * tile, :], d.chunk_sum, d.fwd, grp)
                for d, ti in ctx_items]
    ctx_terms = {}
    for (d, ti), (b, tot) in zip(ctx_items, ctx_cums):
        _, k_off, pair_tot = _pair_offsets(tot, d.fwd)
        kc3 = kc_ref[ti * tile:(ti + 1) * tile, :].astype(F32).reshape(grp, c, dk)
        _, kt_pair = key_side(kc3, b, tot, k_off)
        kt_t = kt_pair.reshape(tile, dk).T.astype(BF16)
        decay = jnp.exp2(pair_tot)
        for p in range(pairs_per_tile):
            r0 = ti * tile + p * pair
            ctx_terms[(d.fwd, ti * pairs_per_tile + p)] = (
                _dot(kt_t[:, p * pair:(p + 1) * pair], vc_ref[r0:r0 + pair, :]),
                _lane_broadcast_column(decay[p]))
    for d in dirs:
        order = range(ctx_tiles * pairs_per_tile)
        s = jnp.zeros((dk, dv), F32)
        for p in (order if d.fwd else reversed(order)):
            inc, dm = ctx_terms[(d.fwd, p)]
            s = s * jnp.concatenate([dm] * (dv // dk), axis=1) + inc
        d.s_scr[...] = s

    def local_body(i, carry):
        tiles = [i * GLA_LOCAL_UNROLL + u for u in range(GLA_LOCAL_UNROLL)]
        rows = [pl.ds(pl.multiple_of(ti * tile, tile), tile) for ti in tiles]
        items = [(u, d) for u in range(GLA_LOCAL_UNROLL) for d in dirs]
        g_tiles = [d.g_ref[rows[u], :] for u, d in items]
        cums = [_tile_decays(g, d.chunk_sum, d.fwd, grp) for (u, d), g in zip(items, g_tiles)]
        q2 = [q_ref[sl, :].astype(F32) for sl in rows]
        k2 = [k_ref[sl, :].astype(F32) for sl in rows]
        pair_terms = []
        for (u, d), (b, tot) in zip(items, cums):
            q_off, k_off, pair_tot = _pair_offsets(tot, d.fwd)
            q3, k3 = q2[u].reshape(grp, c, dk), k2[u].reshape(grp, c, dk)
            qb = q3 * jnp.exp2(b)
            kt, kt_pair = key_side(k3, b, tot, k_off)
            pair_terms.append(_nt_dot(qb.reshape(tile, dk).astype(BF16), kt.reshape(tile, dk).astype(BF16)))
            d.qb_scr[rows[u], :] = (qb * jnp.exp2(q_off)).reshape(tile, dk).astype(BF16)
            kt_t = kt_pair.reshape(tile, dk).T.astype(BF16)
            decay = jnp.exp2(pair_tot)
            for p in range(pairs_per_tile):
                d.kt_scr[tiles[u] * pairs_per_tile + p] = kt_t[:, p * pair:(p + 1) * pair]
                d.dm_scr[tiles[u] * pairs_per_tile + p] = _lane_broadcast_column(decay[p])

        def store_scores(per_item):
            for u in range(GLA_LOCAL_UNROLL):
                scores = functools.reduce(lambda x, y: x + y, [s for (iu, _), s in zip(items, per_item) if iu == u])
                for p in range(pairs_per_tile):
                    a_scr[pl.ds(pl.multiple_of(tiles[u] * tile + p * pair, pair), pair), :] = (
                        scores[p * pair:(p + 1) * pair, p * pair:(p + 1) * pair])

        def chunk_scores_mild():
            products = []
            for (u, d), (b, _) in zip(items, cums):
                flat = b.reshape(tile, dk)
                mid = _block_row(flat, c, c // 2 - 1 if d.fwd else c // 2)
                products.append(_nt_dot((q2[u] * jnp.exp2(flat - mid)).astype(BF16),
                                        (k2[u] * jnp.exp2(mid - flat)).astype(BF16)))
            store_scores([jnp.where(d.chunk_mask, inner, 0.0).astype(BF16)
                          + outer.astype(BF16) * d.split_masks[0][1]
                          for (_, d), inner, outer in zip(items, products, pair_terms)])

        def chunk_scores_any():
            diag_cums = [_block_sums(d.diag_sum, jnp.maximum(g, GLA_DIAG_DECAY_FLOOR))
                         for (u, d), g in zip(items, g_tiles)]
            products = []
            for (u, d), (b, _), bd in zip(items, cums, diag_cums):
                flat = b.reshape(tile, dk)
                split = []
                for h, mask in d.split_masks[1:]:
                    e = jnp.exp2(-jnp.abs(flat - _block_row(flat, 2 * h, h - 1 if d.fwd else h)))
                    split.append((_nt_dot((q2[u] * e).astype(BF16), (k2[u] * e).astype(BF16)), mask))
                mid = _block_row(bd, diag, diag // 2 - 1 if d.fwd else diag // 2)
                inner = _nt_dot((q2[u] * jnp.exp2(bd - mid)).astype(BF16),
                                (k2[u] * jnp.exp2(mid - bd)).astype(BF16))
                products.append((split, inner))
            per_item = []
            for (_, d), (split, inner), outer in zip(items, products, pair_terms):
                part = jnp.where(d.diag_mask, inner, 0.0).astype(BF16) + outer.astype(BF16) * d.split_masks[0][1]
                for product, mask in split:
                    part = part + product.astype(BF16) * mask
                per_item.append(part)
            store_scores(per_item)

        lowest = jnp.min(functools.reduce(jnp.minimum, g_tiles))
        pl.when(lowest >= GLA_MILD_DECAY_FLOOR)(chunk_scores_mild)
        pl.when(lowest < GLA_MILD_DECAY_FLOOR)(chunk_scores_any)
        return carry

    lax.fori_loop(0, t // (tile * GLA_LOCAL_UNROLL), local_body, 0)

    def scan_body(i, carry, finish):
        steps = []
        for u in range(GLA_SCAN_UNROLL):
            for d in dirs:
                step = i * GLA_SCAN_UNROLL + u
                j = step if d.fwd else n_pairs - 1 - step
                steps.append((d, j, pl.ds(pl.multiple_of(j * pair, pair), pair)))
        increments = [_dot(d.kt_scr[j], v_ref[sl, :]) for d, j, sl in steps]
        local = [None if finish else _dot(a_scr[sl, :], v_ref[sl, :]) for _, _, sl in steps]
        outs = []
        for (d, j, sl), inc in zip(steps, increments):
            qb_scr, dm_scr, s_scr = d.qb_scr, d.dm_scr, d.s_scr
            s = s_scr[...]
            outs.append(_dot(qb_scr[sl, :], s.astype(BF16)))
            s_scr[...] = s * jnp.concatenate([dm_scr[j]] * (dv // dk), axis=1) + inc
        for (_, _, sl), o, loc in zip(steps, outs, local):
            if finish:
                y = _rmsnorm(o + o_scr[sl, :], gn_ref[...])
                out_ref[sl, :] = (y * sr_ref[sl, :].astype(F32)).astype(BF16)
            else:
                o_scr[sl, :] = o + loc
        return carry

    half = n_pairs // (2 * GLA_SCAN_UNROLL)
    lax.fori_loop(0, half, functools.partial(scan_body, finish=False), 0)
    lax.fori_loop(half, 2 * half, functools.partial(scan_body, finish=True), 0)


def _gla_call(q, k, v, sr, gf, gb, kc, vc, gfc, gbc, gn):
    bsz, t, qk = q.shape
    d = v.shape[2]
    tc = kc.shape[1]
    dk = qk // GLA_HEADS
    dv = d // GLA_HEADS
    pair = GLA_KERNEL_CHUNK * GLA_PAIR
    n_pairs = t // pair
    tile = GLA_KERNEL_CHUNK * GLA_GROUP
    seq = lambda rows, width: pl.BlockSpec((None, rows, width), lambda b, h: (b, 0, h))
    scratch_shapes = [
        pltpu.VMEM((t, pair), BF16),
        pltpu.VMEM((t, dv), F32),
        pltpu.VMEM((n_pairs, dk, pair), BF16), pltpu.VMEM((n_pairs, dk, pair), BF16),
        pltpu.VMEM((t, dk), BF16), pltpu.VMEM((t, dk), BF16),
        pltpu.VMEM((n_pairs, dk, dk), F32), pltpu.VMEM((n_pairs, dk, dk), F32),
        pltpu.VMEM((dk, dv), F32), pltpu.VMEM((dk, dv), F32),
    ]
    pipelined = (2 * _nbytes((t, dk), BF16) + 3 * _nbytes((t, dv), BF16) + 2 * _nbytes((t, dk), F32)
                 + _nbytes((tc, dk), BF16) + _nbytes((tc, dv), BF16) + 2 * _nbytes((tc, dk), F32)
                 + _nbytes((1, dv), F32))
    scratch = (_nbytes((t, pair), BF16) + _nbytes((t, dv), F32) + 2 * _nbytes((n_pairs, dk, pair), BF16)
               + 2 * _nbytes((t, dk), BF16) + 2 * _nbytes((n_pairs, dk, dk), F32) + 2 * _nbytes((dk, dv), F32))
    temps = 2 * GLA_LOCAL_UNROLL * (8 * _nbytes((tile, dk), F32) + 2 * _nbytes((tile, tile), F32))
    return pl.pallas_call(
        _gla_kernel,
        out_shape=jax.ShapeDtypeStruct((bsz, t, d), BF16),
        grid=(bsz, GLA_HEADS),
        in_specs=[seq(t, dk), seq(t, dk), seq(t, dv), seq(t, dv), seq(t, dk), seq(t, dk),
                  seq(tc, dk), seq(tc, dv), seq(tc, dk), seq(tc, dk),
                  pl.BlockSpec((1, dv), lambda b, h: (0, h))],
        out_specs=seq(t, dv),
        scratch_shapes=scratch_shapes,
        compiler_params=pltpu.CompilerParams(
            dimension_semantics=("parallel", "parallel"),
            vmem_limit_bytes=_vmem_limit(pipelined, scratch, temps),
        ),
        name="gla",
    )(q, k, v, sr, gf, gb, kc, vc, gfc, gbc, gn)


def _out_kernel(x_ref, h1_ref, ya_ref, yb_ref, ga1_ref, sh2_ref, sc2_ref, ga2_ref, g2_ref, gfin_ref,
                wg_ref, bg_ref, wco_ref, wgo_ref, wo_ref, wup_ref, wdn_ref, o_ref, y_scr, h2_scr):
    tm, d = x_ref.shape
    d_ff = wup_ref.shape[1]

    def merge_stage(j):
        cs = slice(j, j + COL_BLOCK)
        h1 = h1_ref[...]
        za = _dot(h1, wg_ref[:, j:j + COL_BLOCK]) + bg_ref[:, j:j + COL_BLOCK]
        zb = _dot(h1, wg_ref[:, d + j:d + j + COL_BLOCK]) + bg_ref[:, d + j:d + j + COL_BLOCK]
        pa = _dot(ya_ref[...], wco_ref[:, cs])
        pb = _dot(yb_ref[...], wgo_ref[:, cs])

        def epilogue():
            y_scr[:, cs] = (_sigmoid(za) * pa + _sigmoid(zb) * pb).astype(BF16)
        return epilogue

    _run_skewed([functools.partial(merge_stage, j) for j in range(0, d, COL_BLOCK)])
    o_ref[...] = x_ref[...] + ga1_ref[...] * _dot(y_scr[...], wo_ref[...])
    h2_scr[...] = _modulate(o_ref[...], g2_ref[...], sh2_ref[...], sc2_ref[...]).astype(BF16)

    parts = []

    def mlp_stage(j):
        u = _dot(h2_scr[...], wup_ref[:, j:j + FF_BLOCK])

        def epilogue():
            a = jnp.maximum(u, 0.0)
            parts.append(_dot((a * a).astype(BF16), wdn_ref[j:j + FF_BLOCK, :]))
        return epilogue

    _run_skewed([functools.partial(mlp_stage, j) for j in range(0, d_ff, FF_BLOCK)])
    x2 = o_ref[...] + ga2_ref[...] * functools.reduce(lambda a, b: a + b, parts)
    o_ref[...] = _rmsnorm(x2, gfin_ref[...])


def _out_call(x, h1, ya, yb, mod3, g2, gfin, wg, bg, wco, wgo, wo, wup, wdn):
    bsz, t, d = x.shape
    tm = OUT_TM
    tok = lambda: pl.BlockSpec((None, tm, d), lambda b, i: (b, i, 0))
    modrow = lambda col: pl.BlockSpec((None, 1, d), lambda b, i: (b, 0, col))
    pipelined = 2 * _nbytes((tm, d), F32) + 3 * _nbytes((tm, d), BF16) + 4 * _nbytes((1, d), F32)
    resident = sum(_nbytes(a.shape, a.dtype) for a in (g2, gfin, wg, bg, wco, wgo, wo, wup, wdn))
    scratch = 2 * _nbytes((tm, d), BF16)
    temps = 10 * _nbytes((tm, d), F32)
    return pl.pallas_call(
        _out_kernel,
        out_shape=jax.ShapeDtypeStruct((bsz, t, d), F32),
        grid=(bsz, t // tm),
        in_specs=[tok(), tok(), tok(), tok()] + [modrow(cidx) for cidx in (2, 3, 4, 5)] + [_resident()] * 9,
        out_specs=tok(),
        scratch_shapes=[pltpu.VMEM((tm, d), BF16), pltpu.VMEM((tm, d), BF16)],
        compiler_params=pltpu.CompilerParams(
            dimension_semantics=("parallel", "parallel"),
            vmem_limit_bytes=_vmem_limit(pipelined, resident + scratch, temps),
        ),
        name="out",
    )(x, h1, ya, yb, mod3, mod3, mod3, mod3, g2, gfin, wg, bg, wco, wgo, wo, wup, wdn)


def kernel(x, c, ctx, c_ctx, w_ada, b_ada, g_norm1, w_in, b_in, conv_w, w_conv_out, w_a2_f, b_a_f,
           w_a2_b, b_a_b, g_gla_norm, w_gla_out, w_o, g_norm2, w_up, w_down, g_final):
    depth = w_ada.shape[0]
    assert depth == 1, "only the single-layer block is implemented"
    bsz, t, d = x.shape
    qk = w_a2_f.shape[2]
    dk = qk // GLA_HEADS
    rank = w_a2_f.shape[1]
    gla_tile = GLA_GROUP * GLA_KERNEL_CHUNK
    assert t % INPROJ_TM == 0 and t % OUT_TM == 0 and INPROJ_TM % GRID_W == 0
    assert GLA_PAIR == 2 and GLA_GROUP % GLA_PAIR == 0 and GLA_PAIR * GLA_KERNEL_CHUNK == dk
    gla_pair = GLA_PAIR * GLA_KERNEL_CHUNK
    assert t % (GLA_LOCAL_UNROLL * gla_tile) == 0 and ctx.shape[1] % gla_tile == 0
    assert t % (2 * GLA_SCAN_UNROLL * gla_pair) == 0
    assert bsz + 1 <= MOD_ROWS and 2 * rank <= LR_PAD and 5 * d + 2 * qk + LR_PAD <= w_in.shape[2]

    cc = jnp.zeros((MOD_ROWS, d), F32).at[:bsz].set(c).at[bsz].set(c_ctx)
    mod = _ada_call(cc, w_ada[0], b_ada[0][None, :])
    mod3 = mod.reshape(MOD_ROWS, 1, N_MOD * d)

    o_lr = 5 * d + 2 * qk
    o_g = o_lr + 2 * rank
    w_main = w_in[0][:, :o_lr + LR_PAD].astype(BF16)
    b_main = b_in[0][None, :o_lr + LR_PAD]
    wg = w_in[0][:, o_g:].astype(BF16)
    bg = b_in[0][None, o_g:]
    wa2 = jnp.zeros((LR_PAD, 2 * qk), F32)
    wa2 = wa2.at[:rank, :qk].set(w_a2_f[0]).at[rank:2 * rank, qk:].set(w_a2_b[0]).astype(BF16)
    ba2 = jnp.concatenate([b_a_f[0], b_a_b[0]])[None, :]
    g1 = g_norm1[0][None, :]

    h1, ya, q, k, v, sr, gf, gb = _inproj_call(x, mod3, g1, w_main, b_main, conv_w[0], wa2, ba2, qk, dk)
    kc, vc, gfc, gbc = _inproj_ctx_call(ctx, mod3, bsz, g1, w_main, b_main, wa2, ba2, qk)
    yb = _gla_call(q, k, v, sr, gf, gb, kc, vc, gfc, gbc, g_gla_norm[0][None, :])
    return _out_call(x, h1, ya, yb, mod3, g_norm2[0][None, :], g_final[None, :], wg, bg,
                     w_conv_out[0].astype(BF16), w_gla_out[0].astype(BF16), w_o[0].astype(BF16),
                     w_up[0].astype(BF16), w_down[0].astype(BF16))
```

```python
import functools
from typing import NamedTuple

import jax
import jax.numpy as jnp
from jax import lax
from jax.experimental import pallas as pl
from jax.experimental.pallas import tpu as pltpu

F32 = jnp.float32
BF16 = jnp.bfloat16

GLA_HEADS = 4
GLA_TAU = 16.0
GRID_W = 64
N_MOD = 6
RMS_EPS = 1e-6
LOG2_E = 1.4426950408889634

V7X_LANES = 128
V7X_VMEM_LIMIT_BYTES = 60000 * 1024

LR_PAD = V7X_LANES
GLA_KERNEL_CHUNK = 64
GLA_DIAG_BLOCK = 8
GLA_MAX_EXPONENT = 96.0
GLA_DIAG_DECAY_FLOOR = -GLA_MAX_EXPONENT / (GLA_DIAG_BLOCK // 2)
GLA_PAIR = 2
GLA_MILD_DECAY_FLOOR = -GLA_MAX_EXPONENT / (GLA_PAIR * GLA_KERNEL_CHUNK // 2)
GLA_GROUP = 4
GLA_LOCAL_UNROLL = 4
GLA_SCAN_UNROLL = 16
INPROJ_TM = 1024
OUT_TM = 512
COL_BLOCK = 256
FF_BLOCK = 1024
MOD_ROWS = 16


def _vmem_limit(pipelined_bytes, resident_bytes, temp_bytes):
    need = 2 * pipelined_bytes + resident_bytes + temp_bytes
    return int(min(V7X_VMEM_LIMIT_BYTES, need))


def _nbytes(shape, dtype):
    n = 1
    for s in shape:
        n *= s
    return n * jnp.dtype(dtype).itemsize


def _dot(a, b):
    return jnp.dot(a, b, preferred_element_type=F32)


def _nt_dot(a, b):
    return lax.dot_general(a, b, (((1,), (1,)), ((), ())), preferred_element_type=F32)


def _run_skewed(stages):
    pending = None
    for stage in stages:
        epilogue = stage()
        if pending is not None:
            pending()
        pending = epilogue
    if pending is not None:
        pending()


def _sigmoid(x):
    return 1.0 / (1.0 + jnp.exp(-x))


def _log_sigmoid(x):
    return jnp.minimum(x, 0.0) - jnp.log(1.0 + jnp.exp(-jnp.abs(x)))


def _rmsnorm(x, g):
    return x * lax.rsqrt(jnp.mean(x * x, axis=-1, keepdims=True) + RMS_EPS) * g


def _modulate(x, g, shift, scale):
    return _rmsnorm(x, g) * (1.0 + scale) + shift


def _ada_kernel(c_ref, w_ref, b_ref, o_ref):
    c = c_ref[...]
    s = c * _sigmoid(c)
    o_ref[...] = _dot(s.astype(BF16), w_ref[...].astype(BF16)) + b_ref[...]


def _ada_call(cc, w_ada, b_ada):
    d = cc.shape[1]
    n_out = w_ada.shape[1]
    return pl.pallas_call(
        _ada_kernel,
        out_shape=jax.ShapeDtypeStruct((MOD_ROWS, n_out), F32),
        grid=(n_out // d,),
        in_specs=[
            pl.BlockSpec((MOD_ROWS, d), lambda j: (0, 0)),
            pl.BlockSpec((d, d), lambda j: (0, j)),
            pl.BlockSpec((1, d), lambda j: (0, j)),
        ],
        out_specs=pl.BlockSpec((MOD_ROWS, d), lambda j: (0, j)),
        compiler_params=pltpu.CompilerParams(
            dimension_semantics=("arbitrary",),
            vmem_limit_bytes=_vmem_limit(_nbytes((d, d), F32) + _nbytes((MOD_ROWS, 2 * d), F32),
                                         0, 4 * _nbytes((d, d), F32)),
        ),
        name="ada",
    )(cc, w_ada, b_ada)


def _log_decays(lr, wa2_ref, ba2_ref):
    xg = _dot(lr.astype(BF16), wa2_ref[...]) + ba2_ref[...]
    return _log_sigmoid(xg) * (LOG2_E / GLA_TAU)


def _inproj_kernel(x_ref, sh_ref, sc_ref, g1_ref, w_ref, b_ref, cw_ref, wa2_ref, ba2_ref,
                   h_ref, ya_ref, q_ref, k_ref, v_ref, sr_ref, gf_ref, gb_ref, *, q_scale):
    tm, d = x_ref.shape
    qk = q_ref.shape[1]
    h_ref[...] = _modulate(x_ref[...], g1_ref[...], sh_ref[...], sc_ref[...]).astype(BF16)

    col_in_row = lax.broadcasted_iota(jnp.int32, (tm, 1), 0) % GRID_W
    has_left = col_in_row != 0
    has_right = col_in_row != GRID_W - 1

    def proj(c0, width):
        return _dot(h_ref[...], w_ref[:, c0:c0 + width]) + b_ref[:, c0:c0 + width]

    o_q = 3 * d
    o_v = o_q + 2 * qk
    o_r = o_v + d
    o_lr = o_r + d

    def decay_stage():
        lr = proj(o_lr, LR_PAD)

        def epilogue():
            g = _log_decays(lr, wa2_ref, ba2_ref)
            gf_ref[...] = g[:, 0:qk]
            gb_ref[...] = g[:, qk:2 * qk]
        return epilogue

    def conv_stage(j):
        cs = slice(j, j + COL_BLOCK)
        xa = proj(j, COL_BLOCK)
        ba = proj(d + j, COL_BLOCK)
        ca = proj(2 * d + j, COL_BLOCK)

        def epilogue():
            u = ca * xa
            left = jnp.where(has_left, pltpu.roll(u, 1, 0), 0.0)
            right = jnp.where(has_right, pltpu.roll(u, tm - 1, 0), 0.0)
            y = left * cw_ref[0:1, cs] + u * cw_ref[1:2, cs] + right * cw_ref[2:3, cs]
            ya_ref[:, cs] = (ba * y).astype(BF16)
        return epilogue

    def swish_stage(j):
        r = proj(o_r + j, COL_BLOCK)

        def epilogue():
            sr_ref[:, j:j + COL_BLOCK] = (r * _sigmoid(r)).astype(BF16)
        return epilogue

    def qk_stage():
        q = proj(o_q, qk)
        k = proj(o_q + qk, qk)

        def epilogue():
            q_ref[...] = (q * q_scale).astype(BF16)
            k_ref[...] = k.astype(BF16)
        return epilogue

    def value_stage(j):
        v = proj(o_v + j, COL_BLOCK)

        def epilogue():
            v_ref[:, j:j + COL_BLOCK] = v.astype(BF16)
        return epilogue

    _run_skewed([decay_stage] + [functools.partial(conv_stage, j) for j in range(0, d, COL_BLOCK)]
                + [functools.partial(swish_stage, j) for j in range(0, d, COL_BLOCK)] + [qk_stage]
                + [functools.partial(value_stage, j) for j in range(0, d, COL_BLOCK)])


def _inproj_ctx_kernel(x_ref, sh_ref, sc_ref, g1_ref, w_ref, b_ref, wa2_ref, ba2_ref,
                       k_ref, v_ref, gf_ref, gb_ref):
    d = x_ref.shape[1]
    qk = k_ref.shape[1]
    h = _modulate(x_ref[...], g1_ref[...], sh_ref[...], sc_ref[...]).astype(BF16)
    o_k = 3 * d + qk
    k_ref[...] = (_dot(h, w_ref[:, o_k:o_k + qk]) + b_ref[:, o_k:o_k + qk]).astype(BF16)
    for j in range(0, d, COL_BLOCK):
        c0 = o_k + qk + j
        v_ref[:, j:j + COL_BLOCK] = (_dot(h, w_ref[:, c0:c0 + COL_BLOCK]) + b_ref[:, c0:c0 + COL_BLOCK]).astype(BF16)
    c0 = o_k + qk + 2 * d
    lr = _dot(h, w_ref[:, c0:c0 + LR_PAD]) + b_ref[:, c0:c0 + LR_PAD]
    g = _log_decays(lr, wa2_ref, ba2_ref)
    gf_ref[...] = g[:, 0:qk]
    gb_ref[...] = g[:, qk:2 * qk]


def _resident():
    return pl.BlockSpec(memory_space=pltpu.VMEM)


def _inproj_call(x, mod3, g1, w, b, cw, wa2, ba2, qk, dk):
    bsz, t, d = x.shape
    tm = INPROJ_TM
    tok = lambda n: pl.BlockSpec((None, tm, n), lambda b, i: (b, i, 0))
    modrow = lambda col: pl.BlockSpec((None, 1, d), lambda b, i: (b, 0, col))
    out_shapes = (
        jax.ShapeDtypeStruct((bsz, t, d), BF16),
        jax.ShapeDtypeStruct((bsz, t, d), BF16),
        jax.ShapeDtypeStruct((bsz, t, qk), BF16),
        jax.ShapeDtypeStruct((bsz, t, qk), BF16),
        jax.ShapeDtypeStruct((bsz, t, d), BF16),
        jax.ShapeDtypeStruct((bsz, t, d), BF16),
        jax.ShapeDtypeStruct((bsz, t, qk), F32),
        jax.ShapeDtypeStruct((bsz, t, qk), F32),
    )
    pipelined = (_nbytes((tm, d), F32) + 4 * _nbytes((tm, d), BF16) + 2 * _nbytes((tm, qk), BF16)
                 + 2 * _nbytes((tm, qk), F32) + 2 * _nbytes((1, d), F32))
    resident = sum(_nbytes(a.shape, a.dtype) for a in (g1, w, b, cw, wa2, ba2))
    temps = 24 * _nbytes((tm, COL_BLOCK), F32) + 3 * _nbytes((tm, 2 * qk), F32)
    return pl.pallas_call(
        functools.partial(_inproj_kernel, q_scale=dk ** -0.5),
        out_shape=out_shapes,
        grid=(bsz, t // tm),
        in_specs=[tok(d), modrow(0), modrow(1)] + [_resident()] * 6,
        out_specs=(tok(d), tok(d), tok(qk), tok(qk), tok(d), tok(d), tok(qk), tok(qk)),
        compiler_params=pltpu.CompilerParams(
            dimension_semantics=("parallel", "parallel"),
            vmem_limit_bytes=_vmem_limit(pipelined, resident, temps),
        ),
        name="inproj",
    )(x, mod3, mod3, g1, w, b, cw, wa2, ba2)


def _inproj_ctx_call(ctx, mod3, ctx_row, g1, w, b, wa2, ba2, qk):
    bsz, tc, d = ctx.shape
    tok = lambda n: pl.BlockSpec((None, tc, n), lambda b: (b, 0, 0))
    modrow = lambda col: pl.BlockSpec((None, 1, d), lambda b: (ctx_row, 0, col))
    out_shapes = (
        jax.ShapeDtypeStruct((bsz, tc, qk), BF16),
        jax.ShapeDtypeStruct((bsz, tc, d), BF16),
        jax.ShapeDtypeStruct((bsz, tc, qk), F32),
        jax.ShapeDtypeStruct((bsz, tc, qk), F32),
    )
    pipelined = (_nbytes((tc, d), F32) + _nbytes((tc, d), BF16) + _nbytes((tc, qk), BF16)
                 + 2 * _nbytes((tc, qk), F32) + 2 * _nbytes((1, d), F32))
    resident = sum(_nbytes(a.shape, a.dtype) for a in (g1, w, b, wa2, ba2))
    temps = _nbytes((tc, d), BF16) + 6 * _nbytes((tc, COL_BLOCK), F32) + 3 * _nbytes((tc, 2 * qk), F32)
    return pl.pallas_call(
        _inproj_ctx_kernel,
        out_shape=out_shapes,
        grid=(bsz,),
        in_specs=[tok(d), modrow(0), modrow(1)] + [_resident()] * 5,
        out_specs=(tok(qk), tok(d), tok(qk), tok(qk)),
        compiler_params=pltpu.CompilerParams(
            dimension_semantics=("parallel",),
            vmem_limit_bytes=_vmem_limit(pipelined, resident, temps),
        ),
        name="inproj_ctx",
    )(ctx, mod3, mod3, g1, w, b, wa2, ba2)


class _ScanDir(NamedTuple):
    fwd: bool
    g_ref: object
    gc_ref: object
    chunk_sum: object
    diag_sum: object
    split_masks: tuple
    pair_mask: object
    diag_mask: object
    kt_scr: object
    qb_scr: object
    dm_scr: object
    s_scr: object


def _block_sums(sum_matrix, g):
    dk = g.shape[1]
    g_hi = g.astype(BF16)
    g_lo = (g - g_hi.astype(F32)).astype(BF16)
    bb = _dot(sum_matrix, jnp.concatenate([g_hi, g_lo], axis=1))
    return bb[:, :dk] + bb[:, dk:]


def _tile_decays(g, sum_matrix, fwd, n_chunks):
    rows, dk = g.shape
    c = rows // n_chunks
    b = _block_sums(sum_matrix, g).reshape(n_chunks, c, dk)
    return b, (b[:, c - 1:c, :] if fwd else b[:, 0:1, :])


def _block_row(x, block, row):
    rows, n = x.shape
    x3 = x.reshape(rows // block, block, n)
    return jnp.broadcast_to(x3[:, row:row + 1, :], x3.shape).reshape(rows, n)


def _pair_offsets(tot, fwd):
    n_chunks = tot.shape[0]
    zero = jnp.zeros_like(tot[0:1])
    on_odd = jnp.concatenate([t for p in range(0, n_chunks, GLA_PAIR) for t in (zero, tot[p:p + 1])], axis=0)
    on_even = jnp.concatenate([t for p in range(0, n_chunks, GLA_PAIR) for t in (tot[p + 1:p + 2], zero)], axis=0)
    pair_tot = jnp.concatenate([tot[p:p + 1] + tot[p + 1:p + 2] for p in range(0, n_chunks, GLA_PAIR)], axis=0)
    return (on_odd, on_even, pair_tot) if fwd else (on_even, on_odd, pair_tot)


def _lane_broadcast_column(row):
    n = row.shape[1]
    return jnp.broadcast_to(row, (n, n)).T


def _gla_kernel(q_ref, k_ref, v_ref, sr_ref, gf_ref, gb_ref, kc_ref, vc_ref, gfc_ref, gbc_ref, gn_ref,
                out_ref, a_scr, o_scr, ktf_scr, ktb_scr, qbf_scr, qbb_scr, dmf_scr, dmb_scr, sf_scr, sb_scr):
    c = GLA_KERNEL_CHUNK
    grp = GLA_GROUP
    tile = c * grp
    pair = c * GLA_PAIR
    pairs_per_tile = grp // GLA_PAIR
    diag = GLA_DIAG_BLOCK
    t, dk = q_ref.shape
    dv = v_ref.shape[1]
    n_pairs = t // pair
    row = lax.broadcasted_iota(jnp.int32, (tile, tile), 0)
    col = lax.broadcasted_iota(jnp.int32, (tile, tile), 1)
    halves = []
    half = pair // 2
    while half >= diag:
        halves.append(half)
        half //= 2

    def same_block(size):
        return (row // size) == (col // size)

    def scan_dir(fwd, g_ref, gc_ref, kt_scr, qb_scr, dm_scr, s_scr):
        seen = (row >= col) if fwd else (row <= col)
        split_masks = []
        for h in halves:
            later, earlier = (row % (2 * h) >= h), (col % (2 * h) < h)
            if not fwd:
                later, earlier = (row % (2 * h) < h), (col % (2 * h) >= h)
            split_masks.append((h, (same_block(2 * h) & later & earlier).astype(BF16)))
        return _ScanDir(fwd, g_ref, gc_ref, (same_block(c) & seen).astype(BF16),
                        (same_block(diag) & seen).astype(BF16), tuple(split_masks),
                        same_block(pair) & seen, same_block(diag) & seen, kt_scr, qb_scr, dm_scr, s_scr)

    dirs = (scan_dir(True, gf_ref, gfc_ref, ktf_scr, qbf_scr, dmf_scr, sf_scr),
            scan_dir(False, gb_ref, gbc_ref, ktb_scr, qbb_scr, dmb_scr, sb_scr))

    def key_side(k3, b, tot, k_off):
        kt = k3 * jnp.exp2(tot - b)
        return kt, kt * jnp.exp2(k_off)

    ctx_tiles = kc_ref.shape[0] // tile
    ctx_items = [(d, ti) for d in dirs for ti in range(ctx_tiles)]
    ctx_cums = [_tile_decays(d.gc_ref[ti * tile:(ti + 1) * tile, :], d.chunk_sum, d.fwd, grp)
                for d, ti in ctx_items]
    ctx_terms = {}
    for (d, ti), (b, tot) in zip(ctx_items, ctx_cums):
        _, k_off, pair_tot = _pair_offsets(tot, d.fwd)
        kc3 = kc_ref[ti * tile:(ti + 1) * tile, :].astype(F32).reshape(grp, c, dk)
        _, kt_pair = key_side(kc3, b, tot, k_off)
        kt_t = kt_pair.reshape(tile, dk).T.astype(BF16)
        decay = jnp.exp2(pair_tot)
        for p in range(pairs_per_tile):
            r0 = ti * tile + p * pair
            ctx_terms[(d.fwd, ti * pairs_per_tile + p)] = (
                _dot(kt_t[:, p * pair:(p + 1) * pair], vc_ref[r0:r0 + pair, :]),
                _lane_broadcast_column(decay[p]))
    for d in dirs:
        order = range(ctx_tiles * pairs_per_tile)
        s = jnp.zeros((dk, dv), F32)
        for p in (order if d.fwd else reversed(order)):
            inc, dm = ctx_terms[(d.fwd, p)]
            s = s * jnp.concatenate([dm] * (dv // dk), axis=1) + inc
        d.s_scr[...] = s

    def local_body(i, carry):
        tiles = [i * GLA_LOCAL_UNROLL + u for u in range(GLA_LOCAL_UNROLL)]
        rows = [pl.ds(pl.multiple_of(ti * tile, tile), tile) for ti in tiles]
        items = [(u, d) for u in range(GLA_LOCAL_UNROLL) for d in dirs]
        g_tiles = [d.g_ref[rows[u], :] for u, d in items]
        cums = [_tile_decays(g, d.chunk_sum, d.fwd, grp) for (u, d), g in zip(items, g_tiles)]
        q2 = [q_ref[sl, :].astype(F32) for sl in rows]
        k2 = [k_ref[sl, :].astype(F32) for sl in rows]
        chunk_terms = []
        for (u, d), (b, tot) in zip(items, cums):
            q_off, k_off, pair_tot = _pair_offsets(tot, d.fwd)
            q3, k3 = q2[u].reshape(grp, c, dk), k2[u].reshape(grp, c, dk)
            qb = q3 * jnp.exp2(b)
            kt, kt_pair = key_side(k3, b, tot, k_off)
            chunk_terms.append((qb, kt, q_off))
            d.qb_scr[rows[u], :] = (qb * jnp.exp2(q_off)).reshape(tile, dk).astype(BF16)
            kt_t = kt_pair.reshape(tile, dk).T.astype(BF16)
            decay = jnp.exp2(pair_tot)
            for p in range(pairs_per_tile):
                d.kt_scr[tiles[u] * pairs_per_tile + p] = kt_t[:, p * pair:(p + 1) * pair]
                d.dm_scr[tiles[u] * pairs_per_tile + p] = _lane_broadcast_column(decay[p])

        def store_scores(per_item):
            for u in range(GLA_LOCAL_UNROLL):
                scores = functools.reduce(lambda x, y: x + y, [s for (iu, _), s in zip(items, per_item) if iu == u])
                for p in range(pairs_per_tile):
                    a_scr[pl.ds(pl.multiple_of(tiles[u] * tile + p * pair, pair), pair), :] = (
                        scores[p * pair:(p + 1) * pair, p * pair:(p + 1) * pair])

        def pair_scores_mild():
            products = []
            for (u, d), (b, _), (_, _, q_off) in zip(items, cums, chunk_terms):
                flat = (b + q_off).reshape(tile, dk)
                mid = _block_row(flat, pair, pair // 2 - 1 if d.fwd else pair // 2)
                products.append(_nt_dot((q2[u] * jnp.exp2(flat - mid)).astype(BF16),
                                        (k2[u] * jnp.exp2(mid - flat)).astype(BF16)))
            store_scores([jnp.where(d.pair_mask, inner, 0.0).astype(BF16)
                          for (_, d), inner in zip(items, products)])

        def chunk_scores_any():
            pair_terms = [_nt_dot(qb.reshape(tile, dk).astype(BF16), kt.reshape(tile, dk).astype(BF16))
                          for qb, kt, _ in chunk_terms]
            diag_cums = [_block_sums(d.diag_sum, jnp.maximum(g, GLA_DIAG_DECAY_FLOOR))
                         for (u, d), g in zip(items, g_tiles)]
            products = []
            for (u, d), (b, _), bd in zip(items, cums, diag_cums):
                flat = b.reshape(tile, dk)
                split = []
                for h, mask in d.split_masks[1:]:
                    e = jnp.exp2(-jnp.abs(flat - _block_row(flat, 2 * h, h - 1 if d.fwd else h)))
                    split.append((_nt_dot((q2[u] * e).astype(BF16), (k2[u] * e).astype(BF16)), mask))
                mid = _block_row(bd, diag, diag // 2 - 1 if d.fwd else diag // 2)
                inner = _nt_dot((q2[u] * jnp.exp2(bd - mid)).astype(BF16),
                                (k2[u] * jnp.exp2(mid - bd)).astype(BF16))
                products.append((split, inner))
            per_item = []
            for (_, d), (split, inner), outer in zip(items, products, pair_terms):
                part = jnp.where(d.diag_mask, inner, 0.0).astype(BF16) + outer.astype(BF16) * d.split_masks[0][1]
                for product, mask in split:
                    part = part + product.astype(BF16) * mask
                per_item.append(part)
            store_scores(per_item)

        lowest = jnp.min(functools.reduce(jnp.minimum, g_tiles))
        pl.when(lowest >= GLA_MILD_DECAY_FLOOR)(pair_scores_mild)
        pl.when(lowest < GLA_MILD_DECAY_FLOOR)(chunk_scores_any)
        return carry

    lax.fori_loop(0, t // (tile * GLA_LOCAL_UNROLL), local_body, 0)

    def scan_body(i, carry, finish):
        steps = []
        for u in range(GLA_SCAN_UNROLL):
            for d in dirs:
                step = i * GLA_SCAN_UNROLL + u
                j = step if d.fwd else n_pairs - 1 - step
                steps.append((d, j, pl.ds(pl.multiple_of(j * pair, pair), pair)))
        increments = [_dot(d.kt_scr[j], v_ref[sl, :]) for d, j, sl in steps]
        local = [None if finish else _dot(a_scr[sl, :], v_ref[sl, :]) for _, _, sl in steps]
        outs = []
        for (d, j, sl), inc in zip(steps, increments):
            qb_scr, dm_scr, s_scr = d.qb_scr, d.dm_scr, d.s_scr
            s = s_scr[...]
            outs.append(_dot(qb_scr[sl, :], s.astype(BF16)))
            s_scr[...] = s * jnp.concatenate([dm_scr[j]] * (dv // dk), axis=1) + inc
        for (_, _, sl), o, loc in zip(steps, outs, local):
            if finish:
                y = _rmsnorm(o + o_scr[sl, :], gn_ref[...])
                out_ref[sl, :] = (y * sr_ref[sl, :].astype(F32)).astype(BF16)
            else:
                o_scr[sl, :] = o + loc
        return carry

    half = n_pairs // (2 * GLA_SCAN_UNROLL)
    lax.fori_loop(0, half, functools.partial(scan_body, finish=False), 0)
    lax.fori_loop(half, 2 * half, functools.partial(scan_body, finish=True), 0)


def _gla_call(q, k, v, sr, gf, gb, kc, vc, gfc, gbc, gn):
    bsz, t, qk = q.shape
    d = v.shape[2]
    tc = kc.shape[1]
    dk = qk // GLA_HEADS
    dv = d // GLA_HEADS
    pair = GLA_KERNEL_CHUNK * GLA_PAIR
    n_pairs = t // pair
    tile = GLA_KERNEL_CHUNK * GLA_GROUP
    seq = lambda rows, width: pl.BlockSpec((None, rows, width), lambda b, h: (b, 0, h))
    scratch_shapes = [
        pltpu.VMEM((t, pair), BF16),
        pltpu.VMEM((t, dv), F32),
        pltpu.VMEM((n_pairs, dk, pair), BF16), pltpu.VMEM((n_pairs, dk, pair), BF16),
        pltpu.VMEM((t, dk), BF16), pltpu.VMEM((t, dk), BF16),
        pltpu.VMEM((n_pairs, dk, dk), F32), pltpu.VMEM((n_pairs, dk, dk), F32),
        pltpu.VMEM((dk, dv), F32), pltpu.VMEM((dk, dv), F32),
    ]
    pipelined = (2 * _nbytes((t, dk), BF16) + 3 * _nbytes((t, dv), BF16) + 2 * _nbytes((t, dk), F32)
                 + _nbytes((tc, dk), BF16) + _nbytes((tc, dv), BF16) + 2 * _nbytes((tc, dk), F32)
                 + _nbytes((1, dv), F32))
    scratch = (_nbytes((t, pair), BF16) + _nbytes((t, dv), F32) + 2 * _nbytes((n_pairs, dk, pair), BF16)
               + 2 * _nbytes((t, dk), BF16) + 2 * _nbytes((n_pairs, dk, dk), F32) + 2 * _nbytes((dk, dv), F32))
    temps = 2 * GLA_LOCAL_UNROLL * (8 * _nbytes((tile, dk), F32) + 2 * _nbytes((tile, tile), F32))
    return pl.pallas_call(
        _gla_kernel,
        out_shape=jax.ShapeDtypeStruct((bsz, t, d), BF16),
        grid=(bsz, GLA_HEADS),
        in_specs=[seq(t, dk), seq(t, dk), seq(t, dv), seq(t, dv), seq(t, dk), seq(t, dk),
                  seq(tc, dk), seq(tc, dv), seq(tc, dk), seq(tc, dk),
                  pl.BlockSpec((1, dv), lambda b, h: (0, h))],
        out_specs=seq(t, dv),
        scratch_shapes=scratch_shapes,
        compiler_params=pltpu.CompilerParams(
            dimension_semantics=("parallel", "parallel"),
            vmem_limit_bytes=_vmem_limit(pipelined, scratch, temps),
        ),
        name="gla",
    )(q, k, v, sr, gf, gb, kc, vc, gfc, gbc, gn)


def _out_kernel(x_ref, h1_ref, ya_ref, yb_ref, ga1_ref, sh2_ref, sc2_ref, ga2_ref, g2_ref, gfin_ref,
                wg_ref, bg_ref, wco_ref, wgo_ref, wo_ref, wup_ref, wdn_ref, o_ref, y_scr, h2_scr):
    tm, d = x_ref.shape
    d_ff = wup_ref.shape[1]

    def merge_stage(j):
        cs = slice(j, j + COL_BLOCK)
        h1 = h1_ref[...]
        za = _dot(h1, wg_ref[:, j:j + COL_BLOCK]) + bg_ref[:, j:j + COL_BLOCK]
        zb = _dot(h1, wg_ref[:, d + j:d + j + COL_BLOCK]) + bg_ref[:, d + j:d + j + COL_BLOCK]
        pa = _dot(ya_ref[...], wco_ref[:, cs])
        pb = _dot(yb_ref[...], wgo_ref[:, cs])

        def epilogue():
            y_scr[:, cs] = (_sigmoid(za) * pa + _sigmoid(zb) * pb).astype(BF16)
        return epilogue

    _run_skewed([functools.partial(merge_stage, j) for j in range(0, d, COL_BLOCK)])
    o_ref[...] = x_ref[...] + ga1_ref[...] * _dot(y_scr[...], wo_ref[...])
    h2_scr[...] = _modulate(o_ref[...], g2_ref[...], sh2_ref[...], sc2_ref[...]).astype(BF16)

    parts = []

    def mlp_stage(j):
        u = _dot(h2_scr[...], wup_ref[:, j:j + FF_BLOCK])

        def epilogue():
            a = jnp.maximum(u, 0.0)
            parts.append(_dot((a * a).astype(BF16), wdn_ref[j:j + FF_BLOCK, :]))
        return epilogue

    _run_skewed([functools.partial(mlp_stage, j) for j in range(0, d_ff, FF_BLOCK)])
    x2 = o_ref[...] + ga2_ref[...] * functools.reduce(lambda a, b: a + b, parts)
    o_ref[...] = _rmsnorm(x2, gfin_ref[...])


def _out_call(x, h1, ya, yb, mod3, g2, gfin, wg, bg, wco, wgo, wo, wup, wdn):
    bsz, t, d = x.shape
    tm = OUT_TM
    tok = lambda: pl.BlockSpec((None, tm, d), lambda b, i: (b, i, 0))
    modrow = lambda col: pl.BlockSpec((None, 1, d), lambda b, i: (b, 0, col))
    pipelined = 2 * _nbytes((tm, d), F32) + 3 * _nbytes((tm, d), BF16) + 4 * _nbytes((1, d), F32)
    resident = sum(_nbytes(a.shape, a.dtype) for a in (g2, gfin, wg, bg, wco, wgo, wo, wup, wdn))
    scratch = 2 * _nbytes((tm, d), BF16)
    temps = 10 * _nbytes((tm, d), F32)
    return pl.pallas_call(
        _out_kernel,
        out_shape=jax.ShapeDtypeStruct((bsz, t, d), F32),
        grid=(bsz, t // tm),
        in_specs=[tok(), tok(), tok(), tok()] + [modrow(cidx) for cidx in (2, 3, 4, 5)] + [_resident()] * 9,
        out_specs=tok(),
        scratch_shapes=[pltpu.VMEM((tm, d), BF16), pltpu.VMEM((tm, d), BF16)],
        compiler_params=pltpu.CompilerParams(
            dimension_semantics=("parallel", "parallel"),
            vmem_limit_bytes=_vmem_limit(pipelined, resident + scratch, temps),
        ),
        name="out",
    )(x, h1, ya, yb, mod3, mod3, mod3, mod3, g2, gfin, wg, bg, wco, wgo, wo, wup, wdn)


def kernel(x, c, ctx, c_ctx, w_ada, b_ada, g_norm1, w_in, b_in, conv_w, w_conv_out, w_a2_f, b_a_f,
           w_a2_b, b_a_b, g_gla_norm, w_gla_out, w_o, g_norm2, w_up, w_down, g_final):
    depth = w_ada.shape[0]
    assert depth == 1, "only the single-layer block is implemented"
    bsz, t, d = x.shape
    qk = w_a2_f.shape[2]
    dk = qk // GLA_HEADS
    rank = w_a2_f.shape[1]
    gla_tile = GLA_GROUP * GLA_KERNEL_CHUNK
    assert t % INPROJ_TM == 0 and t % OUT_TM == 0 and INPROJ_TM % GRID_W == 0
    assert GLA_PAIR == 2 and GLA_GROUP % GLA_PAIR == 0 and GLA_PAIR * GLA_KERNEL_CHUNK == dk
    gla_pair = GLA_PAIR * GLA_KERNEL_CHUNK
    assert t % (GLA_LOCAL_UNROLL * gla_tile) == 0 and ctx.shape[1] % gla_tile == 0
    assert t % (2 * GLA_SCAN_UNROLL * gla_pair) == 0
    assert bsz + 1 <= MOD_ROWS and 2 * rank <= LR_PAD and 5 * d + 2 * qk + LR_PAD <= w_in.shape[2]

    cc = jnp.zeros((MOD_ROWS, d), F32).at[:bsz].set(c).at[bsz].set(c_ctx)
    mod = _ada_call(cc, w_ada[0], b_ada[0][None, :])
    mod3 = mod.reshape(MOD_ROWS, 1, N_MOD * d)

    o_lr = 5 * d + 2 * qk
    o_g = o_lr + 2 * rank
    w_main = w_in[0][:, :o_lr + LR_PAD].astype(BF16)
    b_main = b_in[0][None, :o_lr + LR_PAD]
    wg = w_in[0][:, o_g:].astype(BF16)
    bg = b_in[0][None, o_g:]
    wa2 = jnp.zeros((LR_PAD, 2 * qk), F32)
    wa2 = wa2.at[:rank, :qk].set(w_a2_f[0]).at[rank:2 * rank, qk:].set(w_a2_b[0]).astype(BF16)
    ba2 = jnp.concatenate([b_a_f[0], b_a_b[0]])[None, :]
    g1 = g_norm1[0][None, :]

    h1, ya, q, k, v, sr, gf, gb = _inproj_call(x, mod3, g1, w_main, b_main, conv_w[0], wa2, ba2, qk, dk)
    kc, vc, gfc, gbc = _inproj_ctx_call(ctx, mod3, bsz, g1, w_main, b_main, wa2, ba2, qk)
    yb = _gla_call(q, k, v, sr, gf, gb, kc, vc, gfc, gbc, g_gla_norm[0][None, :])
    return _out_call(x, h1, ya, yb, mod3, g_norm2[0][None, :], g_final[None, :], wg, bg,
                     w_conv_out[0].astype(BF16), w_gla_out[0].astype(BF16), w_o[0].astype(BF16),
                     w_up[0].astype(BF16), w_down[0].astype(BF16))
```

```python
import functools
from typing import NamedTuple

import jax
import jax.numpy as jnp
from jax import lax
from jax.experimental import pallas as pl
from jax.experimental.pallas import tpu as pltpu

F32 = jnp.float32
BF16 = jnp.bfloat16

GLA_HEADS = 4
GLA_TAU = 16.0
GRID_W = 64
N_MOD = 6
RMS_EPS = 1e-6
LOG2_E = 1.4426950408889634

V7X_LANES = 128
V7X_VMEM_LIMIT_BYTES = 60000 * 1024

LR_PAD = V7X_LANES
GLA_KERNEL_CHUNK = 64
GLA_DIAG_BLOCK = 8
GLA_MAX_EXPONENT = 96.0
GLA_DIAG_DECAY_FLOOR = -GLA_MAX_EXPONENT / (GLA_DIAG_BLOCK // 2)
GLA_PAIR = 2
GLA_MILD_DECAY_FLOOR = -GLA_MAX_EXPONENT / (GLA_PAIR * GLA_KERNEL_CHUNK // 2)
GLA_GROUP = 4
GLA_LOCAL_UNROLL = 4
GLA_SCAN_UNROLL = 16
INPROJ_TM = 1024
OUT_TM = 512
COL_BLOCK = 256
FF_BLOCK = 1024
MOD_ROWS = 16


def _vmem_limit(pipelined_bytes, resident_bytes, temp_bytes):
    need = 2 * pipelined_bytes + resident_bytes + temp_bytes
    return int(min(V7X_VMEM_LIMIT_BYTES, need))


def _nbytes(shape, dtype):
    n = 1
    for s in shape:
        n *= s
    return n * jnp.dtype(dtype).itemsize


def _dot(a, b):
    return jnp.dot(a, b, preferred_element_type=F32)


def _nt_dot(a, b):
    return lax.dot_general(a, b, (((1,), (1,)), ((), ())), preferred_element_type=F32)


def _run_skewed(stages):
    pending = None
    for stage in stages:
        epilogue = stage()
        if pending is not None:
            pending()
        pending = epilogue
    if pending is not None:
        pending()


def _sigmoid(x):
    return 1.0 / (1.0 + jnp.exp(-x))


def _log_sigmoid(x):
    return jnp.minimum(x, 0.0) - jnp.log(1.0 + jnp.exp(-jnp.abs(x)))


def _rmsnorm(x, g):
    return x * lax.rsqrt(jnp.mean(x * x, axis=-1, keepdims=True) + RMS_EPS) * g


def _modulate(x, g, shift, scale):
    return _rmsnorm(x, g) * (1.0 + scale) + shift


def _ada_kernel(c_ref, w_ref, b_ref, o_ref):
    c = c_ref[...]
    s = c * _sigmoid(c)
    o_ref[...] = _dot(s.astype(BF16), w_ref[...].astype(BF16)) + b_ref[...]


def _ada_call(cc, w_ada, b_ada):
    d = cc.shape[1]
    n_out = w_ada.shape[1]
    return pl.pallas_call(
        _ada_kernel,
        out_shape=jax.ShapeDtypeStruct((MOD_ROWS, n_out), F32),
        grid=(n_out // d,),
        in_specs=[
            pl.BlockSpec((MOD_ROWS, d), lambda j: (0, 0)),
            pl.BlockSpec((d, d), lambda j: (0, j)),
            pl.BlockSpec((1, d), lambda j: (0, j)),
        ],
        out_specs=pl.BlockSpec((MOD_ROWS, d), lambda j: (0, j)),
        compiler_params=pltpu.CompilerParams(
            dimension_semantics=("arbitrary",),
            vmem_limit_bytes=_vmem_limit(_nbytes((d, d), F32) + _nbytes((MOD_ROWS, 2 * d), F32),
                                         0, 4 * _nbytes((d, d), F32)),
        ),
        name="ada",
    )(cc, w_ada, b_ada)


def _log_decays(lr, wa2_ref, ba2_ref):
    xg = _dot(lr.astype(BF16), wa2_ref[...]) + ba2_ref[...]
    return _log_sigmoid(xg) * (LOG2_E / GLA_TAU)


def _inproj_kernel(x_ref, sh_ref, sc_ref, g1_ref, w_ref, b_ref, cw_ref, wa2_ref, ba2_ref,
                   h_ref, ya_ref, q_ref, k_ref, v_ref, sr_ref, gf_ref, gb_ref, *, q_scale):
    tm, d = x_ref.shape
    qk = q_ref.shape[1]
    h_ref[...] = _modulate(x_ref[...], g1_ref[...], sh_ref[...], sc_ref[...]).astype(BF16)

    col_in_row = lax.broadcasted_iota(jnp.int32, (tm, 1), 0) % GRID_W
    has_left = col_in_row != 0
    has_right = col_in_row != GRID_W - 1

    def proj(c0, width):
        return _dot(h_ref[...], w_ref[:, c0:c0 + width]) + b_ref[:, c0:c0 + width]

    o_q = 3 * d
    o_v = o_q + 2 * qk
    o_r = o_v + d
    o_lr = o_r + d

    def decay_stage():
        lr = proj(o_lr, LR_PAD)

        def epilogue():
            g = _log_decays(lr, wa2_ref, ba2_ref)
            gf_ref[...] = g[:, 0:qk]
            gb_ref[...] = g[:, qk:2 * qk]
        return epilogue

    def conv_stage(j):
        cs = slice(j, j + COL_BLOCK)
        xa = proj(j, COL_BLOCK)
        ba = proj(d + j, COL_BLOCK)
        ca = proj(2 * d + j, COL_BLOCK)

        def epilogue():
            u = ca * xa
            left = jnp.where(has_left, pltpu.roll(u, 1, 0), 0.0)
            right = jnp.where(has_right, pltpu.roll(u, tm - 1, 0), 0.0)
            y = left * cw_ref[0:1, cs] + u * cw_ref[1:2, cs] + right * cw_ref[2:3, cs]
            ya_ref[:, cs] = (ba * y).astype(BF16)
        return epilogue

    def swish_stage(j):
        r = proj(o_r + j, COL_BLOCK)

        def epilogue():
            sr_ref[:, j:j + COL_BLOCK] = (r * _sigmoid(r)).astype(BF16)
        return epilogue

    def qk_stage():
        q = proj(o_q, qk)
        k = proj(o_q + qk, qk)

        def epilogue():
            q_ref[...] = (q * q_scale).astype(BF16)
            k_ref[...] = k.astype(BF16)
        return epilogue

    def value_stage(j):
        v = proj(o_v + j, COL_BLOCK)

        def epilogue():
            v_ref[:, j:j + COL_BLOCK] = v.astype(BF16)
        return epilogue

    _run_skewed([decay_stage] + [functools.partial(conv_stage, j) for j in range(0, d, COL_BLOCK)]
                + [functools.partial(swish_stage, j) for j in range(0, d, COL_BLOCK)] + [qk_stage]
                + [functools.partial(value_stage, j) for j in range(0, d, COL_BLOCK)])


def _inproj_ctx_kernel(x_ref, sh_ref, sc_ref, g1_ref, w_ref, b_ref, wa2_ref, ba2_ref,
                       k_ref, v_ref, gf_ref, gb_ref):
    d = x_ref.shape[1]
    qk = k_ref.shape[1]
    h = _modulate(x_ref[...], g1_ref[...], sh_ref[...], sc_ref[...]).astype(BF16)
    o_k = 3 * d + qk
    k_ref[...] = (_dot(h, w_ref[:, o_k:o_k + qk]) + b_ref[:, o_k:o_k + qk]).astype(BF16)
    for j in range(0, d, COL_BLOCK):
        c0 = o_k + qk + j
        v_ref[:, j:j + COL_BLOCK] = (_dot(h, w_ref[:, c0:c0 + COL_BLOCK]) + b_ref[:, c0:c0 + COL_BLOCK]).astype(BF16)
    c0 = o_k + qk + 2 * d
    lr = _dot(h, w_ref[:, c0:c0 + LR_PAD]) + b_ref[:, c0:c0 + LR_PAD]
    g = _log_decays(lr, wa2_ref, ba2_ref)
    gf_ref[...] = g[:, 0:qk]
    gb_ref[...] = g[:, qk:2 * qk]


def _resident():
    return pl.BlockSpec(memory_space=pltpu.VMEM)


def _inproj_call(x, mod3, g1, w, b, cw, wa2, ba2, qk, dk):
    bsz, t, d = x.shape
    tm = INPROJ_TM
    tok = lambda n: pl.BlockSpec((None, tm, n), lambda b, i: (b, i, 0))
    modrow = lambda col: pl.BlockSpec((None, 1, d), lambda b, i: (b, 0, col))
    out_shapes = (
        jax.ShapeDtypeStruct((bsz, t, d), BF16),
        jax.ShapeDtypeStruct((bsz, t, d), BF16),
        jax.ShapeDtypeStruct((bsz, t, qk), BF16),
        jax.ShapeDtypeStruct((bsz, t, qk), BF16),
        jax.ShapeDtypeStruct((bsz, t, d), BF16),
        jax.ShapeDtypeStruct((bsz, t, d), BF16),
        jax.ShapeDtypeStruct((bsz, t, qk), F32),
        jax.ShapeDtypeStruct((bsz, t, qk), F32),
    )
    pipelined = (_nbytes((tm, d), F32) + 4 * _nbytes((tm, d), BF16) + 2 * _nbytes((tm, qk), BF16)
                 + 2 * _nbytes((tm, qk), F32) + 2 * _nbytes((1, d), F32))
    resident = sum(_nbytes(a.shape, a.dtype) for a in (g1, w, b, cw, wa2, ba2))
    temps = 24 * _nbytes((tm, COL_BLOCK), F32) + 3 * _nbytes((tm, 2 * qk), F32)
    return pl.pallas_call(
        functools.partial(_inproj_kernel, q_scale=dk ** -0.5),
        out_shape=out_shapes,
        grid=(bsz, t // tm),
        in_specs=[tok(d), modrow(0), modrow(1)] + [_resident()] * 6,
        out_specs=(tok(d), tok(d), tok(qk), tok(qk), tok(d), tok(d), tok(qk), tok(qk)),
        compiler_params=pltpu.CompilerParams(
            dimension_semantics=("parallel", "parallel"),
            vmem_limit_bytes=_vmem_limit(pipelined, resident, temps),
        ),
        name="inproj",
    )(x, mod3, mod3, g1, w, b, cw, wa2, ba2)


def _inproj_ctx_call(ctx, mod3, ctx_row, g1, w, b, wa2, ba2, qk):
    bsz, tc, d = ctx.shape
    rows = bsz * tc
    tm = min(rows, INPROJ_TM)
    assert rows % tm == 0
    tok = lambda n: pl.BlockSpec((tm, n), lambda i: (i, 0))
    modrow = lambda col: pl.BlockSpec((None, 1, d), lambda i: (ctx_row, 0, col))
    out_shapes = (
        jax.ShapeDtypeStruct((rows, qk), BF16),
        jax.ShapeDtypeStruct((rows, d), BF16),
        jax.ShapeDtypeStruct((rows, qk), F32),
        jax.ShapeDtypeStruct((rows, qk), F32),
    )
    pipelined = (_nbytes((tm, d), F32) + _nbytes((tm, d), BF16) + _nbytes((tm, qk), BF16)
                 + 2 * _nbytes((tm, qk), F32) + 2 * _nbytes((1, d), F32))
    resident = sum(_nbytes(a.shape, a.dtype) for a in (g1, w, b, wa2, ba2))
    temps = _nbytes((tm, d), BF16) + 6 * _nbytes((tm, COL_BLOCK), F32) + 3 * _nbytes((tm, 2 * qk), F32)
    outs = pl.pallas_call(
        _inproj_ctx_kernel,
        out_shape=out_shapes,
        grid=(rows // tm,),
        in_specs=[tok(d), modrow(0), modrow(1)] + [_resident()] * 5,
        out_specs=(tok(qk), tok(d), tok(qk), tok(qk)),
        compiler_params=pltpu.CompilerParams(
            dimension_semantics=("parallel",),
            vmem_limit_bytes=_vmem_limit(pipelined, resident, temps),
        ),
        name="inproj_ctx",
    )(ctx.reshape(rows, d), mod3, mod3, g1, w, b, wa2, ba2)
    return tuple(o.reshape(bsz, tc, o.shape[-1]) for o in outs)


class _ScanDir(NamedTuple):
    fwd: bool
    g_ref: object
    gc_ref: object
    chunk_sum: object
    diag_sum: object
    split_masks: tuple
    pair_mask: object
    diag_mask: object
    kt_scr: object
    qb_scr: object
    dm_scr: object
    s_scr: object


def _block_sums(sum_matrix, g):
    dk = g.shape[1]
    g_hi = g.astype(BF16)
    g_lo = (g - g_hi.astype(F32)).astype(BF16)
    bb = _dot(sum_matrix, jnp.concatenate([g_hi, g_lo], axis=1))
    return bb[:, :dk] + bb[:, dk:]


def _tile_decays(g, sum_matrix, fwd, n_chunks):
    rows, dk = g.shape
    c = rows // n_chunks
    b = _block_sums(sum_matrix, g).reshape(n_chunks, c, dk)
    return b, (b[:, c - 1:c, :] if fwd else b[:, 0:1, :])


def _block_row(x, block, row):
    rows, n = x.shape
    x3 = x.reshape(rows // block, block, n)
    return jnp.broadcast_to(x3[:, row:row + 1, :], x3.shape).reshape(rows, n)


def _pair_offsets(tot, fwd):
    n_chunks = tot.shape[0]
    zero = jnp.zeros_like(tot[0:1])
    on_odd = jnp.concatenate([t for p in range(0, n_chunks, GLA_PAIR) for t in (zero, tot[p:p + 1])], axis=0)
    on_even = jnp.concatenate([t for p in range(0, n_chunks, GLA_PAIR) for t in (tot[p + 1:p + 2], zero)], axis=0)
    pair_tot = jnp.concatenate([tot[p:p + 1] + tot[p + 1:p + 2] for p in range(0, n_chunks, GLA_PAIR)], axis=0)
    return (on_odd, on_even, pair_tot) if fwd else (on_even, on_odd, pair_tot)


def _lane_broadcast_column(row):
    n = row.shape[1]
    return jnp.broadcast_to(row, (n, n)).T


def _gla_kernel(q_ref, k_ref, v_ref, sr_ref, gf_ref, gb_ref, kc_ref, vc_ref, gfc_ref, gbc_ref, gn_ref,
                out_ref, a_scr, o_scr, ktf_scr, ktb_scr, qbf_scr, qbb_scr, dmf_scr, dmb_scr, sf_scr, sb_scr):
    c = GLA_KERNEL_CHUNK
    grp = GLA_GROUP
    tile = c * grp
    pair = c * GLA_PAIR
    pairs_per_tile = grp // GLA_PAIR
    diag = GLA_DIAG_BLOCK
    t, dk = q_ref.shape
    dv = v_ref.shape[1]
    n_pairs = t // pair
    row = lax.broadcasted_iota(jnp.int32, (tile, tile), 0)
    col = lax.broadcasted_iota(jnp.int32, (tile, tile), 1)
    halves = []
    half = pair // 2
    while half >= diag:
        halves.append(half)
        half //= 2

    def same_block(size):
        return (row // size) == (col // size)

    def scan_dir(fwd, g_ref, gc_ref, kt_scr, qb_scr, dm_scr, s_scr):
        seen = (row >= col) if fwd else (row <= col)
        split_masks = []
        for h in halves:
            later, earlier = (row % (2 * h) >= h), (col % (2 * h) < h)
            if not fwd:
                later, earlier = (row % (2 * h) < h), (col % (2 * h) >= h)
            split_masks.append((h, (same_block(2 * h) & later & earlier).astype(BF16)))
        return _ScanDir(fwd, g_ref, gc_ref, (same_block(c) & seen).astype(BF16),
                        (same_block(diag) & seen).astype(BF16), tuple(split_masks),
                        same_block(pair) & seen, same_block(diag) & seen, kt_scr, qb_scr, dm_scr, s_scr)

    dirs = (scan_dir(True, gf_ref, gfc_ref, ktf_scr, qbf_scr, dmf_scr, sf_scr),
            scan_dir(False, gb_ref, gbc_ref, ktb_scr, qbb_scr, dmb_scr, sb_scr))

    def key_side(k3, b, tot, k_off):
        kt = k3 * jnp.exp2(tot - b)
        return kt, kt * jnp.exp2(k_off)

    ctx_tiles = kc_ref.shape[0] // tile
    ctx_items = [(d, ti) for d in dirs for ti in range(ctx_tiles)]
    ctx_cums = [_tile_decays(d.gc_ref[ti * tile:(ti + 1) * tile, :], d.chunk_sum, d.fwd, grp)
                for d, ti in ctx_items]
    ctx_terms = {}
    for (d, ti), (b, tot) in zip(ctx_items, ctx_cums):
        _, k_off, pair_tot = _pair_offsets(tot, d.fwd)
        kc3 = kc_ref[ti * tile:(ti + 1) * tile, :].astype(F32).reshape(grp, c, dk)
        _, kt_pair = key_side(kc3, b, tot, k_off)
        kt_t = kt_pair.reshape(tile, dk).T.astype(BF16)
        decay = jnp.exp2(pair_tot)
        for p in range(pairs_per_tile):
            r0 = ti * tile + p * pair
            ctx_terms[(d.fwd, ti * pairs_per_tile + p)] = (
                _dot(kt_t[:, p * pair:(p + 1) * pair], vc_ref[r0:r0 + pair, :]),
                _lane_broadcast_column(decay[p]))
    for d in dirs:
        order = range(ctx_tiles * pairs_per_tile)
        s = jnp.zeros((dk, dv), F32)
        for p in (order if d.fwd else reversed(order)):
            inc, dm = ctx_terms[(d.fwd, p)]
            s = s * jnp.concatenate([dm] * (dv // dk), axis=1) + inc
        d.s_scr[...] = s

    def local_body(i, carry):
        tiles = [i * GLA_LOCAL_UNROLL + u for u in range(GLA_LOCAL_UNROLL)]
        rows = [pl.ds(pl.multiple_of(ti * tile, tile), tile) for ti in tiles]
        items = [(u, d) for u in range(GLA_LOCAL_UNROLL) for d in dirs]
        g_tiles = [d.g_ref[rows[u], :] for u, d in items]
        cums = [_tile_decays(g, d.chunk_sum, d.fwd, grp) for (u, d), g in zip(items, g_tiles)]
        q2 = [q_ref[sl, :].astype(F32) for sl in rows]
        k2 = [k_ref[sl, :].astype(F32) for sl in rows]
        chunk_terms = []
        for (u, d), (b, tot) in zip(items, cums):
            q_off, k_off, pair_tot = _pair_offsets(tot, d.fwd)
            q3, k3 = q2[u].reshape(grp, c, dk), k2[u].reshape(grp, c, dk)
            qb = q3 * jnp.exp2(b)
            kt, kt_pair = key_side(k3, b, tot, k_off)
            chunk_terms.append((qb, kt, q_off))
            d.qb_scr[rows[u], :] = (qb * jnp.exp2(q_off)).reshape(tile, dk).astype(BF16)
            kt_t = kt_pair.reshape(tile, dk).T.astype(BF16)
            decay = jnp.exp2(pair_tot)
            for p in range(pairs_per_tile):
                d.kt_scr[tiles[u] * pairs_per_tile + p] = kt_t[:, p * pair:(p + 1) * pair]
                d.dm_scr[tiles[u] * pairs_per_tile + p] = _lane_broadcast_column(decay[p])

        def store_scores(per_item):
            for u in range(GLA_LOCAL_UNROLL):
                scores = functools.reduce(lambda x, y: x + y, [s for (iu, _), s in zip(items, per_item) if iu == u])
                for p in range(pairs_per_tile):
                    a_scr[pl.ds(pl.multiple_of(tiles[u] * tile + p * pair, pair), pair), :] = (
                        scores[p * pair:(p + 1) * pair, p * pair:(p + 1) * pair])

        def pair_scores_mild():
            products = []
            for (u, d), (b, _), (_, _, q_off) in zip(items, cums, chunk_terms):
                flat = (b + q_off).reshape(tile, dk)
                mid = _block_row(flat, pair, pair // 2 - 1 if d.fwd else pair // 2)
                products.append(_nt_dot((q2[u] * jnp.exp2(flat - mid)).astype(BF16),
                                        (k2[u] * jnp.exp2(mid - flat)).astype(BF16)))
            store_scores([jnp.where(d.pair_mask, inner, 0.0).astype(BF16)
                          for (_, d), inner in zip(items, products)])

        def chunk_scores_any():
            pair_terms = [_nt_dot(qb.reshape(tile, dk).astype(BF16), kt.reshape(tile, dk).astype(BF16))
                          for qb, kt, _ in chunk_terms]
            diag_cums = [_block_sums(d.diag_sum, jnp.maximum(g, GLA_DIAG_DECAY_FLOOR))
                         for (u, d), g in zip(items, g_tiles)]
            products = []
            for (u, d), (b, _), bd in zip(items, cums, diag_cums):
                flat = b.reshape(tile, dk)
                split = []
                for h, mask in d.split_masks[1:]:
                    e = jnp.exp2(-jnp.abs(flat - _block_row(flat, 2 * h, h - 1 if d.fwd else h)))
                    split.append((_nt_dot((q2[u] * e).astype(BF16), (k2[u] * e).astype(BF16)), mask))
                mid = _block_row(bd, diag, diag // 2 - 1 if d.fwd else diag // 2)
                inner = _nt_dot((q2[u] * jnp.exp2(bd - mid)).astype(BF16),
                                (k2[u] * jnp.exp2(mid - bd)).astype(BF16))
                products.append((split, inner))
            per_item = []
            for (_, d), (split, inner), outer in zip(items, products, pair_terms):
                part = jnp.where(d.diag_mask, inner, 0.0).astype(BF16) + outer.astype(BF16) * d.split_masks[0][1]
                for product, mask in split:
                    part = part + product.astype(BF16) * mask
                per_item.append(part)
            store_scores(per_item)

        lowest = jnp.min(functools.reduce(jnp.minimum, g_tiles))
        pl.when(lowest >= GLA_MILD_DECAY_FLOOR)(pair_scores_mild)
        pl.when(lowest < GLA_MILD_DECAY_FLOOR)(chunk_scores_any)
        return carry

    lax.fori_loop(0, t // (tile * GLA_LOCAL_UNROLL), local_body, 0)

    def scan_body(i, carry, finish):
        steps = []
        for u in range(GLA_SCAN_UNROLL):
            for d in dirs:
                step = i * GLA_SCAN_UNROLL + u
                j = step if d.fwd else n_pairs - 1 - step
                steps.append((d, j, pl.ds(pl.multiple_of(j * pair, pair), pair)))
        increments = [_dot(d.kt_scr[j], v_ref[sl, :]) for d, j, sl in steps]
        local = [None if finish else _dot(a_scr[sl, :], v_ref[sl, :]) for _, _, sl in steps]
        outs = []
        for (d, j, sl), inc in zip(steps, increments):
            qb_scr, dm_scr, s_scr = d.qb_scr, d.dm_scr, d.s_scr
            s = s_scr[...]
            outs.append(_dot(qb_scr[sl, :], s.astype(BF16)))
            s_scr[...] = s * jnp.concatenate([dm_scr[j]] * (dv // dk), axis=1) + inc
        for (_, _, sl), o, loc in zip(steps, outs, local):
            if finish:
                y = _rmsnorm(o + o_scr[sl, :], gn_ref[...])
                out_ref[sl, :] = (y * sr_ref[sl, :].astype(F32)).astype(BF16)
            else:
                o_scr[sl, :] = o + loc
        return carry

    half = n_pairs // (2 * GLA_SCAN_UNROLL)
    lax.fori_loop(0, half, functools.partial(scan_body, finish=False), 0)
    lax.fori_loop(half, 2 * half, functools.partial(scan_body, finish=True), 0)


def _gla_call(q, k, v, sr, gf, gb, kc, vc, gfc, gbc, gn):
    bsz, t, qk = q.shape
    d = v.shape[2]
    tc = kc.shape[1]
    dk = qk // GLA_HEADS
    dv = d // GLA_HEADS
    pair = GLA_KERNEL_CHUNK * GLA_PAIR
    n_pairs = t // pair
    tile = GLA_KERNEL_CHUNK * GLA_GROUP
    seq = lambda rows, width: pl.BlockSpec((None, rows, width), lambda b, h: (b, 0, h))
    scratch_shapes = [
        pltpu.VMEM((t, pair), BF16),
        pltpu.VMEM((t, dv), F32),
        pltpu.VMEM((n_pairs, dk, pair), BF16), pltpu.VMEM((n_pairs, dk, pair), BF16),
        pltpu.VMEM((t, dk), BF16), pltpu.VMEM((t, dk), BF16),
        pltpu.VMEM((n_pairs, dk, dk), F32), pltpu.VMEM((n_pairs, dk, dk), F32),
        pltpu.VMEM((dk, dv), F32), pltpu.VMEM((dk, dv), F32),
    ]
    pipelined = (2 * _nbytes((t, dk), BF16) + 3 * _nbytes((t, dv), BF16) + 2 * _nbytes((t, dk), F32)
                 + _nbytes((tc, dk), BF16) + _nbytes((tc, dv), BF16) + 2 * _nbytes((tc, dk), F32)
                 + _nbytes((1, dv), F32))
    scratch = (_nbytes((t, pair), BF16) + _nbytes((t, dv), F32) + 2 * _nbytes((n_pairs, dk, pair), BF16)
               + 2 * _nbytes((t, dk), BF16) + 2 * _nbytes((n_pairs, dk, dk), F32) + 2 * _nbytes((dk, dv), F32))
    temps = 2 * GLA_LOCAL_UNROLL * (8 * _nbytes((tile, dk), F32) + 2 * _nbytes((tile, tile), F32))
    return pl.pallas_call(
        _gla_kernel,
        out_shape=jax.ShapeDtypeStruct((bsz, t, d), BF16),
        grid=(bsz, GLA_HEADS),
        in_specs=[seq(t, dk), seq(t, dk), seq(t, dv), seq(t, dv), seq(t, dk), seq(t, dk),
                  seq(tc, dk), seq(tc, dv), seq(tc, dk), seq(tc, dk),
                  pl.BlockSpec((1, dv), lambda b, h: (0, h))],
        out_specs=seq(t, dv),
        scratch_shapes=scratch_shapes,
        compiler_params=pltpu.CompilerParams(
            dimension_semantics=("parallel", "parallel"),
            vmem_limit_bytes=_vmem_limit(pipelined, scratch, temps),
        ),
        name="gla",
    )(q, k, v, sr, gf, gb, kc, vc, gfc, gbc, gn)


def _out_kernel(x_ref, h1_ref, ya_ref, yb_ref, ga1_ref, sh2_ref, sc2_ref, ga2_ref, g2_ref, gfin_ref,
                wg_ref, bg_ref, wco_ref, wgo_ref, wo_ref, wup_ref, wdn_ref, o_ref, y_scr, h2_scr):
    tm, d = x_ref.shape
    d_ff = wup_ref.shape[1]

    def merge_stage(j):
        cs = slice(j, j + COL_BLOCK)
        h1 = h1_ref[...]
        za = _dot(h1, wg_ref[:, j:j + COL_BLOCK]) + bg_ref[:, j:j + COL_BLOCK]
        zb = _dot(h1, wg_ref[:, d + j:d + j + COL_BLOCK]) + bg_ref[:, d + j:d + j + COL_BLOCK]
        pa = _dot(ya_ref[...], wco_ref[:, cs])
        pb = _dot(yb_ref[...], wgo_ref[:, cs])

        def epilogue():
            y_scr[:, cs] = (_sigmoid(za) * pa + _sigmoid(zb) * pb).astype(BF16)
        return epilogue

    _run_skewed([functools.partial(merge_stage, j) for j in range(0, d, COL_BLOCK)])
    o_ref[...] = x_ref[...] + ga1_ref[...] * _dot(y_scr[...], wo_ref[...])
    h2_scr[...] = _modulate(o_ref[...], g2_ref[...], sh2_ref[...], sc2_ref[...]).astype(BF16)

    parts = []

    def mlp_stage(j):
        u = _dot(h2_scr[...], wup_ref[:, j:j + FF_BLOCK])

        def epilogue():
            a = jnp.maximum(u, 0.0)
            parts.append(_dot((a * a).astype(BF16), wdn_ref[j:j + FF_BLOCK, :]))
        return epilogue

    _run_skewed([functools.partial(mlp_stage, j) for j in range(0, d_ff, FF_BLOCK)])
    x2 = o_ref[...] + ga2_ref[...] * functools.reduce(lambda a, b: a + b, parts)
    o_ref[...] = _rmsnorm(x2, gfin_ref[...])


def _out_call(x, h1, ya, yb, mod3, g2, gfin, wg, bg, wco, wgo, wo, wup, wdn):
    bsz, t, d = x.shape
    tm = OUT_TM
    tok = lambda: pl.BlockSpec((None, tm, d), lambda b, i: (b, i, 0))
    modrow = lambda col: pl.BlockSpec((None, 1, d), lambda b, i: (b, 0, col))
    pipelined = 2 * _nbytes((tm, d), F32) + 3 * _nbytes((tm, d), BF16) + 4 * _nbytes((1, d), F32)
    resident = sum(_nbytes(a.shape, a.dtype) for a in (g2, gfin, wg, bg, wco, wgo, wo, wup, wdn))
    scratch = 2 * _nbytes((tm, d), BF16)
    temps = 10 * _nbytes((tm, d), F32)
    return pl.pallas_call(
        _out_kernel,
        out_shape=jax.ShapeDtypeStruct((bsz, t, d), F32),
        grid=(bsz, t // tm),
        in_specs=[tok(), tok(), tok(), tok()] + [modrow(cidx) for cidx in (2, 3, 4, 5)] + [_resident()] * 9,
        out_specs=tok(),
        scratch_shapes=[pltpu.VMEM((tm, d), BF16), pltpu.VMEM((tm, d), BF16)],
        compiler_params=pltpu.CompilerParams(
            dimension_semantics=("parallel", "parallel"),
            vmem_limit_bytes=_vmem_limit(pipelined, resident + scratch, temps),
        ),
        name="out",
    )(x, h1, ya, yb, mod3, mod3, mod3, mod3, g2, gfin, wg, bg, wco, wgo, wo, wup, wdn)


def kernel(x, c, ctx, c_ctx, w_ada, b_ada, g_norm1, w_in, b_in, conv_w, w_conv_out, w_a2_f, b_a_f,
           w_a2_b, b_a_b, g_gla_norm, w_gla_out, w_o, g_norm2, w_up, w_down, g_final):
    depth = w_ada.shape[0]
    assert depth == 1, "only the single-layer block is implemented"
    bsz, t, d = x.shape
    qk = w_a2_f.shape[2]
    dk = qk // GLA_HEADS
    rank = w_a2_f.shape[1]
    gla_tile = GLA_GROUP * GLA_KERNEL_CHUNK
    assert t % INPROJ_TM == 0 and t % OUT_TM == 0 and INPROJ_TM % GRID_W == 0
    assert GLA_PAIR == 2 and GLA_GROUP % GLA_PAIR == 0 and GLA_PAIR * GLA_KERNEL_CHUNK == dk
    gla_pair = GLA_PAIR * GLA_KERNEL_CHUNK
    assert t % (GLA_LOCAL_UNROLL * gla_tile) == 0 and ctx.shape[1] % gla_tile == 0
    assert t % (2 * GLA_SCAN_UNROLL * gla_pair) == 0
    assert bsz + 1 <= MOD_ROWS and 2 * rank <= LR_PAD and 5 * d + 2 * qk + LR_PAD <= w_in.shape[2]

    cc = jnp.zeros((MOD_ROWS, d), F32).at[:bsz].set(c).at[bsz].set(c_ctx)
    mod = _ada_call(cc, w_ada[0], b_ada[0][None, :])
    mod3 = mod.reshape(MOD_ROWS, 1, N_MOD * d)

    o_lr = 5 * d + 2 * qk
    o_g = o_lr + 2 * rank
    w_main = w_in[0][:, :o_lr + LR_PAD].astype(BF16)
    b_main = b_in[0][None, :o_lr + LR_PAD]
    wg = w_in[0][:, o_g:].astype(BF16)
    bg = b_in[0][None, o_g:]
    wa2 = jnp.zeros((LR_PAD, 2 * qk), F32)
    wa2 = wa2.at[:rank, :qk].set(w_a2_f[0]).at[rank:2 * rank, qk:].set(w_a2_b[0]).astype(BF16)
    ba2 = jnp.concatenate([b_a_f[0], b_a_b[0]])[None, :]
    g1 = g_norm1[0][None, :]

    h1, ya, q, k, v, sr, gf, gb = _inproj_call(x, mod3, g1, w_main, b_main, conv_w[0], wa2, ba2, qk, dk)
    kc, vc, gfc, gbc = _inproj_ctx_call(ctx, mod3, bsz, g1, w_main, b_main, wa2, ba2, qk)
    yb = _gla_call(q, k, v, sr, gf, gb, kc, vc, gfc, gbc, g_gla_norm[0][None, :])
    return _out_call(x, h1, ya, yb, mod3, g_norm2[0][None, :], g_final[None, :], wg, bg,
                     w_conv_out[0].astype(BF16), w_gla_out[0].astype(BF16), w_o[0].astype(BF16),
                     w_up[0].astype(BF16), w_down[0].astype(BF16))
```
